```python
import jax, jax.numpy as jnp
from jax import lax
import numpy as np

D_MODEL = 2048
BATCH = 8
SEQ = 4096
DEPTH = 4

HEAD_DIM = 128
ATTN_WIDTH = D_MODEL // 2
N_Q_HEADS = ATTN_WIDTH // HEAD_DIM
N_KV_HEADS = N_Q_HEADS // 4
KV_WIDTH = N_KV_HEADS * HEAD_DIM
CONV_WIDTH = D_MODEL // 4
CONV_KERNEL = 31
SGU_WIDTH = D_MODEL // 4
SGU_HEAD_DIM = 128
SGU_HEADS = SGU_WIDTH // SGU_HEAD_DIM
CHUNK = 128
MIX_WIDTH = ATTN_WIDTH + CONV_WIDTH + SGU_WIDTH
IN_WIDTH = ATTN_WIDTH + 2 * KV_WIDTH + 2 * CONV_WIDTH + 2 * SGU_WIDTH
WINDOW = 128
BLOCK = 128
ROPE_THETA = 500000.0
ROT_DIM = HEAD_DIM // 4
D_FF = ((8 * D_MODEL // 3 + 255) // 256) * 256
EPS = 1e-6

kernel_name = "hybrid_parallel_groups_encoder"


def rms_norm(x, g):
    xf = x.astype(jnp.float32)
    y = xf * lax.rsqrt(jnp.mean(xf * xf, axis=-1, keepdims=True) + EPS)
    return (y * g.astype(jnp.float32)).astype(x.dtype)


def layer_norm(x, g, b):
    xf = x.astype(jnp.float32)
    mu = jnp.mean(xf, axis=-1, keepdims=True)
    var = jnp.mean(jnp.square(xf - mu), axis=-1, keepdims=True)
    y = (xf - mu) * lax.rsqrt(var + EPS)
    return (y * g.astype(jnp.float32) + b.astype(jnp.float32)).astype(x.dtype)


def rope_tables(seq):
    pos = jnp.arange(seq, dtype=jnp.float32)
    inv = ROPE_THETA ** (-jnp.arange(0, ROT_DIM, 2, dtype=jnp.float32) / ROT_DIM)
    ang = pos[:, None] * inv[None, :]
    return jnp.cos(ang), jnp.sin(ang)


def partial_rope(t, cos, sin):
    half = ROT_DIM // 2
    t1 = t[..., :half].astype(jnp.float32)
    t2 = t[..., half:ROT_DIM].astype(jnp.float32)
    c = cos[None, :, None, :]
    s = sin[None, :, None, :]
    rot = jnp.concatenate([t1 * c - t2 * s, t2 * c + t1 * s], axis=-1).astype(t.dtype)
    return jnp.concatenate([rot, t[..., ROT_DIM:]], axis=-1)


def windowed_gqa_sink(q, k, v, sink):
    B, S, H, Dh = q.shape
    G = k.shape[2]
    R = H // G
    nb = S // BLOCK
    pad = ((0, 0), (BLOCK, BLOCK), (0, 0), (0, 0))
    kp = jnp.pad(k, pad).reshape(B, nb + 2, BLOCK, G, Dh)
    vp = jnp.pad(v, pad).reshape(B, nb + 2, BLOCK, G, Dh)
    kw = jnp.concatenate([kp[:, :-2], kp[:, 1:-1], kp[:, 2:]], axis=2)
    vw = jnp.concatenate([vp[:, :-2], vp[:, 1:-1], vp[:, 2:]], axis=2)
    qb = q.reshape(B, nb, BLOCK, G, R, Dh)
    scale = 1.0 / float(np.sqrt(Dh))
    s = jnp.einsum('bnqgrd,bnkgd->bngrqk', qb, kw).astype(jnp.float32) * scale
    qpos = jnp.arange(S).reshape(nb, BLOCK)
    kpos = jnp.arange(nb)[:, None] * BLOCK - BLOCK + jnp.arange(3 * BLOCK)[None, :]
    valid = ((kpos[:, None, :] >= 0) & (kpos[:, None, :] < S)
             & (jnp.abs(qpos[:, :, None] - kpos[:, None, :]) <= WINDOW))
    s = jnp.where(valid[None, :, None, None], s, jnp.finfo(jnp.float32).min)
    sk = jnp.broadcast_to(sink.astype(jnp.float32).reshape(1, 1, G, R, 1, 1), s.shape[:-1] + (1,))
    p = jax.nn.softmax(jnp.concatenate([s, sk], axis=-1), axis=-1)[..., :-1]
    o = jnp.einsum('bngrqk,bnkgd->bnqgrd', p.astype(v.dtype), vw)
    return o.reshape(B, S, H * Dh)


def conformer_conv(a, gate, dw_w, dw_b, ln_g, ln_b):
    c = a * jax.nn.sigmoid(gate)
    C = c.shape[-1]
    c = lax.conv_general_dilated(
        c, dw_w[:, None, :].astype(c.dtype), window_strides=(1,),
        padding=[((CONV_KERNEL - 1) // 2, (CONV_KERNEL - 1) // 2)],
        dimension_numbers=('NWC', 'WIO', 'NWC'), feature_group_count=C) + dw_b
    c = layer_norm(c, ln_g, ln_b)
    return jax.nn.silu(c)


def spatial_gating(uv, ln_g, ln_b, w_s, b_s):
    B, S, _ = uv.shape
    uv = jax.nn.gelu(uv, approximate=False)
    u, v = jnp.split(uv, 2, axis=-1)
    v = layer_norm(v, ln_g, ln_b)
    vc = v.reshape(B, S // CHUNK, CHUNK, SGU_HEADS, SGU_HEAD_DIM)
    sp = jnp.einsum('hpq,bcqhe->bcphe', w_s, vc) + b_s.T[None, None, :, :, None]
    return u * sp.reshape(B, S, SGU_WIDTH)


def hybrid_mixer(h, w_in, sink, dw_w, dw_b, cln_g, cln_b, sln_g, sln_b, sgu_w, sgu_b, w_out, cos, sin):
    B, S, _ = h.shape
    z = h @ w_in
    offs = np.cumsum([ATTN_WIDTH, KV_WIDTH, KV_WIDTH, CONV_WIDTH, CONV_WIDTH]).tolist()
    q, k, v, ca, cg, uv = jnp.split(z, offs, axis=-1)
    q = partial_rope(q.reshape(B, S, N_Q_HEADS, HEAD_DIM), cos, sin)
    k = partial_rope(k.reshape(B, S, N_KV_HEADS, HEAD_DIM), cos, sin)
    v = v.reshape(B, S, N_KV_HEADS, HEAD_DIM)
    attn = windowed_gqa_sink(q, k, v, sink)
    conv = conformer_conv(ca, cg, dw_w, dw_b, cln_g, cln_b)
    sgu = spatial_gating(uv, sln_g, sln_b, sgu_w, sgu_b)
    return jnp.concatenate([attn.astype(h.dtype), conv.astype(h.dtype), sgu.astype(h.dtype)], axis=-1) @ w_out


def swiglu(h, w_gate, w_up, w_down):
    return (jax.nn.silu(h @ w_gate) * (h @ w_up)) @ w_down


def _fwd_setup_inputs(seed: int = 0) -> dict:
    key = jax.random.key(seed)
    ks = jax.random.split(key, 20)
    f32 = jnp.float32
    nrm = lambda k, shape, sc: jax.random.normal(k, shape, f32) * sc
    return {
        "x": nrm(ks[0], (BATCH, SEQ, D_MODEL), 1.0),
        "mix_norm_g": 1.0 + nrm(ks[1], (DEPTH, D_MODEL), 0.02),
        "w_in": nrm(ks[2], (DEPTH, D_MODEL, IN_WIDTH), D_MODEL ** -0.5),
        "sink": nrm(ks[3], (DEPTH, N_Q_HEADS), 0.5),
        "conv_dw_w": nrm(ks[4], (DEPTH, CONV_KERNEL, CONV_WIDTH), CONV_KERNEL ** -0.5),
        "conv_dw_b": nrm(ks[5], (DEPTH, CONV_WIDTH), 0.02),
        "conv_ln_g": 1.0 + nrm(ks[6], (DEPTH, CONV_WIDTH), 0.02),
        "conv_ln_b": nrm(ks[7], (DEPTH, CONV_WIDTH), 0.02),
        "sgu_ln_g": 1.0 + nrm(ks[8], (DEPTH, SGU_WIDTH), 0.02),
        "sgu_ln_b": nrm(ks[9], (DEPTH, SGU_WIDTH), 0.02),
        "sgu_w": nrm(ks[10], (DEPTH, SGU_HEADS, CHUNK, CHUNK), CHUNK ** -0.5),
        "sgu_b": 1.0 + nrm(ks[11], (DEPTH, SGU_HEADS, CHUNK), 0.02),
        "w_out": nrm(ks[12], (DEPTH, MIX_WIDTH, D_MODEL), MIX_WIDTH ** -0.5),
        "ffn_norm_g": 1.0 + nrm(ks[13], (DEPTH, D_MODEL), 0.02),
        "w_gate": nrm(ks[14], (DEPTH, D_MODEL, D_FF), D_MODEL ** -0.5),
        "w_up": nrm(ks[15], (DEPTH, D_MODEL, D_FF), D_MODEL ** -0.5),
        "w_down": nrm(ks[16], (DEPTH, D_FF, D_MODEL), D_FF ** -0.5),
        "final_norm_g": 1.0 + nrm(ks[17], (D_MODEL,), 0.02),
    }


def _fwd_reference(x, mix_norm_g, w_in, sink, conv_dw_w, conv_dw_b, conv_ln_g, conv_ln_b,
              sgu_ln_g, sgu_ln_b, sgu_w, sgu_b, w_out, ffn_norm_g, w_gate, w_up, w_down,
              final_norm_g):
    cos, sin = rope_tables(x.shape[1])
    for l in range(DEPTH):
        h = rms_norm(x, mix_norm_g[l])
        x = x + hybrid_mixer(h, w_in[l], sink[l], conv_dw_w[l], conv_dw_b[l], conv_ln_g[l],
                             conv_ln_b[l], sgu_ln_g[l], sgu_ln_b[l], sgu_w[l], sgu_b[l],
                             w_out[l], cos, sin)
        h = rms_norm(x, ffn_norm_g[l])
        x = x + swiglu(h, w_gate[l], w_up[l], w_down[l])
    return rms_norm(x, final_norm_g)


import jax as _jax
import jax.numpy as _jnp

TWIN_FORMAT = 'train_step'
FWD_PARAMS = ['x', 'mix_norm_g', 'w_in', 'sink', 'conv_dw_w', 'conv_dw_b', 'conv_ln_g', 'conv_ln_b', 'sgu_ln_g', 'sgu_ln_b', 'sgu_w', 'sgu_b', 'w_out', 'ffn_norm_g', 'w_gate', 'w_up', 'w_down', 'final_norm_g']
TWIN_WEIGHTS = ['mix_norm_g', 'w_in', 'sink', 'conv_dw_w', 'conv_dw_b', 'conv_ln_g', 'conv_ln_b', 'sgu_ln_g', 'sgu_ln_b', 'sgu_w', 'sgu_b', 'w_out', 'ffn_norm_g', 'w_gate', 'w_up', 'w_down', 'final_norm_g']
TWIN_DIFF_INPUT = 'x'
TWIN_INPUTS = ['x', 'mix_norm_g', 'w_in', 'sink', 'conv_dw_w', 'conv_dw_b', 'conv_ln_g', 'conv_ln_b', 'sgu_ln_g', 'sgu_ln_b', 'sgu_w', 'sgu_b', 'w_out', 'ffn_norm_g', 'w_gate', 'w_up', 'w_down', 'final_norm_g', 'loss_target', 'm_mix_norm_g', 'm_w_in', 'm_sink', 'm_conv_dw_w', 'm_conv_dw_b', 'm_conv_ln_g', 'm_conv_ln_b', 'm_sgu_ln_g', 'm_sgu_ln_b', 'm_sgu_w', 'm_sgu_b', 'm_w_out', 'm_ffn_norm_g', 'm_w_gate', 'm_w_up', 'm_w_down', 'm_final_norm_g', 'v_mix_norm_g', 'v_w_in', 'v_sink', 'v_conv_dw_w', 'v_conv_dw_b', 'v_conv_ln_g', 'v_conv_ln_b', 'v_sgu_ln_g', 'v_sgu_ln_b', 'v_sgu_w', 'v_sgu_b', 'v_w_out', 'v_ffn_norm_g', 'v_w_gate', 'v_w_up', 'v_w_down', 'v_final_norm_g']
TWIN_OUTPUTS = ['loss', 'grad_x', 'grad_mix_norm_g', 'grad_w_in', 'grad_sink', 'grad_conv_dw_w', 'grad_conv_dw_b', 'grad_conv_ln_g', 'grad_conv_ln_b', 'grad_sgu_ln_g', 'grad_sgu_ln_b', 'grad_sgu_w', 'grad_sgu_b', 'grad_w_out', 'grad_ffn_norm_g', 'grad_w_gate', 'grad_w_up', 'grad_w_down', 'grad_final_norm_g', 'delta_mix_norm_g', 'delta_w_in', 'delta_sink', 'delta_conv_dw_w', 'delta_conv_dw_b', 'delta_conv_ln_g', 'delta_conv_ln_b', 'delta_sgu_ln_g', 'delta_sgu_ln_b', 'delta_sgu_w', 'delta_sgu_b', 'delta_w_out', 'delta_ffn_norm_g', 'delta_w_gate', 'delta_w_up', 'delta_w_down', 'delta_final_norm_g', 'new_m_mix_norm_g', 'new_m_w_in', 'new_m_sink', 'new_m_conv_dw_w', 'new_m_conv_dw_b', 'new_m_conv_ln_g', 'new_m_conv_ln_b', 'new_m_sgu_ln_g', 'new_m_sgu_ln_b', 'new_m_sgu_w', 'new_m_sgu_b', 'new_m_w_out', 'new_m_ffn_norm_g', 'new_m_w_gate', 'new_m_w_up', 'new_m_w_down', 'new_m_final_norm_g', 'new_v_mix_norm_g', 'new_v_w_in', 'new_v_sink', 'new_v_conv_dw_w', 'new_v_conv_dw_b', 'new_v_conv_ln_g', 'new_v_conv_ln_b', 'new_v_sgu_ln_g', 'new_v_sgu_ln_b', 'new_v_sgu_w', 'new_v_sgu_b', 'new_v_w_out', 'new_v_ffn_norm_g', 'new_v_w_gate', 'new_v_w_up', 'new_v_w_down', 'new_v_final_norm_g']
TWIN_LEAF_KINDS = {'loss': 'loss', 'grad_x': 'grad_x', 'grad_mix_norm_g': 'grad_w', 'grad_w_in': 'grad_w', 'grad_sink': 'grad_w', 'grad_conv_dw_w': 'grad_w', 'grad_conv_dw_b': 'grad_w', 'grad_conv_ln_g': 'grad_w', 'grad_conv_ln_b': 'grad_w', 'grad_sgu_ln_g': 'grad_w', 'grad_sgu_ln_b': 'grad_w', 'grad_sgu_w': 'grad_w', 'grad_sgu_b': 'grad_w', 'grad_w_out': 'grad_w', 'grad_ffn_norm_g': 'grad_w', 'grad_w_gate': 'grad_w', 'grad_w_up': 'grad_w', 'grad_w_down': 'grad_w', 'grad_final_norm_g': 'grad_w', 'delta_mix_norm_g': 'delta_w', 'delta_w_in': 'delta_w', 'delta_sink': 'delta_w', 'delta_conv_dw_w': 'delta_w', 'delta_conv_dw_b': 'delta_w', 'delta_conv_ln_g': 'delta_w', 'delta_conv_ln_b': 'delta_w', 'delta_sgu_ln_g': 'delta_w', 'delta_sgu_ln_b': 'delta_w', 'delta_sgu_w': 'delta_w', 'delta_sgu_b': 'delta_w', 'delta_w_out': 'delta_w', 'delta_ffn_norm_g': 'delta_w', 'delta_w_gate': 'delta_w', 'delta_w_up': 'delta_w', 'delta_w_down': 'delta_w', 'delta_final_norm_g': 'delta_w', 'new_m_mix_norm_g': 'new_m', 'new_m_w_in': 'new_m', 'new_m_sink': 'new_m', 'new_m_conv_dw_w': 'new_m', 'new_m_conv_dw_b': 'new_m', 'new_m_conv_ln_g': 'new_m', 'new_m_conv_ln_b': 'new_m', 'new_m_sgu_ln_g': 'new_m', 'new_m_sgu_ln_b': 'new_m', 'new_m_sgu_w': 'new_m', 'new_m_sgu_b': 'new_m', 'new_m_w_out': 'new_m', 'new_m_ffn_norm_g': 'new_m', 'new_m_w_gate': 'new_m', 'new_m_w_up': 'new_m', 'new_m_w_down': 'new_m', 'new_m_final_norm_g': 'new_m', 'new_v_mix_norm_g': 'new_v', 'new_v_w_in': 'new_v', 'new_v_sink': 'new_v', 'new_v_conv_dw_w': 'new_v', 'new_v_conv_dw_b': 'new_v', 'new_v_conv_ln_g': 'new_v', 'new_v_conv_ln_b': 'new_v', 'new_v_sgu_ln_g': 'new_v', 'new_v_sgu_ln_b': 'new_v', 'new_v_sgu_w': 'new_v', 'new_v_sgu_b': 'new_v', 'new_v_w_out': 'new_v', 'new_v_ffn_norm_g': 'new_v', 'new_v_w_gate': 'new_v', 'new_v_w_up': 'new_v', 'new_v_w_down': 'new_v', 'new_v_final_norm_g': 'new_v'}


def _forward(args):
    return _fwd_reference(*[args[k] for k in FWD_PARAMS])


def _output_shape():
    def fwd():
        inp = _fwd_setup_inputs(0)
        return _fwd_reference(*[inp[k] for k in FWD_PARAMS])
    out = _jax.eval_shape(fwd)
    return out.shape, out.dtype

N_MICROBATCH = 1
ADAM_LR = 0.001
ADAM_B1 = 0.9
ADAM_B2 = 0.999
ADAM_EPS = 1e-08
ADAM_WD = 0.01
ADAM_STEP = 10
PER_EXAMPLE_BATCH_AXIS = {'x': 0, 'loss_target': 0}
SHARED_INPUTS = []
_WEIGHT_DTYPES = {'mix_norm_g': _jnp.float32, 'w_in': _jnp.float32, 'sink': _jnp.float32, 'conv_dw_w': _jnp.float32, 'conv_dw_b': _jnp.float32, 'conv_ln_g': _jnp.float32, 'conv_ln_b': _jnp.float32, 'sgu_ln_g': _jnp.float32, 'sgu_ln_b': _jnp.float32, 'sgu_w': _jnp.float32, 'sgu_b': _jnp.float32, 'w_out': _jnp.float32, 'ffn_norm_g': _jnp.float32, 'w_gate': _jnp.float32, 'w_up': _jnp.float32, 'w_down': _jnp.float32, 'final_norm_g': _jnp.float32}
MOMENT_SCALE = {'mix_norm_g': 5.819191e-02, 'w_in': 4.334356e-02, 'sink': 7.178285e-04, 'conv_dw_w': 4.983341e-02, 'conv_dw_b': 1.179036e-01, 'conv_ln_g': 6.386175e-02, 'conv_ln_b': 6.485847e-02, 'sgu_ln_g': 6.023368e-02, 'sgu_ln_b': 5.182484e-02, 'sgu_w': 5.238284e-02, 'sgu_b': 5.338262e-02, 'w_out': 4.784861e-02, 'ffn_norm_g': 5.744549e-02, 'w_gate': 2.499820e-02, 'w_up': 2.426022e-02, 'w_down': 4.023984e-02, 'final_norm_g': 1.604400e+01}


def _to_microbatches(a, axis):
    t = _jnp.moveaxis(a, axis, 0)
    t = t.reshape((N_MICROBATCH, t.shape[0] // N_MICROBATCH) + t.shape[1:])
    return _jnp.moveaxis(t, 1, axis + 1)


def setup_inputs(seed: int = 0) -> dict:
    inp = _fwd_setup_inputs(seed)
    key = _jax.random.fold_in(_jax.random.key(seed), 7919)
    shape, _ = _output_shape()
    out = dict(inp)
    out["loss_target"] = _jax.random.normal(_jax.random.fold_in(key, 0), shape, _jnp.float32)
    for i, name in enumerate(TWIN_WEIGHTS):
        w = inp[name].astype(_jnp.float32)
        if MOMENT_SCALE is None:
            s = _jnp.sqrt(_jnp.mean(_jnp.square(w)) + 1e-30)
        else:
            s = MOMENT_SCALE[name]
        km, kv = _jax.random.split(_jax.random.fold_in(key, i + 1))
        out[name] = w
        out["m_" + name] = s * _jax.random.normal(km, w.shape, _jnp.float32)
        out["v_" + name] = (s * s) * _jax.random.uniform(kv, w.shape, _jnp.float32, 0.5, 1.5)
    if N_MICROBATCH > 1:
        for name, axis in PER_EXAMPLE_BATCH_AXIS.items():
            out[name] = _to_microbatches(out[name], axis)
    return {'x': out['x'], 'mix_norm_g': out['mix_norm_g'], 'w_in': out['w_in'], 'sink': out['sink'], 'conv_dw_w': out['conv_dw_w'], 'conv_dw_b': out['conv_dw_b'], 'conv_ln_g': out['conv_ln_g'], 'conv_ln_b': out['conv_ln_b'], 'sgu_ln_g': out['sgu_ln_g'], 'sgu_ln_b': out['sgu_ln_b'], 'sgu_w': out['sgu_w'], 'sgu_b': out['sgu_b'], 'w_out': out['w_out'], 'ffn_norm_g': out['ffn_norm_g'], 'w_gate': out['w_gate'], 'w_up': out['w_up'], 'w_down': out['w_down'], 'final_norm_g': out['final_norm_g'], 'loss_target': out['loss_target'], 'm_mix_norm_g': out['m_mix_norm_g'], 'm_w_in': out['m_w_in'], 'm_sink': out['m_sink'], 'm_conv_dw_w': out['m_conv_dw_w'], 'm_conv_dw_b': out['m_conv_dw_b'], 'm_conv_ln_g': out['m_conv_ln_g'], 'm_conv_ln_b': out['m_conv_ln_b'], 'm_sgu_ln_g': out['m_sgu_ln_g'], 'm_sgu_ln_b': out['m_sgu_ln_b'], 'm_sgu_w': out['m_sgu_w'], 'm_sgu_b': out['m_sgu_b'], 'm_w_out': out['m_w_out'], 'm_ffn_norm_g': out['m_ffn_norm_g'], 'm_w_gate': out['m_w_gate'], 'm_w_up': out['m_w_up'], 'm_w_down': out['m_w_down'], 'm_final_norm_g': out['m_final_norm_g'], 'v_mix_norm_g': out['v_mix_norm_g'], 'v_w_in': out['v_w_in'], 'v_sink': out['v_sink'], 'v_conv_dw_w': out['v_conv_dw_w'], 'v_conv_dw_b': out['v_conv_dw_b'], 'v_conv_ln_g': out['v_conv_ln_g'], 'v_conv_ln_b': out['v_conv_ln_b'], 'v_sgu_ln_g': out['v_sgu_ln_g'], 'v_sgu_ln_b': out['v_sgu_ln_b'], 'v_sgu_w': out['v_sgu_w'], 'v_sgu_b': out['v_sgu_b'], 'v_w_out': out['v_w_out'], 'v_ffn_norm_g': out['v_ffn_norm_g'], 'v_w_gate': out['v_w_gate'], 'v_w_up': out['v_w_up'], 'v_w_down': out['v_w_down'], 'v_final_norm_g': out['v_final_norm_g']}


def _loss(weights, diff, rest, loss_target):
    with _jax.named_scope("forward"):
        args = {**rest, TWIN_DIFF_INPUT: diff, **{k: w.astype(_WEIGHT_DTYPES[k]) for k, w in weights.items()}}
        y = _forward(args)
    with _jax.named_scope("loss_head"):
        err = _jnp.square(y.astype(_jnp.float32) - loss_target)
        return 0.5 * _jnp.sum(_jnp.mean(err, axis=-1)) if err.ndim else 0.5 * err


def _adamw(w, g, m, v):
    m = ADAM_B1 * m + (1.0 - ADAM_B1) * g
    v = ADAM_B2 * v + (1.0 - ADAM_B2) * _jnp.square(g)
    m_hat = m / (1.0 - ADAM_B1 ** ADAM_STEP)
    v_hat = v / (1.0 - ADAM_B2 ** ADAM_STEP)
    delta = -ADAM_LR * (m_hat / (_jnp.sqrt(v_hat) + ADAM_EPS) + ADAM_WD * w)
    return delta, m, v


def reference(x, mix_norm_g, w_in, sink, conv_dw_w, conv_dw_b, conv_ln_g, conv_ln_b, sgu_ln_g, sgu_ln_b, sgu_w, sgu_b, w_out, ffn_norm_g, w_gate, w_up, w_down, final_norm_g, loss_target, m_mix_norm_g, m_w_in, m_sink, m_conv_dw_w, m_conv_dw_b, m_conv_ln_g, m_conv_ln_b, m_sgu_ln_g, m_sgu_ln_b, m_sgu_w, m_sgu_b, m_w_out, m_ffn_norm_g, m_w_gate, m_w_up, m_w_down, m_final_norm_g, v_mix_norm_g, v_w_in, v_sink, v_conv_dw_w, v_conv_dw_b, v_conv_ln_g, v_conv_ln_b, v_sgu_ln_g, v_sgu_ln_b, v_sgu_w, v_sgu_b, v_w_out, v_ffn_norm_g, v_w_gate, v_w_up, v_w_down, v_final_norm_g):
    given = dict(x=x, mix_norm_g=mix_norm_g, w_in=w_in, sink=sink, conv_dw_w=conv_dw_w, conv_dw_b=conv_dw_b, conv_ln_g=conv_ln_g, conv_ln_b=conv_ln_b, sgu_ln_g=sgu_ln_g, sgu_ln_b=sgu_ln_b, sgu_w=sgu_w, sgu_b=sgu_b, w_out=w_out, ffn_norm_g=ffn_norm_g, w_gate=w_gate, w_up=w_up, w_down=w_down, final_norm_g=final_norm_g, loss_target=loss_target, m_mix_norm_g=m_mix_norm_g, m_w_in=m_w_in, m_sink=m_sink, m_conv_dw_w=m_conv_dw_w, m_conv_dw_b=m_conv_dw_b, m_conv_ln_g=m_conv_ln_g, m_conv_ln_b=m_conv_ln_b, m_sgu_ln_g=m_sgu_ln_g, m_sgu_ln_b=m_sgu_ln_b, m_sgu_w=m_sgu_w, m_sgu_b=m_sgu_b, m_w_out=m_w_out, m_ffn_norm_g=m_ffn_norm_g, m_w_gate=m_w_gate, m_w_up=m_w_up, m_w_down=m_w_down, m_final_norm_g=m_final_norm_g, v_mix_norm_g=v_mix_norm_g, v_w_in=v_w_in, v_sink=v_sink, v_conv_dw_w=v_conv_dw_w, v_conv_dw_b=v_conv_dw_b, v_conv_ln_g=v_conv_ln_g, v_conv_ln_b=v_conv_ln_b, v_sgu_ln_g=v_sgu_ln_g, v_sgu_ln_b=v_sgu_ln_b, v_sgu_w=v_sgu_w, v_sgu_b=v_sgu_b, v_w_out=v_w_out, v_ffn_norm_g=v_ffn_norm_g, v_w_gate=v_w_gate, v_w_up=v_w_up, v_w_down=v_w_down, v_final_norm_g=v_final_norm_g)
    weights = {n: given[n] for n in TWIN_WEIGHTS}
    shared = {n: given[n] for n in SHARED_INPUTS}
    per_example = {n: given[n] for n in ['x']}
    grad_fn = _jax.value_and_grad(_loss, argnums=(0, 1))

    def one_microbatch(ex, loss_target):
        ex = dict(ex)
        diff = ex.pop(TWIN_DIFF_INPUT)
        return grad_fn(weights, diff, {**shared, **ex}, loss_target)

    if N_MICROBATCH == 1:
        loss, (grad_w, grad_x) = one_microbatch(per_example, given["loss_target"])
    else:
        def body(carry, xs):
            loss_sum, grad_sum = carry
            l_k, (gw_k, gx_k) = one_microbatch(xs[0], xs[1])
            with _jax.named_scope("update"):
                return (loss_sum + l_k, _jax.tree.map(_jnp.add, grad_sum, gw_k)), gx_k

        init = (_jnp.zeros((), _jnp.float32), _jax.tree.map(_jnp.zeros_like, weights))
        (loss, grad_w), grad_x = _jax.lax.scan(body, init, (per_example, given["loss_target"]))
    with _jax.named_scope("update"):
        delta_w, new_m, new_v = {}, {}, {}
        for n in TWIN_WEIGHTS:
            delta_w[n], new_m[n], new_v[n] = _adamw(weights[n], grad_w[n], given["m_" + n], given["v_" + n])
    return (loss, grad_x, *[grad_w[n] for n in TWIN_WEIGHTS], *[delta_w[n] for n in TWIN_WEIGHTS],
            *[new_m[n] for n in TWIN_WEIGHTS], *[new_v[n] for n in TWIN_WEIGHTS])
```

```python
import functools
import math

import jax
import jax.numpy as jnp
from jax import lax
from jax.experimental import pallas as pl
from jax.experimental.pallas import tpu as pltpu

F32 = jnp.float32
BF16 = jnp.bfloat16

HEAD_DIM = 128
Q_PER_KV = 4
WINDOW = 128
BLOCK = 128
ROT_DIM = 32
ROPE_THETA = 500000.0
CONV_KERNEL = 31
CONV_PAD = (CONV_KERNEL - 1) // 2
CHUNK = 128
EPS = 1e-6

ADAM_LR = 0.001
ADAM_B1 = 0.9
ADAM_B2 = 0.999
ADAM_EPS = 1e-08
ADAM_WD = 0.01
ADAM_STEP = 10

N_DEV = 8
MESH_AXES = ("x", "y", "c")
VMEM_LIMIT_BYTES = 56 * 1024 * 1024
HALO_ROWS = 16
ADAMW_TILE_ELEMS = 256 * 1024
MESH_ID = pl.DeviceIdType.MESH


class Cfg:
    def __init__(self, d_model, seq):
        self.d = d_model
        self.t = seq
        self.attn = d_model // 2
        self.hq = self.attn // HEAD_DIM
        self.g = self.hq // Q_PER_KV
        self.kv = self.g * HEAD_DIM
        self.conv = d_model // 4
        self.sgu = d_model // 4
        self.sh = self.sgu // HEAD_DIM
        self.off_k = self.attn
        self.off_v = self.attn + self.kv
        self.off_ca = self.attn + 2 * self.kv
        self.off_cg = self.off_ca + self.conv
        self.off_u = self.off_cg + self.conv
        self.off_sv = self.off_u + self.sgu
        self.inw = self.off_sv + self.sgu


def _pick(dim, prefs):
    for p in prefs:
        if dim % p == 0:
            return p
    return dim


def _cparams(*sem):
    return pltpu.CompilerParams(dimension_semantics=sem, vmem_limit_bytes=VMEM_LIMIT_BYTES)


def _sigmoid(v):
    return 1.0 / (1.0 + jnp.exp(-v))


_DN = {
    "nn": (((1,), (0,)), ((), ())),
    "nt": (((1,), (1,)), ((), ())),
    "tn": (((0,), (0,)), ((), ())),
}


def _dot(a, b, mode):
    return lax.dot_general(a, b, _DN[mode], preferred_element_type=F32)


def _matmul(a, b, *, mode, tm, tn, tk, epilogue, out_dtypes, extras=(), name):
    if mode == "tn":
        kdim, m = a.shape
        n = b.shape[1]
    elif mode == "nn":
        m, kdim = a.shape
        n = b.shape[1]
    else:
        m, kdim = a.shape
        n = b.shape[0]
    assert m % tm == 0 and n % tn == 0 and kdim % tk == 0, (name, m, n, kdim, tm, tn, tk)
    gm, gn, gk = m // tm, n // tn, kdim // tk
    if mode == "tn":
        a_spec = pl.BlockSpec((tk, tm), lambda i, j, k: (k, i))
    else:
        a_spec = pl.BlockSpec((tm, tk), lambda i, j, k: (i, k))
    if mode == "nt":
        b_spec = pl.BlockSpec((tn, tk), lambda i, j, k: (j, k))
    else:
        b_spec = pl.BlockSpec((tk, tn), lambda i, j, k: (k, j))
    tile = pl.BlockSpec((tm, tn), lambda i, j, k: (i, j))
    ne, no = len(extras), len(out_dtypes)

    def body(a_ref, b_ref, *rest):
        ex = rest[:ne]
        outs = rest[ne:ne + no]
        part = _dot(a_ref[...], b_ref[...], mode)

        def finish(acc):
            vals = epilogue(acc, *[e[...] for e in ex])
            for o_ref, val in zip(outs, vals):
                o_ref[...] = val.astype(o_ref.dtype)

        if gk == 1:
            finish(part)
        else:
            acc_ref = rest[ne + no]
            k = pl.program_id(2)

            @pl.when(k == 0)
            def _():
                acc_ref[...] = part

            if gk > 2:
                @pl.when((k > 0) & (k < gk - 1))
                def _():
                    acc_ref[...] += part

            @pl.when(k == gk - 1)
            def _():
                finish(acc_ref[...] + part)

    return pl.pallas_call(
        body,
        grid=(gm, gn, gk),
        in_specs=[a_spec, b_spec] + [tile] * ne,
        out_specs=[tile] * no,
        out_shape=[jax.ShapeDtypeStruct((m, n), dt) for dt in out_dtypes],
        scratch_shapes=[pltpu.VMEM((tm, tn), F32)] if gk > 1 else [],
        compiler_params=_cparams("parallel", "parallel", "arbitrary"),
        name=name,
    )(a, b, *extras)


def _ep_plain(acc):
    return (acc,)


def _ep_add(acc, r):
    return (r.astype(F32) + acc,)


def _ep_swiglu_bwd(dact, gate, up):
    gate = gate.astype(F32)
    up = up.astype(F32)
    s = _sigmoid(gate)
    silu = gate * s
    dgate = dact * up * (s * (1.0 + gate * (1.0 - s)))
    dup = dact * silu
    return dgate, dup


def _ffn_up(h, wgt, wut, *, tm, tn, name):
    m, kdim = h.shape
    n = wgt.shape[0]

    def body(h_ref, g_ref, u_ref, gate_ref, up_ref, act_ref):
        hv = h_ref[...]
        gate = _dot(hv, g_ref[...], "nt")
        up = _dot(hv, u_ref[...], "nt")
        gate_ref[...] = gate.astype(BF16)
        up_ref[...] = up.astype(BF16)
        act_ref[...] = (gate * _sigmoid(gate) * up).astype(BF16)

    tile = pl.BlockSpec((tm, tn), lambda i, j: (i, j))
    wspec = pl.BlockSpec((tn, kdim), lambda i, j: (j, 0))
    return pl.pallas_call(
        body,
        grid=(m // tm, n // tn),
        in_specs=[pl.BlockSpec((tm, kdim), lambda i, j: (i, 0)), wspec, wspec],
        out_specs=[tile] * 3,
        out_shape=[jax.ShapeDtypeStruct((m, n), BF16)] * 3,
        compiler_params=_cparams("parallel", "parallel"),
        name=name,
    )(h, wgt, wut)


def _rms_fwd(x, g, *, tr, name):
    t, d = x.shape

    def body(x_ref, g_ref, h_ref):
        xv = x_ref[...]
        r = lax.rsqrt(jnp.mean(xv * xv, axis=-1, keepdims=True) + EPS)
        h_ref[...] = (xv * r * g_ref[...]).astype(BF16)

    row = pl.BlockSpec((tr, d), lambda i: (i, 0))
    return pl.pallas_call(
        body,
        grid=(t // tr,),
        in_specs=[row, pl.BlockSpec((1, d), lambda i: (0, 0))],
        out_specs=row,
        out_shape=jax.ShapeDtypeStruct((t, d), BF16),
        compiler_params=_cparams("parallel"),
        name=name,
    )(x, g)


def _rms_bwd_math(dy, xv, g):
    r = lax.rsqrt(jnp.mean(xv * xv, axis=-1, keepdims=True) + EPS)
    xh = xv * r
    dg = jnp.sum(dy * xh, axis=0, keepdims=True)
    dyg = dy * g
    dx = r * (dyg - xh * jnp.mean(dyg * xh, axis=-1, keepdims=True))
    return dx, dg


def _rms_bwd(dh, x, g, dres, *, tr, name):
    t, d = x.shape

    def body(dh_ref, x_ref, g_ref, dres_ref, dx_ref, dxb_ref, dg_ref):
        dx, dg = _rms_bwd_math(dh_ref[...], x_ref[...], g_ref[...])
        dx = dx + dres_ref[...]
        dx_ref[...] = dx
        dxb_ref[...] = dx.astype(BF16)

        @pl.when(pl.program_id(0) == 0)
        def _():
            dg_ref[...] = jnp.zeros_like(dg_ref)

        dg_ref[0:1, :] += dg

    row = pl.BlockSpec((tr, d), lambda i: (i, 0))
    vec = pl.BlockSpec((1, d), lambda i: (0, 0))
    return pl.pallas_call(
        body,
        grid=(t // tr,),
        in_specs=[row, row, vec, row],
        out_specs=[row, row, pl.BlockSpec((8, d), lambda i: (0, 0))],
        out_shape=[jax.ShapeDtypeStruct((t, d), F32), jax.ShapeDtypeStruct((t, d), BF16),
                   jax.ShapeDtypeStruct((8, d), F32)],
        compiler_params=_cparams("arbitrary"),
        name=name,
    )(dh, x, g, dres)


def _loss_head(x, g, target, *, tr, name):
    t, d = x.shape

    def body(x_ref, g_ref, t_ref, dx_ref, dxb_ref, st_ref):
        xv = x_ref[...]
        gv = g_ref[...]
        r = lax.rsqrt(jnp.mean(xv * xv, axis=-1, keepdims=True) + EPS)
        err = xv * r * gv - t_ref[...]
        sq = jnp.sum(jnp.sum(err * err, axis=1, keepdims=True), axis=0, keepdims=True)
        dx, dg = _rms_bwd_math(err * (1.0 / d), xv, gv)
        dx_ref[...] = dx
        dxb_ref[...] = dx.astype(BF16)

        @pl.when(pl.program_id(0) == 0)
        def _():
            st_ref[...] = jnp.zeros_like(st_ref)

        st_ref[0:1, :] += dg
        st_ref[1:2, :] += jnp.broadcast_to(sq * (0.5 / d), (1, d))

    row = pl.BlockSpec((tr, d), lambda i: (i, 0))
    return pl.pallas_call(
        body,
        grid=(t // tr,),
        in_specs=[row, pl.BlockSpec((1, d), lambda i: (0, 0)), row],
        out_specs=[row, row, pl.BlockSpec((8, d), lambda i: (0, 0))],
        out_shape=[jax.ShapeDtypeStruct((t, d), F32), jax.ShapeDtypeStruct((t, d), BF16),
                   jax.ShapeDtypeStruct((8, d), F32)],
        compiler_params=_cparams("arbitrary"),
        name=name,
    )(x, g, target)


def _rope_tables(t):
    half = ROT_DIM // 2
    pos = jnp.arange(t, dtype=F32)
    inv = ROPE_THETA ** (-jnp.arange(0, ROT_DIM, 2, dtype=F32) / ROT_DIM)
    ang = pos[:, None] * inv[None, :]
    cos, sin = jnp.cos(ang), jnp.sin(ang)
    rest = HEAD_DIM - ROT_DIM
    c = jnp.concatenate([cos, cos, jnp.ones((t, rest), F32)], axis=1)
    sa = jnp.concatenate([-sin, jnp.zeros((t, HEAD_DIM - half), F32)], axis=1)
    sb = jnp.concatenate([jnp.zeros((t, half), F32), sin, jnp.zeros((t, rest), F32)], axis=1)
    return c, sa, sb


def _rope_apply(v, c, sa, sb):
    half = ROT_DIM // 2
    return v * c + pltpu.roll(v, HEAD_DIM - half, axis=1) * sa + pltpu.roll(v, half, axis=1) * sb


def _rope_apply_t(dv, c, sa, sb):
    half = ROT_DIM // 2
    return dv * c + pltpu.roll(dv * sa, half, axis=1) + pltpu.roll(dv * sb, HEAD_DIM - half, axis=1)


def _rope_fwd(z, tabs, ncol, *, tr, name):
    t = z.shape[0]

    def body(z_ref, c_ref, sa_ref, sb_ref, o_ref):
        o_ref[...] = _rope_apply(z_ref[...].astype(F32), c_ref[...], sa_ref[...], sb_ref[...]).astype(BF16)

    blk = pl.BlockSpec((tr, HEAD_DIM), lambda i, j: (i, j))
    tab = pl.BlockSpec((tr, HEAD_DIM), lambda i, j: (i, 0))
    return pl.pallas_call(
        body,
        grid=(t // tr, ncol),
        in_specs=[blk, tab, tab, tab],
        out_specs=blk,
        out_shape=jax.ShapeDtypeStruct((t, ncol * HEAD_DIM), BF16),
        compiler_params=_cparams("parallel", "parallel"),
        name=name,
    )(z, *tabs)


def _attn_specs(cfg, nb):
    kcol = cfg.hq
    vcol = cfg.off_v // HEAD_DIM
    qw = Q_PER_KV * HEAD_DIM
    q_spec = pl.BlockSpec((BLOCK, qw), lambda g, n: (n, g))

    def kv(col, shift):
        def idx(g, n):
            return (jnp.clip(n + shift, 0, nb - 1), col + g)
        return pl.BlockSpec((BLOCK, HEAD_DIM), idx)

    k_specs = [kv(kcol, s) for s in (-1, 0, 1)]
    v_specs = [kv(vcol, s) for s in (-1, 0, 1)]
    return q_spec, k_specs, v_specs


def _attn_probs(q, k, valid, sk):
    scale = 1.0 / math.sqrt(HEAD_DIM)
    s = _dot(q, k, "nt") * scale
    s = jnp.where(valid, s, jnp.finfo(F32).min)
    m = jnp.maximum(jnp.max(s, axis=1, keepdims=True), sk)
    e = jnp.exp(s - m)
    es = jnp.exp(sk - m)
    inv = 1.0 / (jnp.sum(e, axis=1, keepdims=True) + es)
    return e * inv, es * inv


def _attn_valid(n, t):
    qpos = n * BLOCK + lax.broadcasted_iota(jnp.int32, (BLOCK, 3 * BLOCK), 0)
    kpos = (n - 1) * BLOCK + lax.broadcasted_iota(jnp.int32, (BLOCK, 3 * BLOCK), 1)
    return (kpos >= 0) & (kpos < t) & (jnp.abs(qpos - kpos) <= WINDOW)


def _attn_fwd(qk, z, sink, cfg, *, name):
    t = qk.shape[0]
    nb = t // BLOCK
    q_spec, k_specs, v_specs = _attn_specs(cfg, nb)

    def body(sink_ref, q_ref, kp, kc, kn, vp, vc, vn, o_ref):
        g = pl.program_id(0)
        n = pl.program_id(1)
        k = jnp.concatenate([kp[...], kc[...], kn[...]], axis=0)
        v = jnp.concatenate([vp[...], vc[...], vn[...]], axis=0)
        valid = _attn_valid(n, t)
        for r in range(Q_PER_KV):
            sl = slice(r * HEAD_DIM, (r + 1) * HEAD_DIM)
            p, _ = _attn_probs(q_ref[:, sl], k, valid, sink_ref[g * Q_PER_KV + r])
            o_ref[:, sl] = _dot(p.astype(BF16), v, "nn").astype(BF16)

    return pl.pallas_call(
        body,
        grid=(cfg.g, nb),
        in_specs=[pl.BlockSpec(memory_space=pltpu.SMEM), q_spec] + k_specs + v_specs,
        out_specs=q_spec,
        out_shape=jax.ShapeDtypeStruct((t, cfg.attn), BF16),
        compiler_params=_cparams("parallel", "parallel"),
        name=name,
    )(sink, qk, qk, qk, qk, z, z, z)


def _attn_bwd(qk, z, dmix, sink, tabs, cfg, *, name):
    t = qk.shape[0]
    nb = t // BLOCK
    q_spec, k_specs, v_specs = _attn_specs(cfg, nb)
    tab = pl.BlockSpec((BLOCK, HEAD_DIM), lambda g, n: (n, 0))
    acc_spec = pl.BlockSpec((None, t + 2 * BLOCK, HEAD_DIM), lambda g, n: (g, 0, 0))
    scale = 1.0 / math.sqrt(HEAD_DIM)

    def body(sink_ref, q_ref, kp, kc, kn, vp, vc, vn, do_ref, c_ref, sa_ref, sb_ref,
             dq_ref, dk_ref, dv_ref, ds_ref):
        g = pl.program_id(0)
        n = pl.program_id(1)

        @pl.when(n == 0)
        def _():
            dk_ref[...] = jnp.zeros_like(dk_ref)
            dv_ref[...] = jnp.zeros_like(dv_ref)
            ds_ref[...] = jnp.zeros_like(ds_ref)

        k = jnp.concatenate([kp[...], kc[...], kn[...]], axis=0)
        v = jnp.concatenate([vp[...], vc[...], vn[...]], axis=0)
        valid = _attn_valid(n, t)
        dk_acc = jnp.zeros((3 * BLOCK, HEAD_DIM), F32)
        dv_acc = jnp.zeros((3 * BLOCK, HEAD_DIM), F32)
        for r in range(Q_PER_KV):
            sl = slice(r * HEAD_DIM, (r + 1) * HEAD_DIM)
            q = q_ref[:, sl]
            do = do_ref[:, sl]
            p, ps = _attn_probs(q, k, valid, sink_ref[g * Q_PER_KV + r])
            dp = _dot(do, v, "nt")
            dv_acc = dv_acc + _dot(p.astype(BF16), do, "tn")
            delta = jnp.sum(p * dp, axis=1, keepdims=True)
            dsc = (p * (dp - delta) * scale).astype(BF16)
            dq = _dot(dsc, k, "nn")
            dk_acc = dk_acc + _dot(dsc, q, "tn")
            dq_ref[:, sl] = _rope_apply_t(dq, c_ref[...], sa_ref[...], sb_ref[...]).astype(BF16)
            dsink = -jnp.sum(ps * delta, axis=0, keepdims=True)
            ds_ref[r:r + 1, :] += jnp.broadcast_to(dsink, (1, HEAD_DIM))
        rows = pl.ds(pl.multiple_of(n * BLOCK, BLOCK), 3 * BLOCK)
        dk_ref[rows, :] += dk_acc
        dv_ref[rows, :] += dv_acc

    acc_shape = jax.ShapeDtypeStruct((cfg.g, t + 2 * BLOCK, HEAD_DIM), F32)
    return pl.pallas_call(
        body,
        grid=(cfg.g, nb),
        in_specs=[pl.BlockSpec(memory_space=pltpu.SMEM), q_spec] + k_specs + v_specs + [q_spec, tab, tab, tab],
        out_specs=[q_spec, acc_spec, acc_spec, pl.BlockSpec((None, 8, HEAD_DIM), lambda g, n: (g, 0, 0))],
        out_shape=[jax.ShapeDtypeStruct((t, cfg.attn), BF16), acc_shape, acc_shape,
                   jax.ShapeDtypeStruct((cfg.g, 8, HEAD_DIM), F32)],
        compiler_params=_cparams("arbitrary", "arbitrary"),
        name=name,
    )(sink, qk, qk, qk, qk, z, z, z, dmix, *tabs)


def _kv_finish(dk_acc, dv_acc, tabs, cfg, *, name):
    t = dk_acc.shape[1] - 2 * BLOCK

    def body(dk_ref, dv_ref, c_ref, sa_ref, sb_ref, ok_ref, ov_ref):
        ok_ref[...] = _rope_apply_t(dk_ref[...], c_ref[...], sa_ref[...], sb_ref[...]).astype(BF16)
        ov_ref[...] = dv_ref[...].astype(BF16)

    acc = pl.BlockSpec((None, BLOCK, HEAD_DIM), lambda g, i: (g, i + 1, 0))
    tab = pl.BlockSpec((BLOCK, HEAD_DIM), lambda g, i: (i, 0))
    out = pl.BlockSpec((BLOCK, HEAD_DIM), lambda g, i: (i, g))
    return pl.pallas_call(
        body,
        grid=(cfg.g, t // BLOCK),
        in_specs=[acc, acc, tab, tab, tab],
        out_specs=[out, out],
        out_shape=[jax.ShapeDtypeStruct((t, cfg.kv), BF16)] * 2,
        compiler_params=_cparams("parallel", "parallel"),
        name=name,
    )(dk_acc, dv_acc, *tabs)


def _halo_specs(width, col, tb, t):
    per = tb // HALO_ROWS
    last = t // HALO_ROWS - 1
    prev = pl.BlockSpec((HALO_ROWS, width), lambda i: (jnp.maximum(i * per - 1, 0), col))
    cur = pl.BlockSpec((tb, width), lambda i: (i, col))
    nxt = pl.BlockSpec((HALO_ROWS, width), lambda i: (jnp.minimum((i + 1) * per, last), col))
    return [prev, cur, nxt]


def _halo_load(refs):
    return jnp.concatenate([r[...].astype(F32) for r in refs], axis=0)


def _shift_rows(v, start, rows):
    total = v.shape[0]
    return pltpu.roll(v, (total - start) % total, axis=0)[0:rows]


def _conv_glu(a_refs, g_refs, i, tb, t):
    a = _halo_load(a_refs)
    g = _halo_load(g_refs)
    rows = i * tb - HALO_ROWS + lax.broadcasted_iota(jnp.int32, (tb + 2 * HALO_ROWS, 1), 0)
    valid = (rows >= 0) & (rows < t)
    sg = _sigmoid(g)
    return a, sg, jnp.where(valid, a * sg, 0.0), valid


def _conv_fwd(z, w, b, lg, lb, cfg, *, tb, name):
    t = z.shape[0]
    cw = cfg.conv
    vec = pl.BlockSpec((1, cw), lambda i: (0, 0))

    def body(ap, ac, an, gp, gc, gn, w_ref, b_ref, lg_ref, lb_ref, o_ref, y_ref):
        _, _, c, _ = _conv_glu((ap, ac, an), (gp, gc, gn), pl.program_id(0), tb, t)
        acc = jnp.zeros((tb, cw), F32)
        for j in range(CONV_KERNEL):
            acc = acc + w_ref[j:j + 1, :] * _shift_rows(c, j + HALO_ROWS - CONV_PAD, tb)
        y = acc + b_ref[...]
        y_ref[...] = y
        mu = jnp.mean(y, axis=-1, keepdims=True)
        dlt = y - mu
        var = jnp.mean(dlt * dlt, axis=-1, keepdims=True)
        yn = dlt * lax.rsqrt(var + EPS) * lg_ref[...] + lb_ref[...]
        o_ref[...] = (yn * _sigmoid(yn)).astype(BF16)

    out = pl.BlockSpec((tb, cw), lambda i: (i, 0))
    return pl.pallas_call(
        body,
        grid=(t // tb,),
        in_specs=(_halo_specs(cw, cfg.off_ca // cw, tb, t) + _halo_specs(cw, cfg.off_cg // cw, tb, t)
                  + [pl.BlockSpec((CONV_KERNEL + 1, cw), lambda i: (0, 0)), vec, vec, vec]),
        out_specs=[out, out],
        out_shape=[jax.ShapeDtypeStruct((t, cw), BF16), jax.ShapeDtypeStruct((t, cw), F32)],
        compiler_params=_cparams("parallel"),
        name=name,
    )(z, z, z, z, z, z, w, b, lg, lb)


def _conv_bwd(z, y, dmix, w, lg, lb, cfg, *, tb, name):
    t = z.shape[0]
    cw = cfg.conv
    vec = pl.BlockSpec((1, cw), lambda i: (0, 0))
    cen = slice(HALO_ROWS, HALO_ROWS + tb)

    def body(ap, ac, an, gp, gc, gn, yp, yc, yn_, dp, dc_, dn, w_ref, lg_ref, lb_ref,
             da_ref, dg_ref, dw_ref, st_ref):
        @pl.when(pl.program_id(0) == 0)
        def _():
            dw_ref[...] = jnp.zeros_like(dw_ref)
            st_ref[...] = jnp.zeros_like(st_ref)

        a, sg, c, valid = _conv_glu((ap, ac, an), (gp, gc, gn), pl.program_id(0), tb, t)
        yv = _halo_load((yp, yc, yn_))
        do = _halo_load((dp, dc_, dn))
        mu = jnp.mean(yv, axis=-1, keepdims=True)
        dlt = yv - mu
        rstd = lax.rsqrt(jnp.mean(dlt * dlt, axis=-1, keepdims=True) + EPS)
        xh = dlt * rstd
        lgv = lg_ref[...]
        yn = xh * lgv + lb_ref[...]
        s = _sigmoid(yn)
        dyn = do * (s * (1.0 + yn * (1.0 - s)))
        dxh = dyn * lgv
        dy = rstd * (dxh - jnp.mean(dxh, axis=-1, keepdims=True)
                     - xh * jnp.mean(dxh * xh, axis=-1, keepdims=True))
        dy = jnp.where(valid, dy, 0.0)
        dyc = dy[cen]
        st_ref[0:1, :] += jnp.sum(dyc, axis=0, keepdims=True)
        st_ref[1:2, :] += jnp.sum((dyn * xh)[cen], axis=0, keepdims=True)
        st_ref[2:3, :] += jnp.sum(dyn[cen], axis=0, keepdims=True)
        dc = jnp.zeros((tb, cw), F32)
        for j in range(CONV_KERNEL):
            dc = dc + w_ref[j:j + 1, :] * _shift_rows(dy, HALO_ROWS + CONV_PAD - j, tb)
            dw_ref[j:j + 1, :] += jnp.sum(dyc * _shift_rows(c, j + HALO_ROWS - CONV_PAD, tb),
                                          axis=0, keepdims=True)
        sgc = sg[cen]
        da_ref[...] = (dc * sgc).astype(BF16)
        dg_ref[...] = (dc * a[cen] * sgc * (1.0 - sgc)).astype(BF16)

    out = pl.BlockSpec((tb, cw), lambda i: (i, 0))
    wspec = pl.BlockSpec((CONV_KERNEL + 1, cw), lambda i: (0, 0))
    return pl.pallas_call(
        body,
        grid=(t // tb,),
        in_specs=(_halo_specs(cw, cfg.off_ca // cw, tb, t) + _halo_specs(cw, cfg.off_cg // cw, tb, t)
                  + _halo_specs(cw, 0, tb, t) + _halo_specs(cw, cfg.attn // cw, tb, t) + [wspec, vec, vec]),
        out_specs=[out, out, wspec, pl.BlockSpec((8, cw), lambda i: (0, 0))],
        out_shape=[jax.ShapeDtypeStruct((t, cw), BF16), jax.ShapeDtypeStruct((t, cw), BF16),
                   jax.ShapeDtypeStruct((CONV_KERNEL + 1, cw), F32), jax.ShapeDtypeStruct((8, cw), F32)],
        compiler_params=_cparams("arbitrary"),
        name=name,
    )(z, z, z, z, z, z, y, y, y, dmix, dmix, dmix, w, lg, lb)


_SQRT_HALF = 1.0 / math.sqrt(2.0)
_INV_SQRT_2PI = 1.0 / math.sqrt(2.0 * math.pi)


def _gelu(v):
    return 0.5 * v * (1.0 + lax.erf(v * _SQRT_HALF))


def _gelu_grad(v):
    return 0.5 * (1.0 + lax.erf(v * _SQRT_HALF)) + v * jnp.exp(-0.5 * v * v) * _INV_SQRT_2PI


def _sgu_norm(zv_ref, lg_ref, lb_ref):
    xv = zv_ref[...].astype(F32)
    v = _gelu(xv)
    mu = jnp.mean(v, axis=-1, keepdims=True)
    dlt = v - mu
    rstd = lax.rsqrt(jnp.mean(dlt * dlt, axis=-1, keepdims=True) + EPS)
    xh = dlt * rstd
    return xv, xh, rstd, xh * lg_ref[...] + lb_ref[...]


def _sgu_specs(cfg):
    sw = cfg.sgu
    zu = pl.BlockSpec((CHUNK, sw), lambda i: (i, cfg.off_u // sw))
    zv = pl.BlockSpec((CHUNK, sw), lambda i: (i, cfg.off_sv // sw))
    vec = pl.BlockSpec((1, sw), lambda i: (0, 0))
    ws = pl.BlockSpec((cfg.sh, CHUNK, CHUNK), lambda i: (0, 0, 0))
    bs = pl.BlockSpec((cfg.sh, CHUNK, 1), lambda i: (0, 0, 0))
    return zu, zv, vec, ws, bs


def _sgu_fwd(z, lg, lb, ws, bs, cfg, *, name):
    t = z.shape[0]
    sw = cfg.sgu
    zu, zv, vec, wspec, bspec = _sgu_specs(cfg)

    def body(zu_ref, zv_ref, lg_ref, lb_ref, ws_ref, bs_ref, o_ref):
        u = _gelu(zu_ref[...].astype(F32))
        _, _, _, vn = _sgu_norm(zv_ref, lg_ref, lb_ref)
        vnb = vn.astype(BF16)
        for h in range(cfg.sh):
            sl = slice(h * HEAD_DIM, (h + 1) * HEAD_DIM)
            sp = _dot(ws_ref[h], vnb[:, sl], "nn") + bs_ref[h]
            o_ref[:, sl] = (u[:, sl] * sp).astype(BF16)

    return pl.pallas_call(
        body,
        grid=(t // CHUNK,),
        in_specs=[zu, zv, vec, vec, wspec, bspec],
        out_specs=pl.BlockSpec((CHUNK, sw), lambda i: (i, 0)),
        out_shape=jax.ShapeDtypeStruct((t, sw), BF16),
        compiler_params=_cparams("parallel"),
        name=name,
    )(z, z, lg, lb, ws, bs)


def _sgu_bwd(z, dmix, lg, lb, ws, bs, cfg, *, name):
    t = z.shape[0]
    sw = cfg.sgu
    zu, zv, vec, wspec, bspec = _sgu_specs(cfg)
    do_spec = pl.BlockSpec((CHUNK, sw), lambda i: (i, (cfg.attn + cfg.conv) // sw))

    def body(zu_ref, zv_ref, do_ref, lg_ref, lb_ref, ws_ref, bs_ref, duv_ref, dws_ref, dbs_ref, st_ref):
        @pl.when(pl.program_id(0) == 0)
        def _():
            dws_ref[...] = jnp.zeros_like(dws_ref)
            dbs_ref[...] = jnp.zeros_like(dbs_ref)
            st_ref[...] = jnp.zeros_like(st_ref)

        xu = zu_ref[...].astype(F32)
        u = _gelu(xu)
        xv, xh, rstd, vn = _sgu_norm(zv_ref, lg_ref, lb_ref)
        vnb = vn.astype(BF16)
        do = do_ref[...].astype(F32)
        dvn_parts = []
        for h in range(cfg.sh):
            sl = slice(h * HEAD_DIM, (h + 1) * HEAD_DIM)
            wh = ws_ref[h]
            sp = _dot(wh, vnb[:, sl], "nn") + bs_ref[h]
            dsp = do[:, sl] * u[:, sl]
            dspb = dsp.astype(BF16)
            dvn_parts.append(_dot(wh, dspb, "tn"))
            dws_ref[h] += _dot(dspb, vnb[:, sl], "nt")
            dbs_ref[h] += jnp.sum(dsp, axis=1, keepdims=True)
            duv_ref[:, sl] = (do[:, sl] * sp * _gelu_grad(xu[:, sl])).astype(BF16)
        dvn = jnp.concatenate(dvn_parts, axis=1)
        st_ref[0:1, :] += jnp.sum(dvn * xh, axis=0, keepdims=True)
        st_ref[1:2, :] += jnp.sum(dvn, axis=0, keepdims=True)
        dxh = dvn * lg_ref[...]
        dv = rstd * (dxh - jnp.mean(dxh, axis=-1, keepdims=True)
                     - xh * jnp.mean(dxh * xh, axis=-1, keepdims=True))
        duv_ref[:, sw:] = (dv * _gelu_grad(xv)).astype(BF16)

    return pl.pallas_call(
        body,
        grid=(t // CHUNK,),
        in_specs=[zu, zv, do_spec, vec, vec, wspec, bspec],
        out_specs=[pl.BlockSpec((CHUNK, 2 * sw), lambda i: (i, 0)), wspec, bspec,
                   pl.BlockSpec((8, sw), lambda i: (0, 0))],
        out_shape=[jax.ShapeDtypeStruct((t, 2 * sw), BF16), jax.ShapeDtypeStruct((cfg.sh, CHUNK, CHUNK), F32),
                   jax.ShapeDtypeStruct((cfg.sh, CHUNK, 1), F32), jax.ShapeDtypeStruct((8, sw), F32)],
        compiler_params=_cparams("arbitrary"),
        name=name,
    )(z, z, dmix, lg, lb, ws, bs)


def _sum_shards(parts, *, tr, name):
    _, r, c = parts.shape

    def body(p_ref, o_ref):
        acc = p_ref[0].astype(F32)
        for q in range(1, N_DEV):
            acc = acc + p_ref[q].astype(F32)
        o_ref[...] = acc

    return pl.pallas_call(
        body,
        grid=(r // tr,),
        in_specs=[pl.BlockSpec((N_DEV, tr, c), lambda i: (0, i, 0))],
        out_specs=pl.BlockSpec((tr, c), lambda i: (i, 0)),
        out_shape=jax.ShapeDtypeStruct((r, c), F32),
        compiler_params=_cparams("parallel"),
        name=name,
    )(parts)


def _adamw(w, g, m, v, *, name):
    r, c = w.shape
    tr = _pick(r, [p for p in (1024, 512, 256, 128, 64, 32, 16, 8) if p * c <= ADAMW_TILE_ELEMS])

    def body(w_ref, g_ref, m_ref, v_ref, d_ref, nm_ref, nv_ref):
        gv = g_ref[...]
        nm = ADAM_B1 * m_ref[...] + (1.0 - ADAM_B1) * gv
        nv = ADAM_B2 * v_ref[...] + (1.0 - ADAM_B2) * (gv * gv)
        m_hat = nm / (1.0 - ADAM_B1 ** ADAM_STEP)
        v_hat = nv / (1.0 - ADAM_B2 ** ADAM_STEP)
        d_ref[...] = -ADAM_LR * (m_hat / (jnp.sqrt(v_hat) + ADAM_EPS) + ADAM_WD * w_ref[...])
        nm_ref[...] = nm
        nv_ref[...] = nv

    blk = pl.BlockSpec((tr, c), lambda i: (i, 0))
    return pl.pallas_call(
        body,
        grid=(r // tr,),
        in_specs=[blk] * 4,
        out_specs=[blk] * 3,
        out_shape=[jax.ShapeDtypeStruct((r, c), F32)] * 3,
        compiler_params=_cparams("parallel"),
        name=name,
    )(w, g, m, v)


def _my_coords():
    return tuple(lax.axis_index(a) for a in MESH_AXES)


def _peer_coords(me, p):
    return tuple(1 - v if (p >> (2 - a)) & 1 else v for a, v in enumerate(me))


def _flat_id(coords):
    return 4 * coords[0] + 2 * coords[1] + coords[2]


def _exchange(arrs, *, scatter, name):
    na = len(arrs)

    def body(*refs):
        ins = refs[:na]
        outs = refs[na:2 * na]
        send_sems, recv_sems, local_sems = refs[2 * na:]
        me = _my_coords()
        my_id = _flat_id(me)

        local = []
        for k in range(na):
            src = ins[k].at[my_id] if scatter else ins[k]
            cp = pltpu.make_async_copy(src, outs[k].at[my_id], local_sems.at[k])
            cp.start()
            local.append(cp)

        def remote(p, k):
            peer = _peer_coords(me, p)
            peer_id = _flat_id(peer)
            sem = (p - 1) * na + k
            src = ins[k].at[peer_id] if scatter else ins[k]
            send = pltpu.make_async_remote_copy(
                src_ref=src, dst_ref=outs[k].at[my_id], send_sem=send_sems.at[sem],
                recv_sem=recv_sems.at[sem], device_id=peer, device_id_type=MESH_ID)
            recv = pltpu.make_async_remote_copy(
                src_ref=src, dst_ref=outs[k].at[peer_id], send_sem=send_sems.at[sem],
                recv_sem=recv_sems.at[sem], device_id=peer, device_id_type=MESH_ID)
            return send, recv

        pairs = [remote(p, k) for p in range(1, N_DEV) for k in range(na)]
        for send, _ in pairs:
            send.start()
        for _, recv in pairs:
            recv.wait_recv()
        for send, _ in pairs:
            send.wait_send()
        for cp in local:
            cp.wait()

    def out_of(a):
        return jax.ShapeDtypeStruct(a.shape if scatter else (N_DEV,) + a.shape, a.dtype)

    hbm = pl.BlockSpec(memory_space=pl.ANY)
    nsem = (N_DEV - 1) * na
    return pl.pallas_call(
        body,
        in_specs=[hbm] * na,
        out_specs=[hbm] * na,
        out_shape=[out_of(a) for a in arrs],
        scratch_shapes=[pltpu.SemaphoreType.DMA((nsem,)), pltpu.SemaphoreType.DMA((nsem,)),
                        pltpu.SemaphoreType.DMA((na,))],
        name=name,
    )(*arrs)


def _allreduce_small(flat, *, name):
    r, c = flat.shape

    def body(in_ref, out_ref, buf, send_sems, recv_sems):
        me = _my_coords()
        my_id = _flat_id(me)
        buf[my_id] = in_ref[...]

        def remote(p):
            peer = _peer_coords(me, p)
            send = pltpu.make_async_remote_copy(
                src_ref=in_ref, dst_ref=buf.at[my_id], send_sem=send_sems.at[p - 1],
                recv_sem=recv_sems.at[p - 1], device_id=peer, device_id_type=MESH_ID)
            recv = pltpu.make_async_remote_copy(
                src_ref=in_ref, dst_ref=buf.at[_flat_id(peer)], send_sem=send_sems.at[p - 1],
                recv_sem=recv_sems.at[p - 1], device_id=peer, device_id_type=MESH_ID)
            return send, recv

        pairs = [remote(p) for p in range(1, N_DEV)]
        for send, _ in pairs:
            send.start()
        for _, recv in pairs:
            recv.wait_recv()
        for send, _ in pairs:
            send.wait_send()
        acc = buf[0]
        for q in range(1, N_DEV):
            acc = acc + buf[q]
        out_ref[...] = acc

    vmem = pl.BlockSpec(memory_space=pltpu.VMEM)
    return pl.pallas_call(
        body,
        in_specs=[vmem],
        out_specs=vmem,
        out_shape=jax.ShapeDtypeStruct((r, c), F32),
        scratch_shapes=[pltpu.VMEM((N_DEV, r, c), F32), pltpu.SemaphoreType.DMA((N_DEV - 1,)),
                        pltpu.SemaphoreType.DMA((N_DEV - 1,))],
        compiler_params=pltpu.CompilerParams(vmem_limit_bytes=VMEM_LIMIT_BYTES),
        name=name,
    )(flat)


WEIGHT_NAMES = ("mix_norm_g", "w_in", "sink", "conv_dw_w", "conv_dw_b", "conv_ln_g", "conv_ln_b",
                "sgu_ln_g", "sgu_ln_b", "sgu_w", "sgu_b", "w_out", "ffn_norm_g", "w_gate", "w_up",
                "w_down", "final_norm_g")
SHARDED = ("w_in", "conv_dw_w", "w_out", "w_gate", "w_up", "w_down")
SMALL = tuple(n for n in WEIGHT_NAMES if n not in ("w_in", "w_out", "w_gate", "w_up", "w_down"))


def _pack_small(parts):
    flat = jnp.concatenate([parts[n].reshape(-1) for n in SMALL])
    pad = (-flat.shape[0]) % (8 * 128)
    return jnp.pad(flat, (0, pad)).reshape(-1, 128)


def _unpack_small(packed, shapes):
    flat = packed.reshape(-1)
    out, pos = {}, 0
    for n in SMALL:
        size = math.prod(shapes[n])
        out[n] = flat[pos:pos + size].reshape(shapes[n])
        pos += size
    return out


def kernel(x, mix_norm_g, w_in, sink, conv_dw_w, conv_dw_b, conv_ln_g, conv_ln_b, sgu_ln_g, sgu_ln_b, sgu_w, sgu_b, w_out, ffn_norm_g, w_gate, w_up, w_down, final_norm_g, loss_target, m_mix_norm_g, m_w_in, m_sink, m_conv_dw_w, m_conv_dw_b, m_conv_ln_g, m_conv_ln_b, m_sgu_ln_g, m_sgu_ln_b, m_sgu_w, m_sgu_b, m_w_out, m_ffn_norm_g, m_w_gate, m_w_up, m_w_down, m_final_norm_g, v_mix_norm_g, v_w_in, v_sink, v_conv_dw_w, v_conv_dw_b, v_conv_ln_g, v_conv_ln_b, v_sgu_ln_g, v_sgu_ln_b, v_sgu_w, v_sgu_b, v_w_out, v_ffn_norm_g, v_w_gate, v_w_up, v_w_down, v_final_norm_g):
    w = dict(mix_norm_g=mix_norm_g, w_in=w_in, sink=sink, conv_dw_w=conv_dw_w, conv_dw_b=conv_dw_b,
             conv_ln_g=conv_ln_g, conv_ln_b=conv_ln_b, sgu_ln_g=sgu_ln_g, sgu_ln_b=sgu_ln_b, sgu_w=sgu_w,
             sgu_b=sgu_b, w_out=w_out, ffn_norm_g=ffn_norm_g, w_gate=w_gate, w_up=w_up, w_down=w_down,
             final_norm_g=final_norm_g)
    mom_m = dict(zip(WEIGHT_NAMES, (m_mix_norm_g, m_w_in, m_sink, m_conv_dw_w, m_conv_dw_b, m_conv_ln_g,
                                    m_conv_ln_b, m_sgu_ln_g, m_sgu_ln_b, m_sgu_w, m_sgu_b, m_w_out,
                                    m_ffn_norm_g, m_w_gate, m_w_up, m_w_down, m_final_norm_g)))
    mom_v = dict(zip(WEIGHT_NAMES, (v_mix_norm_g, v_w_in, v_sink, v_conv_dw_w, v_conv_dw_b, v_conv_ln_g,
                                    v_conv_ln_b, v_sgu_ln_g, v_sgu_ln_b, v_sgu_w, v_sgu_b, v_w_out,
                                    v_ffn_norm_g, v_w_gate, v_w_up, v_w_down, v_final_norm_g)))

    _, t, d = x.shape
    depth = w_in.shape[0]
    cfg = Cfg(d, t)
    ff = w_gate.shape[2] * N_DEV
    my_id = _flat_id(_my_coords())
    xs = x[0]
    target = loss_target[0]

    tm = _pick(t, (1024, 512))
    tr = _pick(t, (256, 128))
    tb = _pick(t, (256, 128))
    tn_in = _pick(cfg.inw, (896, 512, 448))
    tn_ff = _pick(ff, (512, 1408, 704))
    tk_ff = _pick(ff, (1408, 704))
    tk_in = _pick(cfg.inw, (896, 448))
    tn_d = _pick(d, (512,))
    tk_t = _pick(t, (1024, 512))
    tm_in = _pick(cfg.inw, (896, 448))
    tm_ff = _pick(ff, (1408, 704))
    tn_dw = _pick(d, (1024,))

    tabs = _rope_tables(t)

    cflat = conv_dw_w.reshape(-1)
    cshard = jnp.pad(cflat, (0, (-cflat.shape[0]) % (8 * 128))).reshape(-1, 128)
    dw_all = _exchange([cshard], scatter=False, name="gather_conv_w")[0]
    dw_all = dw_all.reshape(N_DEV, -1)[:, :cflat.shape[0]].reshape(N_DEV, depth, CONV_KERNEL, -1)
    dw_all = dw_all.transpose(1, 2, 0, 3).reshape(depth, CONV_KERNEL, cfg.conv)
    dw_pad = jnp.pad(dw_all, ((0, 0), (0, 1), (0, 0)))

    def row(v):
        return v.reshape(1, -1)

    saved = []
    for l in range(depth):
        shards = [w_in[l].T.astype(BF16), w_gate[l].T.astype(BF16), w_up[l].T.astype(BF16),
                  w_out[l].astype(BF16), w_down[l].astype(BF16)]
        full = _exchange(shards, scatter=False, name="gather_weights")
        win_t, wg_t, wu_t, wo, wd = [f.reshape(-1, d) for f in full]

        h = _rms_fwd(xs, row(mix_norm_g[l]), tr=tr, name="mix_norm")
        z = _matmul(h, win_t, mode="nt", tm=tm, tn=tn_in, tk=d, epilogue=_ep_plain,
                    out_dtypes=[BF16], name="in_proj")[0]
        qk = _rope_fwd(z, tabs, cfg.hq + cfg.g, tr=tr, name="rope")
        attn = _attn_fwd(qk, z, sink[l], cfg, name="attn_fwd")
        conv, conv_y = _conv_fwd(z, dw_pad[l], row(conv_dw_b[l]), row(conv_ln_g[l]), row(conv_ln_b[l]),
                                 cfg, tb=tb, name="conv_fwd")
        ws_b = sgu_w[l].astype(BF16)
        bs_c = sgu_b[l][:, :, None]
        sgu = _sgu_fwd(z, row(sgu_ln_g[l]), row(sgu_ln_b[l]), ws_b, bs_c, cfg, name="sgu_fwd")
        mix = jnp.concatenate([attn, conv, sgu], axis=1)
        x1 = _matmul(mix, wo, mode="nn", tm=tm, tn=tn_d, tk=d, epilogue=_ep_add, extras=(xs,),
                     out_dtypes=[F32], name="out_proj")[0]
        h2 = _rms_fwd(x1, row(ffn_norm_g[l]), tr=tr, name="ffn_norm")
        gate, up, act = _ffn_up(h2, wg_t, wu_t, tm=tm, tn=tn_ff, name="ffn_up")
        x2 = _matmul(act, wd, mode="nn", tm=tm, tn=tn_d, tk=tk_ff, epilogue=_ep_add, extras=(x1,),
                     out_dtypes=[F32], name="ffn_down")[0]
        saved.append(dict(x0=xs, h=h, z=z, qk=qk, mix=mix, conv_y=conv_y, x1=x1, h2=h2, gate=gate, up=up,
                          act=act, win_t=win_t, wg_t=wg_t, wu_t=wu_t, wo=wo, wd=wd, ws_b=ws_b, bs_c=bs_c))
        xs = x2

    dx, dxb, head = _loss_head(xs, row(final_norm_g), target, tr=tr, name="loss_head")
    loss = lax.psum(head[1, 0], MESH_AXES)

    small = {n: [None] * depth for n in SMALL if n != "final_norm_g"}
    big = {n: [None] * depth for n in ("w_in", "w_out", "w_gate", "w_up", "w_down")}
    for l in reversed(range(depth)):
        s = saved[l]
        dgate, dup = _matmul(dxb, s["wd"], mode="nt", tm=tm, tn=tn_ff, tk=d, epilogue=_ep_swiglu_bwd,
                             extras=(s["gate"], s["up"]), out_dtypes=[BF16, BF16], name="ffn_down_bwd")
        dwd = _matmul(s["act"], dxb, mode="tn", tm=tm_ff, tn=tn_dw, tk=tk_t, epilogue=_ep_plain,
                      out_dtypes=[BF16], name="ffn_down_wgrad")[0]
        dh2 = _matmul(dgate, s["wg_t"], mode="nn", tm=tm, tn=tn_d, tk=tk_ff, epilogue=_ep_plain,
                      out_dtypes=[F32], name="ffn_gate_bwd")[0]
        dh2 = _matmul(dup, s["wu_t"], mode="nn", tm=tm, tn=tn_d, tk=tk_ff, epilogue=_ep_add, extras=(dh2,),
                      out_dtypes=[F32], name="ffn_up_bwd")[0]
        dwg_t = _matmul(dgate, s["h2"], mode="tn", tm=tm_ff, tn=tn_dw, tk=tk_t, epilogue=_ep_plain,
                        out_dtypes=[BF16], name="ffn_gate_wgrad")[0]
        dwu_t = _matmul(dup, s["h2"], mode="tn", tm=tm_ff, tn=tn_dw, tk=tk_t, epilogue=_ep_plain,
                        out_dtypes=[BF16], name="ffn_up_wgrad")[0]
        dx1, dx1b, dg2 = _rms_bwd(dh2, s["x1"], row(ffn_norm_g[l]), dx, tr=tr, name="ffn_norm_bwd")

        dmix = _matmul(dx1b, s["wo"], mode="nt", tm=tm, tn=tn_d, tk=d, epilogue=_ep_plain,
                       out_dtypes=[BF16], name="out_proj_bwd")[0]
        dwo = _matmul(s["mix"], dx1b, mode="tn", tm=_pick(d, (1024,)), tn=tn_dw, tk=tk_t, epilogue=_ep_plain,
                      out_dtypes=[BF16], name="out_proj_wgrad")[0]
        dq, dk_acc, dv_acc, dsink = _attn_bwd(s["qk"], s["z"], dmix, sink[l], tabs, cfg, name="attn_bwd")
        dk, dv = _kv_finish(dk_acc, dv_acc, tabs, cfg, name="attn_bwd_kv")
        da, dcg, dcw, cst = _conv_bwd(s["z"], s["conv_y"], dmix, dw_pad[l], row(conv_ln_g[l]),
                                      row(conv_ln_b[l]), cfg, tb=tb, name="conv_bwd")
        duv, dws, dbs, sst = _sgu_bwd(s["z"], dmix, row(sgu_ln_g[l]), row(sgu_ln_b[l]), s["ws_b"], s["bs_c"],
                                      cfg, name="sgu_bwd")
        dz = jnp.concatenate([dq, dk, dv, da, dcg, duv], axis=1)
        dh = _matmul(dz, s["win_t"], mode="nn", tm=tm, tn=tn_d, tk=tk_in, epilogue=_ep_plain,
                     out_dtypes=[F32], name="in_proj_bwd")[0]
        dwin_t = _matmul(dz, s["h"], mode="tn", tm=tm_in, tn=tn_dw, tk=tk_t, epilogue=_ep_plain,
                         out_dtypes=[BF16], name="in_proj_wgrad")[0]
        dx, dxb, dg1 = _rms_bwd(dh, s["x0"], row(mix_norm_g[l]), dx1, tr=tr, name="mix_norm_bwd")

        parts = [g.reshape(N_DEV, -1, d) for g in (dwin_t, dwg_t, dwu_t, dwo, dwd)]
        recv = _exchange(parts, scatter=True, name="scatter_grads")
        sums = [_sum_shards(p, tr=_pick(p.shape[1], (64, 32, 16)), name="sum_grads") for p in recv]
        big["w_in"][l] = sums[0].T
        big["w_gate"][l] = sums[1].T
        big["w_up"][l] = sums[2].T
        big["w_out"][l] = sums[3]
        big["w_down"][l] = sums[4]

        small["mix_norm_g"][l] = dg1[0]
        small["ffn_norm_g"][l] = dg2[0]
        small["sink"][l] = dsink[:, :Q_PER_KV, 0].reshape(-1)
        small["conv_dw_w"][l] = dcw[:CONV_KERNEL]
        small["conv_dw_b"][l] = cst[0]
        small["conv_ln_g"][l] = cst[1]
        small["conv_ln_b"][l] = cst[2]
        small["sgu_ln_g"][l] = sst[0]
        small["sgu_ln_b"][l] = sst[1]
        small["sgu_w"][l] = dws
        small["sgu_b"][l] = dbs[:, :, 0]

    parts = {n: jnp.stack(v) for n, v in small.items()}
    parts["final_norm_g"] = head[0]
    shapes = {n: parts[n].shape for n in SMALL}
    summed = _unpack_small(_allreduce_small(_pack_small(parts), name="allreduce_small"), shapes)

    grads = {n: jnp.stack(v) for n, v in big.items()}
    for n in SMALL:
        grads[n] = summed[n]
    cshard_w = conv_dw_w.shape[2]
    grads["conv_dw_w"] = lax.dynamic_slice_in_dim(summed["conv_dw_w"], my_id * cshard_w, cshard_w, axis=2)

    deltas, new_m, new_v = {}, {}, {}
    for n in WEIGHT_NAMES:
        shape = w[n].shape
        cols = shape[-1]
        view = lambda a: a.reshape(-1, cols)
        dl, nm, nv = _adamw(view(w[n]), view(grads[n]), view(mom_m[n]), view(mom_v[n]), name="adamw")
        deltas[n], new_m[n], new_v[n] = dl.reshape(shape), nm.reshape(shape), nv.reshape(shape)

    return (loss, dx[None], *[grads[n] for n in WEIGHT_NAMES], *[deltas[n] for n in WEIGHT_NAMES],
            *[new_m[n] for n in WEIGHT_NAMES], *[new_v[n] for n in WEIGHT_NAMES])
```

```python
import functools
import math

import jax
import jax.numpy as jnp
from jax import lax
from jax.experimental import pallas as pl
from jax.experimental.pallas import tpu as pltpu

F32 = jnp.float32
BF16 = jnp.bfloat16

HEAD_DIM = 128
Q_PER_KV = 4
WINDOW = 128
BLOCK = 128
ROT_DIM = 32
ROPE_THETA = 500000.0
CONV_KERNEL = 31
CONV_PAD = (CONV_KERNEL - 1) // 2
CHUNK = 128
EPS = 1e-6

ADAM_LR = 0.001
ADAM_B1 = 0.9
ADAM_B2 = 0.999
ADAM_EPS = 1e-08
ADAM_WD = 0.01
ADAM_STEP = 10

N_DEV = 8
MESH_AXES = ("x", "y", "c")
VMEM_LIMIT_BYTES = 56 * 1024 * 1024
HALO_ROWS = 16
ADAMW_TILE_ELEMS = 256 * 1024
MESH_ID = pl.DeviceIdType.MESH


class Cfg:
    def __init__(self, d_model, seq):
        self.d = d_model
        self.t = seq
        self.attn = d_model // 2
        self.hq = self.attn // HEAD_DIM
        self.g = self.hq // Q_PER_KV
        self.kv = self.g * HEAD_DIM
        self.conv = d_model // 4
        self.sgu = d_model // 4
        self.sh = self.sgu // HEAD_DIM
        self.off_k = self.attn
        self.off_v = self.attn + self.kv
        self.off_ca = self.attn + 2 * self.kv
        self.off_cg = self.off_ca + self.conv
        self.off_u = self.off_cg + self.conv
        self.off_sv = self.off_u + self.sgu
        self.inw = self.off_sv + self.sgu


def _pick(dim, prefs):
    for p in prefs:
        if dim % p == 0:
            return p
    return dim


def _cparams(*sem):
    return pltpu.CompilerParams(dimension_semantics=sem, vmem_limit_bytes=VMEM_LIMIT_BYTES)


def _sigmoid(v):
    return 1.0 / (1.0 + jnp.exp(-v))


_DN = {
    "nn": (((1,), (0,)), ((), ())),
    "nt": (((1,), (1,)), ((), ())),
    "tn": (((0,), (0,)), ((), ())),
}


def _dot(a, b, mode):
    return lax.dot_general(a, b, _DN[mode], preferred_element_type=F32)


def _matmul(a, b, *, mode, tm, tn, tk, epilogue, out_dtypes, extras=(), name):
    if mode == "tn":
        kdim, m = a.shape
        n = b.shape[1]
    elif mode == "nn":
        m, kdim = a.shape
        n = b.shape[1]
    else:
        m, kdim = a.shape
        n = b.shape[0]
    assert m % tm == 0 and n % tn == 0 and kdim % tk == 0, (name, m, n, kdim, tm, tn, tk)
    gm, gn, gk = m // tm, n // tn, kdim // tk
    if mode == "tn":
        a_spec = pl.BlockSpec((tk, tm), lambda i, j, k: (k, i))
    else:
        a_spec = pl.BlockSpec((tm, tk), lambda i, j, k: (i, k))
    if mode == "nt":
        b_spec = pl.BlockSpec((tn, tk), lambda i, j, k: (j, k))
    else:
        b_spec = pl.BlockSpec((tk, tn), lambda i, j, k: (k, j))
    tile = pl.BlockSpec((tm, tn), lambda i, j, k: (i, j))
    ne, no = len(extras), len(out_dtypes)

    def body(a_ref, b_ref, *rest):
        ex = rest[:ne]
        outs = rest[ne:ne + no]
        part = _dot(a_ref[...], b_ref[...], mode)

        def finish(acc):
            vals = epilogue(acc, *[e[...] for e in ex])
            for o_ref, val in zip(outs, vals):
                o_ref[...] = val.astype(o_ref.dtype)

        if gk == 1:
            finish(part)
        else:
            acc_ref = rest[ne + no]
            k = pl.program_id(2)

            @pl.when(k == 0)
            def _():
                acc_ref[...] = part

            if gk > 2:
                @pl.when((k > 0) & (k < gk - 1))
                def _():
                    acc_ref[...] += part

            @pl.when(k == gk - 1)
            def _():
                finish(acc_ref[...] + part)

    return pl.pallas_call(
        body,
        grid=(gm, gn, gk),
        in_specs=[a_spec, b_spec] + [tile] * ne,
        out_specs=[tile] * no,
        out_shape=[jax.ShapeDtypeStruct((m, n), dt) for dt in out_dtypes],
        scratch_shapes=[pltpu.VMEM((tm, tn), F32)] if gk > 1 else [],
        compiler_params=_cparams("parallel", "parallel", "arbitrary"),
        name=name,
    )(a, b, *extras)


def _ep_plain(acc):
    return (acc,)


def _ep_add(acc, r):
    return (r.astype(F32) + acc,)


def _ep_swiglu_bwd(dact, gate, up):
    gate = gate.astype(F32)
    up = up.astype(F32)
    s = _sigmoid(gate)
    silu = gate * s
    dgate = dact * up * (s * (1.0 + gate * (1.0 - s)))
    dup = dact * silu
    return dgate, dup


def _ffn_up(h, wgt, wut, *, tm, tn, name):
    m, kdim = h.shape
    n = wgt.shape[0]

    def body(h_ref, g_ref, u_ref, gate_ref, up_ref, act_ref):
        hv = h_ref[...]
        gate = _dot(hv, g_ref[...], "nt")
        up = _dot(hv, u_ref[...], "nt")
        gate_ref[...] = gate.astype(BF16)
        up_ref[...] = up.astype(BF16)
        act_ref[...] = (gate * _sigmoid(gate) * up).astype(BF16)

    tile = pl.BlockSpec((tm, tn), lambda i, j: (i, j))
    wspec = pl.BlockSpec((tn, kdim), lambda i, j: (j, 0))
    return pl.pallas_call(
        body,
        grid=(m // tm, n // tn),
        in_specs=[pl.BlockSpec((tm, kdim), lambda i, j: (i, 0)), wspec, wspec],
        out_specs=[tile] * 3,
        out_shape=[jax.ShapeDtypeStruct((m, n), BF16)] * 3,
        compiler_params=_cparams("parallel", "parallel"),
        name=name,
    )(h, wgt, wut)


def _rms_fwd(x, g, *, tr, name):
    t, d = x.shape

    def body(x_ref, g_ref, h_ref):
        xv = x_ref[...]
        r = lax.rsqrt(jnp.mean(xv * xv, axis=-1, keepdims=True) + EPS)
        h_ref[...] = (xv * r * g_ref[...]).astype(BF16)

    row = pl.BlockSpec((tr, d), lambda i: (i, 0))
    return pl.pallas_call(
        body,
        grid=(t // tr,),
        in_specs=[row, pl.BlockSpec((1, d), lambda i: (0, 0))],
        out_specs=row,
        out_shape=jax.ShapeDtypeStruct((t, d), BF16),
        compiler_params=_cparams("parallel"),
        name=name,
    )(x, g)


def _rms_bwd_math(dy, xv, g):
    r = lax.rsqrt(jnp.mean(xv * xv, axis=-1, keepdims=True) + EPS)
    xh = xv * r
    dg = jnp.sum(dy * xh, axis=0, keepdims=True)
    dyg = dy * g
    dx = r * (dyg - xh * jnp.mean(dyg * xh, axis=-1, keepdims=True))
    return dx, dg


def _rms_bwd(dh, x, g, dres, *, tr, name):
    t, d = x.shape

    def body(dh_ref, x_ref, g_ref, dres_ref, dx_ref, dxb_ref, dg_ref):
        dx, dg = _rms_bwd_math(dh_ref[...], x_ref[...], g_ref[...])
        dx = dx + dres_ref[...]
        dx_ref[...] = dx
        dxb_ref[...] = dx.astype(BF16)

        @pl.when(pl.program_id(0) == 0)
        def _():
            dg_ref[...] = jnp.zeros_like(dg_ref)

        dg_ref[0:1, :] += dg

    row = pl.BlockSpec((tr, d), lambda i: (i, 0))
    vec = pl.BlockSpec((1, d), lambda i: (0, 0))
    return pl.pallas_call(
        body,
        grid=(t // tr,),
        in_specs=[row, row, vec, row],
        out_specs=[row, row, pl.BlockSpec((8, d), lambda i: (0, 0))],
        out_shape=[jax.ShapeDtypeStruct((t, d), F32), jax.ShapeDtypeStruct((t, d), BF16),
                   jax.ShapeDtypeStruct((8, d), F32)],
        compiler_params=_cparams("arbitrary"),
        name=name,
    )(dh, x, g, dres)


def _loss_head(x, g, target, *, tr, name):
    t, d = x.shape

    def body(x_ref, g_ref, t_ref, dx_ref, dxb_ref, st_ref):
        xv = x_ref[...]
        gv = g_ref[...]
        r = lax.rsqrt(jnp.mean(xv * xv, axis=-1, keepdims=True) + EPS)
        err = xv * r * gv - t_ref[...]
        sq = jnp.sum(jnp.sum(err * err, axis=1, keepdims=True), axis=0, keepdims=True)
        dx, dg = _rms_bwd_math(err * (1.0 / d), xv, gv)
        dx_ref[...] = dx
        dxb_ref[...] = dx.astype(BF16)

        @pl.when(pl.program_id(0) == 0)
        def _():
            st_ref[...] = jnp.zeros_like(st_ref)

        st_ref[0:1, :] += dg
        st_ref[1:2, :] += jnp.broadcast_to(sq * (0.5 / d), (1, d))

    row = pl.BlockSpec((tr, d), lambda i: (i, 0))
    return pl.pallas_call(
        body,
        grid=(t // tr,),
        in_specs=[row, pl.BlockSpec((1, d), lambda i: (0, 0)), row],
        out_specs=[row, row, pl.BlockSpec((8, d), lambda i: (0, 0))],
        out_shape=[jax.ShapeDtypeStruct((t, d), F32), jax.ShapeDtypeStruct((t, d), BF16),
                   jax.ShapeDtypeStruct((8, d), F32)],
        compiler_params=_cparams("arbitrary"),
        name=name,
    )(x, g, target)


def _rope_tables(t):
    half = ROT_DIM // 2
    pos = jnp.arange(t, dtype=F32)
    inv = ROPE_THETA ** (-jnp.arange(0, ROT_DIM, 2, dtype=F32) / ROT_DIM)
    ang = pos[:, None] * inv[None, :]
    cos, sin = jnp.cos(ang), jnp.sin(ang)
    rest = HEAD_DIM - ROT_DIM
    c = jnp.concatenate([cos, cos, jnp.ones((t, rest), F32)], axis=1)
    sa = jnp.concatenate([-sin, jnp.zeros((t, HEAD_DIM - half), F32)], axis=1)
    sb = jnp.concatenate([jnp.zeros((t, half), F32), sin, jnp.zeros((t, rest), F32)], axis=1)
    return c, sa, sb


def _rope_apply(v, c, sa, sb):
    half = ROT_DIM // 2
    return v * c + pltpu.roll(v, HEAD_DIM - half, axis=1) * sa + pltpu.roll(v, half, axis=1) * sb


def _rope_apply_t(dv, c, sa, sb):
    half = ROT_DIM // 2
    return dv * c + pltpu.roll(dv * sa, half, axis=1) + pltpu.roll(dv * sb, HEAD_DIM - half, axis=1)


def _rope_fwd(z, tabs, ncol, *, tr, name):
    t = z.shape[0]

    def body(z_ref, c_ref, sa_ref, sb_ref, o_ref):
        o_ref[...] = _rope_apply(z_ref[...].astype(F32), c_ref[...], sa_ref[...], sb_ref[...]).astype(BF16)

    blk = pl.BlockSpec((tr, HEAD_DIM), lambda i, j: (i, j))
    tab = pl.BlockSpec((tr, HEAD_DIM), lambda i, j: (i, 0))
    return pl.pallas_call(
        body,
        grid=(t // tr, ncol),
        in_specs=[blk, tab, tab, tab],
        out_specs=blk,
        out_shape=jax.ShapeDtypeStruct((t, ncol * HEAD_DIM), BF16),
        compiler_params=_cparams("parallel", "parallel"),
        name=name,
    )(z, *tabs)


def _attn_specs(cfg, nb):
    kcol = cfg.hq
    vcol = cfg.off_v // HEAD_DIM
    qw = Q_PER_KV * HEAD_DIM
    q_spec = pl.BlockSpec((BLOCK, qw), lambda g, n: (n, g))

    def kv(col, shift):
        def idx(g, n):
            return (jnp.clip(n + shift, 0, nb - 1), col + g)
        return pl.BlockSpec((BLOCK, HEAD_DIM), idx)

    k_specs = [kv(kcol, s) for s in (-1, 0, 1)]
    v_specs = [kv(vcol, s) for s in (-1, 0, 1)]
    return q_spec, k_specs, v_specs


def _attn_probs(q, k, valid, sk):
    scale = 1.0 / math.sqrt(HEAD_DIM)
    s = _dot(q, k, "nt") * scale
    s = jnp.where(valid, s, jnp.finfo(F32).min)
    m = jnp.maximum(jnp.max(s, axis=1, keepdims=True), sk)
    e = jnp.exp(s - m)
    es = jnp.exp(sk - m)
    inv = 1.0 / (jnp.sum(e, axis=1, keepdims=True) + es)
    return e * inv, es * inv


def _attn_valid(n, t):
    qpos = n * BLOCK + lax.broadcasted_iota(jnp.int32, (BLOCK, 3 * BLOCK), 0)
    kpos = (n - 1) * BLOCK + lax.broadcasted_iota(jnp.int32, (BLOCK, 3 * BLOCK), 1)
    return (kpos >= 0) & (kpos < t) & (jnp.abs(qpos - kpos) <= WINDOW)


def _attn_fwd(qk, z, sink, cfg, *, name):
    t = qk.shape[0]
    nb = t // BLOCK
    q_spec, k_specs, v_specs = _attn_specs(cfg, nb)

    def body(sink_ref, q_ref, kp, kc, kn, vp, vc, vn, o_ref):
        g = pl.program_id(0)
        n = pl.program_id(1)
        k = jnp.concatenate([kp[...], kc[...], kn[...]], axis=0)
        v = jnp.concatenate([vp[...], vc[...], vn[...]], axis=0)
        valid = _attn_valid(n, t)
        for r in range(Q_PER_KV):
            sl = slice(r * HEAD_DIM, (r + 1) * HEAD_DIM)
            p, _ = _attn_probs(q_ref[:, sl], k, valid, sink_ref[g * Q_PER_KV + r])
            o_ref[:, sl] = _dot(p.astype(BF16), v, "nn").astype(BF16)

    return pl.pallas_call(
        body,
        grid=(cfg.g, nb),
        in_specs=[pl.BlockSpec(memory_space=pltpu.SMEM), q_spec] + k_specs + v_specs,
        out_specs=q_spec,
        out_shape=jax.ShapeDtypeStruct((t, cfg.attn), BF16),
        compiler_params=_cparams("parallel", "parallel"),
        name=name,
    )(sink, qk, qk, qk, qk, z, z, z)


def _attn_bwd(qk, z, dmix, sink, tabs, cfg, *, name):
    t = qk.shape[0]
    nb = t // BLOCK
    q_spec, k_specs, v_specs = _attn_specs(cfg, nb)
    tab = pl.BlockSpec((BLOCK, HEAD_DIM), lambda g, n: (n, 0))
    acc_spec = pl.BlockSpec((None, t + 2 * BLOCK, HEAD_DIM), lambda g, n: (g, 0, 0))
    scale = 1.0 / math.sqrt(HEAD_DIM)

    def body(sink_ref, q_ref, kp, kc, kn, vp, vc, vn, do_ref, c_ref, sa_ref, sb_ref,
             dq_ref, dk_ref, dv_ref, ds_ref):
        g = pl.program_id(0)
        n = pl.program_id(1)

        @pl.when(n == 0)
        def _():
            dk_ref[...] = jnp.zeros_like(dk_ref)
            dv_ref[...] = jnp.zeros_like(dv_ref)
            ds_ref[...] = jnp.zeros_like(ds_ref)

        k = jnp.concatenate([kp[...], kc[...], kn[...]], axis=0)
        v = jnp.concatenate([vp[...], vc[...], vn[...]], axis=0)
        valid = _attn_valid(n, t)
        dk_acc = jnp.zeros((3 * BLOCK, HEAD_DIM), F32)
        dv_acc = jnp.zeros((3 * BLOCK, HEAD_DIM), F32)
        for r in range(Q_PER_KV):
            sl = slice(r * HEAD_DIM, (r + 1) * HEAD_DIM)
            q = q_ref[:, sl]
            do = do_ref[:, sl]
            p, ps = _attn_probs(q, k, valid, sink_ref[g * Q_PER_KV + r])
            dp = _dot(do, v, "nt")
            dv_acc = dv_acc + _dot(p.astype(BF16), do, "tn")
            delta = jnp.sum(p * dp, axis=1, keepdims=True)
            dsc = (p * (dp - delta) * scale).astype(BF16)
            dq = _dot(dsc, k, "nn")
            dk_acc = dk_acc + _dot(dsc, q, "tn")
            dq_ref[:, sl] = _rope_apply_t(dq, c_ref[...], sa_ref[...], sb_ref[...]).astype(BF16)
            dsink = -jnp.sum(ps * delta, axis=0, keepdims=True)
            ds_ref[r:r + 1, :] += jnp.broadcast_to(dsink, (1, HEAD_DIM))
        rows = pl.ds(pl.multiple_of(n * BLOCK, BLOCK), 3 * BLOCK)
        dk_ref[rows, :] += dk_acc
        dv_ref[rows, :] += dv_acc

    acc_shape = jax.ShapeDtypeStruct((cfg.g, t + 2 * BLOCK, HEAD_DIM), F32)
    return pl.pallas_call(
        body,
        grid=(cfg.g, nb),
        in_specs=[pl.BlockSpec(memory_space=pltpu.SMEM), q_spec] + k_specs + v_specs + [q_spec, tab, tab, tab],
        out_specs=[q_spec, acc_spec, acc_spec, pl.BlockSpec((None, 8, HEAD_DIM), lambda g, n: (g, 0, 0))],
        out_shape=[jax.ShapeDtypeStruct((t, cfg.attn), BF16), acc_shape, acc_shape,
                   jax.ShapeDtypeStruct((cfg.g, 8, HEAD_DIM), F32)],
        compiler_params=_cparams("arbitrary", "arbitrary"),
        name=name,
    )(sink, qk, qk, qk, qk, z, z, z, dmix, *tabs)


def _kv_finish(dk_acc, dv_acc, tabs, cfg, *, name):
    t = dk_acc.shape[1] - 2 * BLOCK

    def body(dk_ref, dv_ref, c_ref, sa_ref, sb_ref, ok_ref, ov_ref):
        ok_ref[...] = _rope_apply_t(dk_ref[...], c_ref[...], sa_ref[...], sb_ref[...]).astype(BF16)
        ov_ref[...] = dv_ref[...].astype(BF16)

    acc = pl.BlockSpec((None, BLOCK, HEAD_DIM), lambda g, i: (g, i + 1, 0))
    tab = pl.BlockSpec((BLOCK, HEAD_DIM), lambda g, i: (i, 0))
    out = pl.BlockSpec((BLOCK, HEAD_DIM), lambda g, i: (i, g))
    return pl.pallas_call(
        body,
        grid=(cfg.g, t // BLOCK),
        in_specs=[acc, acc, tab, tab, tab],
        out_specs=[out, out],
        out_shape=[jax.ShapeDtypeStruct((t, cfg.kv), BF16)] * 2,
        compiler_params=_cparams("parallel", "parallel"),
        name=name,
    )(dk_acc, dv_acc, *tabs)


def _halo_specs(width, col, tb, t):
    per = tb // HALO_ROWS
    last = t // HALO_ROWS - 1
    prev = pl.BlockSpec((HALO_ROWS, width), lambda i: (jnp.maximum(i * per - 1, 0), col))
    cur = pl.BlockSpec((tb, width), lambda i: (i, col))
    nxt = pl.BlockSpec((HALO_ROWS, width), lambda i: (jnp.minimum((i + 1) * per, last), col))
    return [prev, cur, nxt]


def _halo_load(refs):
    return jnp.concatenate([r[...].astype(F32) for r in refs], axis=0)


def _shift_rows(v, start, rows):
    total = v.shape[0]
    return pltpu.roll(v, (total - start) % total, axis=0)[0:rows]


def _conv_glu(a_refs, g_refs, i, tb, t):
    a = _halo_load(a_refs)
    g = _halo_load(g_refs)
    rows = i * tb - HALO_ROWS + lax.broadcasted_iota(jnp.int32, (tb + 2 * HALO_ROWS, 1), 0)
    valid = (rows >= 0) & (rows < t)
    sg = _sigmoid(g)
    return a, sg, jnp.where(valid, a * sg, 0.0), valid


def _conv_fwd(z, w, b, lg, lb, cfg, *, tb, name):
    t = z.shape[0]
    cw = cfg.conv
    vec = pl.BlockSpec((1, cw), lambda i: (0, 0))

    def body(ap, ac, an, gp, gc, gn, w_ref, b_ref, lg_ref, lb_ref, o_ref, y_ref):
        _, _, c, _ = _conv_glu((ap, ac, an), (gp, gc, gn), pl.program_id(0), tb, t)
        acc = jnp.zeros((tb, cw), F32)
        for j in range(CONV_KERNEL):
            acc = acc + w_ref[j:j + 1, :] * _shift_rows(c, j + HALO_ROWS - CONV_PAD, tb)
        y = acc + b_ref[...]
        y_ref[...] = y
        mu = jnp.mean(y, axis=-1, keepdims=True)
        dlt = y - mu
        var = jnp.mean(dlt * dlt, axis=-1, keepdims=True)
        yn = dlt * lax.rsqrt(var + EPS) * lg_ref[...] + lb_ref[...]
        o_ref[...] = (yn * _sigmoid(yn)).astype(BF16)

    out = pl.BlockSpec((tb, cw), lambda i: (i, 0))
    return pl.pallas_call(
        body,
        grid=(t // tb,),
        in_specs=(_halo_specs(cw, cfg.off_ca // cw, tb, t) + _halo_specs(cw, cfg.off_cg // cw, tb, t)
                  + [pl.BlockSpec((CONV_KERNEL + 1, cw), lambda i: (0, 0)), vec, vec, vec]),
        out_specs=[out, out],
        out_shape=[jax.ShapeDtypeStruct((t, cw), BF16), jax.ShapeDtypeStruct((t, cw), F32)],
        compiler_params=_cparams("parallel"),
        name=name,
    )(z, z, z, z, z, z, w, b, lg, lb)


def _conv_bwd(z, y, dmix, w, lg, lb, cfg, *, tb, name):
    t = z.shape[0]
    cw = cfg.conv
    vec = pl.BlockSpec((1, cw), lambda i: (0, 0))
    cen = slice(HALO_ROWS, HALO_ROWS + tb)

    def body(ap, ac, an, gp, gc, gn, yp, yc, yn_, dp, dc_, dn, w_ref, lg_ref, lb_ref,
             da_ref, dg_ref, dw_ref, st_ref):
        @pl.when(pl.program_id(0) == 0)
        def _():
            dw_ref[...] = jnp.zeros_like(dw_ref)
            st_ref[...] = jnp.zeros_like(st_ref)

        a, sg, c, valid = _conv_glu((ap, ac, an), (gp, gc, gn), pl.program_id(0), tb, t)
        yv = _halo_load((yp, yc, yn_))
        do = _halo_load((dp, dc_, dn))
        mu = jnp.mean(yv, axis=-1, keepdims=True)
        dlt = yv - mu
        rstd = lax.rsqrt(jnp.mean(dlt * dlt, axis=-1, keepdims=True) + EPS)
        xh = dlt * rstd
        lgv = lg_ref[...]
        yn = xh * lgv + lb_ref[...]
        s = _sigmoid(yn)
        dyn = do * (s * (1.0 + yn * (1.0 - s)))
        dxh = dyn * lgv
        dy = rstd * (dxh - jnp.mean(dxh, axis=-1, keepdims=True)
                     - xh * jnp.mean(dxh * xh, axis=-1, keepdims=True))
        dy = jnp.where(valid, dy, 0.0)
        dyc = dy[cen]
        st_ref[0:1, :] += jnp.sum(dyc, axis=0, keepdims=True)
        st_ref[1:2, :] += jnp.sum((dyn * xh)[cen], axis=0, keepdims=True)
        st_ref[2:3, :] += jnp.sum(dyn[cen], axis=0, keepdims=True)
        dc = jnp.zeros((tb, cw), F32)
        for j in range(CONV_KERNEL):
            dc = dc + w_ref[j:j + 1, :] * _shift_rows(dy, HALO_ROWS + CONV_PAD - j, tb)
            dw_ref[j:j + 1, :] += jnp.sum(dyc * _shift_rows(c, j + HALO_ROWS - CONV_PAD, tb),
                                          axis=0, keepdims=True)
        sgc = sg[cen]
        da_ref[...] = (dc * sgc).astype(BF16)
        dg_ref[...] = (dc * a[cen] * sgc * (1.0 - sgc)).astype(BF16)

    out = pl.BlockSpec((tb, cw), lambda i: (i, 0))
    wspec = pl.BlockSpec((CONV_KERNEL + 1, cw), lambda i: (0, 0))
    return pl.pallas_call(
        body,
        grid=(t // tb,),
        in_specs=(_halo_specs(cw, cfg.off_ca // cw, tb, t) + _halo_specs(cw, cfg.off_cg // cw, tb, t)
                  + _halo_specs(cw, 0, tb, t) + _halo_specs(cw, cfg.attn // cw, tb, t) + [wspec, vec, vec]),
        out_specs=[out, out, wspec, pl.BlockSpec((8, cw), lambda i: (0, 0))],
        out_shape=[jax.ShapeDtypeStruct((t, cw), BF16), jax.ShapeDtypeStruct((t, cw), BF16),
                   jax.ShapeDtypeStruct((CONV_KERNEL + 1, cw), F32), jax.ShapeDtypeStruct((8, cw), F32)],
        compiler_params=_cparams("arbitrary"),
        name=name,
    )(z, z, z, z, z, z, y, y, y, dmix, dmix, dmix, w, lg, lb)


_SQRT_HALF = 1.0 / math.sqrt(2.0)
_INV_SQRT_2PI = 1.0 / math.sqrt(2.0 * math.pi)


def _gelu(v):
    return 0.5 * v * (1.0 + lax.erf(v * _SQRT_HALF))


def _gelu_grad(v):
    return 0.5 * (1.0 + lax.erf(v * _SQRT_HALF)) + v * jnp.exp(-0.5 * v * v) * _INV_SQRT_2PI


def _sgu_norm(zv_ref, lg_ref, lb_ref):
    xv = zv_ref[...].astype(F32)
    v = _gelu(xv)
    mu = jnp.mean(v, axis=-1, keepdims=True)
    dlt = v - mu
    rstd = lax.rsqrt(jnp.mean(dlt * dlt, axis=-1, keepdims=True) + EPS)
    xh = dlt * rstd
    return xv, xh, rstd, xh * lg_ref[...] + lb_ref[...]


def _sgu_specs(cfg):
    sw = cfg.sgu
    zu = pl.BlockSpec((CHUNK, sw), lambda i: (i, cfg.off_u // sw))
    zv = pl.BlockSpec((CHUNK, sw), lambda i: (i, cfg.off_sv // sw))
    vec = pl.BlockSpec((1, sw), lambda i: (0, 0))
    ws = pl.BlockSpec((cfg.sh, CHUNK, CHUNK), lambda i: (0, 0, 0))
    bs = pl.BlockSpec((cfg.sh, CHUNK, 1), lambda i: (0, 0, 0))
    return zu, zv, vec, ws, bs


def _sgu_fwd(z, lg, lb, ws, bs, cfg, *, name):
    t = z.shape[0]
    sw = cfg.sgu
    zu, zv, vec, wspec, bspec = _sgu_specs(cfg)

    def body(zu_ref, zv_ref, lg_ref, lb_ref, ws_ref, bs_ref, o_ref):
        u = _gelu(zu_ref[...].astype(F32))
        _, _, _, vn = _sgu_norm(zv_ref, lg_ref, lb_ref)
        vnb = vn.astype(BF16)
        for h in range(cfg.sh):
            sl = slice(h * HEAD_DIM, (h + 1) * HEAD_DIM)
            sp = _dot(ws_ref[h], vnb[:, sl], "nn") + bs_ref[h]
            o_ref[:, sl] = (u[:, sl] * sp).astype(BF16)

    return pl.pallas_call(
        body,
        grid=(t // CHUNK,),
        in_specs=[zu, zv, vec, vec, wspec, bspec],
        out_specs=pl.BlockSpec((CHUNK, sw), lambda i: (i, 0)),
        out_shape=jax.ShapeDtypeStruct((t, sw), BF16),
        compiler_params=_cparams("parallel"),
        name=name,
    )(z, z, lg, lb, ws, bs)


def _sgu_bwd(z, dmix, lg, lb, ws, bs, cfg, *, name):
    t = z.shape[0]
    sw = cfg.sgu
    zu, zv, vec, wspec, bspec = _sgu_specs(cfg)
    do_spec = pl.BlockSpec((CHUNK, sw), lambda i: (i, (cfg.attn + cfg.conv) // sw))

    def body(zu_ref, zv_ref, do_ref, lg_ref, lb_ref, ws_ref, bs_ref, duv_ref, dws_ref, dbs_ref, st_ref):
        @pl.when(pl.program_id(0) == 0)
        def _():
            dws_ref[...] = jnp.zeros_like(dws_ref)
            dbs_ref[...] = jnp.zeros_like(dbs_ref)
            st_ref[...] = jnp.zeros_like(st_ref)

        xu = zu_ref[...].astype(F32)
        u = _gelu(xu)
        xv, xh, rstd, vn = _sgu_norm(zv_ref, lg_ref, lb_ref)
        vnb = vn.astype(BF16)
        do = do_ref[...].astype(F32)
        dvn_parts = []
        for h in range(cfg.sh):
            sl = slice(h * HEAD_DIM, (h + 1) * HEAD_DIM)
            wh = ws_ref[h]
            sp = _dot(wh, vnb[:, sl], "nn") + bs_ref[h]
            dsp = do[:, sl] * u[:, sl]
            dspb = dsp.astype(BF16)
            dvn_parts.append(_dot(wh, dspb, "tn"))
            dws_ref[h] += _dot(dspb, vnb[:, sl], "nt")
            dbs_ref[h] += jnp.sum(dsp, axis=1, keepdims=True)
            duv_ref[:, sl] = (do[:, sl] * sp * _gelu_grad(xu[:, sl])).astype(BF16)
        dvn = jnp.concatenate(dvn_parts, axis=1)
        st_ref[0:1, :] += jnp.sum(dvn * xh, axis=0, keepdims=True)
        st_ref[1:2, :] += jnp.sum(dvn, axis=0, keepdims=True)
        dxh = dvn * lg_ref[...]
        dv = rstd * (dxh - jnp.mean(dxh, axis=-1, keepdims=True)
                     - xh * jnp.mean(dxh * xh, axis=-1, keepdims=True))
        duv_ref[:, sw:] = (dv * _gelu_grad(xv)).astype(BF16)

    return pl.pallas_call(
        body,
        grid=(t // CHUNK,),
        in_specs=[zu, zv, do_spec, vec, vec, wspec, bspec],
        out_specs=[pl.BlockSpec((CHUNK, 2 * sw), lambda i: (i, 0)), wspec, bspec,
                   pl.BlockSpec((8, sw), lambda i: (0, 0))],
        out_shape=[jax.ShapeDtypeStruct((t, 2 * sw), BF16), jax.ShapeDtypeStruct((cfg.sh, CHUNK, CHUNK), F32),
                   jax.ShapeDtypeStruct((cfg.sh, CHUNK, 1), F32), jax.ShapeDtypeStruct((8, sw), F32)],
        compiler_params=_cparams("arbitrary"),
        name=name,
    )(z, z, dmix, lg, lb, ws, bs)


def _sum_shards(parts, *, tr, name):
    _, r, c = parts.shape

    def body(p_ref, o_ref):
        acc = p_ref[0].astype(F32)
        for q in range(1, N_DEV):
            acc = acc + p_ref[q].astype(F32)
        o_ref[...] = acc

    return pl.pallas_call(
        body,
        grid=(r // tr,),
        in_specs=[pl.BlockSpec((N_DEV, tr, c), lambda i: (0, i, 0))],
        out_specs=pl.BlockSpec((tr, c), lambda i: (i, 0)),
        out_shape=jax.ShapeDtypeStruct((r, c), F32),
        compiler_params=_cparams("parallel"),
        name=name,
    )(parts)


def _adamw(w, g, m, v, *, name):
    r, c = w.shape
    tr = _pick(r, [p for p in (1024, 512, 256, 128, 64, 32, 16, 8) if p * c <= ADAMW_TILE_ELEMS])

    def body(w_ref, g_ref, m_ref, v_ref, d_ref, nm_ref, nv_ref):
        gv = g_ref[...]
        nm = ADAM_B1 * m_ref[...] + (1.0 - ADAM_B1) * gv
        nv = ADAM_B2 * v_ref[...] + (1.0 - ADAM_B2) * (gv * gv)
        m_hat = nm / (1.0 - ADAM_B1 ** ADAM_STEP)
        v_hat = nv / (1.0 - ADAM_B2 ** ADAM_STEP)
        d_ref[...] = -ADAM_LR * (m_hat / (jnp.sqrt(v_hat) + ADAM_EPS) + ADAM_WD * w_ref[...])
        nm_ref[...] = nm
        nv_ref[...] = nv

    blk = pl.BlockSpec((tr, c), lambda i: (i, 0))
    return pl.pallas_call(
        body,
        grid=(r // tr,),
        in_specs=[blk] * 4,
        out_specs=[blk] * 3,
        out_shape=[jax.ShapeDtypeStruct((r, c), F32)] * 3,
        compiler_params=_cparams("parallel"),
        name=name,
    )(w, g, m, v)


def _my_coords():
    return tuple(lax.axis_index(a) for a in MESH_AXES)


def _peer_coords(me, p):
    return tuple(1 - v if (p >> (2 - a)) & 1 else v for a, v in enumerate(me))


def _flat_id(coords):
    return 4 * coords[0] + 2 * coords[1] + coords[2]


def _exchange(arrs, *, scatter, name):
    na = len(arrs)

    def body(*refs):
        ins = refs[:na]
        outs = refs[na:2 * na]
        send_sems, recv_sems, local_sems = refs[2 * na:]
        me = _my_coords()
        my_id = _flat_id(me)

        local = []
        for k in range(na):
            src = ins[k].at[my_id] if scatter else ins[k]
            cp = pltpu.make_async_copy(src, outs[k].at[my_id], local_sems.at[k])
            cp.start()
            local.append(cp)

        def remote(p, k):
            peer = _peer_coords(me, p)
            peer_id = _flat_id(peer)
            sem = (p - 1) * na + k
            src = ins[k].at[peer_id] if scatter else ins[k]
            send = pltpu.make_async_remote_copy(
                src_ref=src, dst_ref=outs[k].at[my_id], send_sem=send_sems.at[sem],
                recv_sem=recv_sems.at[sem], device_id=peer, device_id_type=MESH_ID)
            recv = pltpu.make_async_remote_copy(
                src_ref=src, dst_ref=outs[k].at[peer_id], send_sem=send_sems.at[sem],
                recv_sem=recv_sems.at[sem], device_id=peer, device_id_type=MESH_ID)
            return send, recv

        pairs = [remote(p, k) for p in range(1, N_DEV) for k in range(na)]
        for send, _ in pairs:
            send.start()
        for _, recv in pairs:
            recv.wait_recv()
        for send, _ in pairs:
            send.wait_send()
        for cp in local:
            cp.wait()

    def out_of(a):
        return jax.ShapeDtypeStruct(a.shape if scatter else (N_DEV,) + a.shape, a.dtype)

    hbm = pl.BlockSpec(memory_space=pl.ANY)
    nsem = (N_DEV - 1) * na
    return pl.pallas_call(
        body,
        in_specs=[hbm] * na,
        out_specs=[hbm] * na,
        out_shape=[out_of(a) for a in arrs],
        scratch_shapes=[pltpu.SemaphoreType.DMA((nsem,)), pltpu.SemaphoreType.DMA((nsem,)),
                        pltpu.SemaphoreType.DMA((na,))],
        name=name,
    )(*arrs)


_HBM = pl.BlockSpec(memory_space=pltpu.HBM)
_SEM = pl.BlockSpec(memory_space=pltpu.SEMAPHORE)
_EFFECT = pltpu.SideEffectType.DATAFLOW_SIDE_EFFECTING


def _place_own(land, src, *, scatter, name):
    def body(land_ref, src_ref, out_ref, sem):
        my_id = _flat_id(_my_coords())
        mine = src_ref.at[my_id] if scatter else src_ref
        cp = pltpu.make_async_copy(mine, out_ref.at[my_id], sem)
        cp.start()
        cp.wait()

    hbm = pl.BlockSpec(memory_space=pl.ANY)
    return pl.pallas_call(
        body,
        in_specs=[hbm, hbm],
        out_specs=hbm,
        out_shape=jax.ShapeDtypeStruct(land.shape, land.dtype),
        input_output_aliases={0: 0},
        scratch_shapes=[pltpu.SemaphoreType.DMA],
        name=name,
    )(land, src)


def _peer_copy(src_ref, land_ref, send_sems, recv_sems, me, p, scatter, arrival):
    peer = _peer_coords(me, p)
    peer_id = _flat_id(peer)
    return pltpu.make_async_remote_copy(
        src_ref=src_ref.at[peer_id] if scatter else src_ref,
        dst_ref=land_ref.at[peer_id if arrival else _flat_id(me)],
        send_sem=send_sems.at[p - 1], recv_sem=recv_sems.at[p - 1], device_id=peer, device_id_type=MESH_ID)


def _exchange_start(src, land, *, scatter, name):
    def body(src_ref, land_ref, send_sems, recv_sems, src_thru, land_thru, token):
        me = _my_coords()
        for p in range(1, N_DEV):
            _peer_copy(src_ref, land_ref, send_sems, recv_sems, me, p, scatter, False).start()
        token[...] = jnp.zeros_like(token)

    nsem = N_DEV - 1
    return pl.pallas_call(
        body,
        name=name,
        out_shape=(pltpu.SemaphoreType.DMA((nsem,)), pltpu.SemaphoreType.DMA((nsem,)),
                   pltpu.HBM(src.shape, src.dtype), pltpu.HBM(land.shape, land.dtype),
                   jax.ShapeDtypeStruct((8, 128), F32)),
        in_specs=(_HBM, _HBM),
        out_specs=(_SEM, _SEM, _HBM, _HBM, pl.BlockSpec(memory_space=pltpu.VMEM)),
        input_output_aliases={0: 2, 1: 3},
        compiler_params=pltpu.CompilerParams(has_side_effects=_EFFECT),
    )(pltpu.with_memory_space_constraint(src, pltpu.HBM), pltpu.with_memory_space_constraint(land, pltpu.HBM))


def _exchange_wait(handle, after, *, scatter, name):
    send_sems, recv_sems, src_thru, land_thru = handle

    def body(src_ref, land_ref, send_sems, recv_sems, after_ref, src_dead, got_ref):
        me = _my_coords()
        for p in range(1, N_DEV):
            _peer_copy(src_ref, land_ref, send_sems, recv_sems, me, p, scatter, False).wait_send()
            _peer_copy(src_ref, land_ref, send_sems, recv_sems, me, p, scatter, True).wait_recv()

    return pl.pallas_call(
        body,
        name=name,
        out_shape=(pltpu.HBM(src_thru.shape, src_thru.dtype), pltpu.HBM(land_thru.shape, land_thru.dtype)),
        in_specs=(_HBM, _HBM, _SEM, _SEM, pl.BlockSpec(memory_space=pl.ANY)),
        out_specs=(_HBM, _HBM),
        input_output_aliases={0: 0, 1: 1},
        compiler_params=pltpu.CompilerParams(has_side_effects=_EFFECT),
    )(src_thru, land_thru, send_sems, recv_sems, after)[1]


def _exchange_begin(src, *, scatter, name):
    shape = src.shape if scatter else (N_DEV,) + src.shape
    land = _place_own(lax.empty(shape, src.dtype), src, scatter=scatter, name=name + "_own")
    *handle, token = _exchange_start(src, land, scatter=scatter, name=name + "_start")
    return tuple(handle), token[0, 0]


def _allreduce_small(flat, *, name):
    r, c = flat.shape

    def body(in_ref, out_ref, buf, send_sems, recv_sems):
        me = _my_coords()
        my_id = _flat_id(me)
        buf[my_id] = in_ref[...]

        def remote(p):
            peer = _peer_coords(me, p)
            send = pltpu.make_async_remote_copy(
                src_ref=in_ref, dst_ref=buf.at[my_id], send_sem=send_sems.at[p - 1],
                recv_sem=recv_sems.at[p - 1], device_id=peer, device_id_type=MESH_ID)
            recv = pltpu.make_async_remote_copy(
                src_ref=in_ref, dst_ref=buf.at[_flat_id(peer)], send_sem=send_sems.at[p - 1],
                recv_sem=recv_sems.at[p - 1], device_id=peer, device_id_type=MESH_ID)
            return send, recv

        pairs = [remote(p) for p in range(1, N_DEV)]
        for send, _ in pairs:
            send.start()
        for _, recv in pairs:
            recv.wait_recv()
        for send, _ in pairs:
            send.wait_send()
        acc = buf[0]
        for q in range(1, N_DEV):
            acc = acc + buf[q]
        out_ref[...] = acc

    vmem = pl.BlockSpec(memory_space=pltpu.VMEM)
    return pl.pallas_call(
        body,
        in_specs=[vmem],
        out_specs=vmem,
        out_shape=jax.ShapeDtypeStruct((r, c), F32),
        scratch_shapes=[pltpu.VMEM((N_DEV, r, c), F32), pltpu.SemaphoreType.DMA((N_DEV - 1,)),
                        pltpu.SemaphoreType.DMA((N_DEV - 1,))],
        compiler_params=pltpu.CompilerParams(vmem_limit_bytes=VMEM_LIMIT_BYTES),
        name=name,
    )(flat)


WEIGHT_NAMES = ("mix_norm_g", "w_in", "sink", "conv_dw_w", "conv_dw_b", "conv_ln_g", "conv_ln_b",
                "sgu_ln_g", "sgu_ln_b", "sgu_w", "sgu_b", "w_out", "ffn_norm_g", "w_gate", "w_up",
                "w_down", "final_norm_g")
SHARDED = ("w_in", "conv_dw_w", "w_out", "w_gate", "w_up", "w_down")
SMALL = tuple(n for n in WEIGHT_NAMES if n not in ("w_in", "w_out", "w_gate", "w_up", "w_down"))


def _pack_small(parts):
    flat = jnp.concatenate([parts[n].reshape(-1) for n in SMALL])
    pad = (-flat.shape[0]) % (8 * 128)
    return jnp.pad(flat, (0, pad)).reshape(-1, 128)


def _unpack_small(packed, shapes):
    flat = packed.reshape(-1)
    out, pos = {}, 0
    for n in SMALL:
        size = math.prod(shapes[n])
        out[n] = flat[pos:pos + size].reshape(shapes[n])
        pos += size
    return out


def kernel(x, mix_norm_g, w_in, sink, conv_dw_w, conv_dw_b, conv_ln_g, conv_ln_b, sgu_ln_g, sgu_ln_b, sgu_w, sgu_b, w_out, ffn_norm_g, w_gate, w_up, w_down, final_norm_g, loss_target, m_mix_norm_g, m_w_in, m_sink, m_conv_dw_w, m_conv_dw_b, m_conv_ln_g, m_conv_ln_b, m_sgu_ln_g, m_sgu_ln_b, m_sgu_w, m_sgu_b, m_w_out, m_ffn_norm_g, m_w_gate, m_w_up, m_w_down, m_final_norm_g, v_mix_norm_g, v_w_in, v_sink, v_conv_dw_w, v_conv_dw_b, v_conv_ln_g, v_conv_ln_b, v_sgu_ln_g, v_sgu_ln_b, v_sgu_w, v_sgu_b, v_w_out, v_ffn_norm_g, v_w_gate, v_w_up, v_w_down, v_final_norm_g):
    w = dict(mix_norm_g=mix_norm_g, w_in=w_in, sink=sink, conv_dw_w=conv_dw_w, conv_dw_b=conv_dw_b,
             conv_ln_g=conv_ln_g, conv_ln_b=conv_ln_b, sgu_ln_g=sgu_ln_g, sgu_ln_b=sgu_ln_b, sgu_w=sgu_w,
             sgu_b=sgu_b, w_out=w_out, ffn_norm_g=ffn_norm_g, w_gate=w_gate, w_up=w_up, w_down=w_down,
             final_norm_g=final_norm_g)
    mom_m = dict(zip(WEIGHT_NAMES, (m_mix_norm_g, m_w_in, m_sink, m_conv_dw_w, m_conv_dw_b, m_conv_ln_g,
                                    m_conv_ln_b, m_sgu_ln_g, m_sgu_ln_b, m_sgu_w, m_sgu_b, m_w_out,
                                    m_ffn_norm_g, m_w_gate, m_w_up, m_w_down, m_final_norm_g)))
    mom_v = dict(zip(WEIGHT_NAMES, (v_mix_norm_g, v_w_in, v_sink, v_conv_dw_w, v_conv_dw_b, v_conv_ln_g,
                                    v_conv_ln_b, v_sgu_ln_g, v_sgu_ln_b, v_sgu_w, v_sgu_b, v_w_out,
                                    v_ffn_norm_g, v_w_gate, v_w_up, v_w_down, v_final_norm_g)))

    _, t, d = x.shape
    depth = w_in.shape[0]
    cfg = Cfg(d, t)
    ff = w_gate.shape[2] * N_DEV
    my_id = _flat_id(_my_coords())
    xs = x[0]
    target = loss_target[0]

    tm = _pick(t, (1024, 512))
    tr = _pick(t, (256, 128))
    tb = _pick(t, (256, 128))
    tn_in = _pick(cfg.inw, (896, 512, 448))
    tn_ff = _pick(ff, (512, 1408, 704))
    tk_ff = _pick(ff, (1408, 704))
    tk_in = _pick(cfg.inw, (896, 448))
    tn_d = _pick(d, (512,))
    tk_t = _pick(t, (1024, 512))
    tm_in = _pick(cfg.inw, (896, 448))
    tm_ff = _pick(ff, (1408, 704))
    tn_dw = _pick(d, (1024,))

    tabs = _rope_tables(t)

    cflat = conv_dw_w.reshape(-1)
    cshard = jnp.pad(cflat, (0, (-cflat.shape[0]) % (8 * 128))).reshape(-1, 128)
    dw_all = _exchange([cshard], scatter=False, name="gather_conv_w")[0]
    dw_all = dw_all.reshape(N_DEV, -1)[:, :cflat.shape[0]].reshape(N_DEV, depth, CONV_KERNEL, -1)
    dw_all = dw_all.transpose(1, 2, 0, 3).reshape(depth, CONV_KERNEL, cfg.conv)
    dw_pad = jnp.pad(dw_all, ((0, 0), (0, 1), (0, 0)))

    def row(v):
        return v.reshape(1, -1)

    gathers = [None] * depth

    def gather_begin(l, zero):
        shards = dict(win=w_in[l].T, wo=w_out[l], wg=w_gate[l].T, wu=w_up[l].T, wd=w_down[l])
        gathers[l] = {}
        for k, v in shards.items():
            gathers[l][k], started = _exchange_begin((v + zero).astype(BF16), scatter=False,
                                                     name=f"gather_{k}_{l}")
            zero = zero + started
        return zero

    def gathered(l, k, after):
        full = _exchange_wait(gathers[l][k], after, scatter=False, name=f"gather_{k}_{l}_wait")
        return full.reshape(-1, d)

    saved = []
    scatters = [None] * depth
    started = gather_begin(0, jnp.zeros((), F32))
    for l in range(depth):
        h = _rms_fwd(xs, row(mix_norm_g[l]) + started, tr=tr, name="mix_norm")
        win_t = gathered(l, "win", h)
        z = _matmul(h, win_t, mode="nt", tm=tm, tn=tn_in, tk=d, epilogue=_ep_plain,
                    out_dtypes=[BF16], name="in_proj")[0]
        started = jnp.zeros((), F32)
        if l + 1 < depth:
            z, started = lax.optimization_barrier((z, started))
            started = gather_begin(l + 1, started)
        qk = _rope_fwd(z, tabs, cfg.hq + cfg.g, tr=tm, name="rope")
        attn = _attn_fwd(qk, z, sink[l], cfg, name="attn_fwd")
        conv, conv_y = _conv_fwd(z, dw_pad[l], row(conv_dw_b[l]), row(conv_ln_g[l]) + started,
                                 row(conv_ln_b[l]), cfg, tb=tb, name="conv_fwd")
        ws_b = sgu_w[l].astype(BF16)
        bs_c = sgu_b[l][:, :, None]
        sgu = _sgu_fwd(z, row(sgu_ln_g[l]), row(sgu_ln_b[l]), ws_b, bs_c, cfg, name="sgu_fwd")
        mix = jnp.concatenate([attn, conv, sgu], axis=1)
        wo = gathered(l, "wo", mix)
        x1 = _matmul(mix, wo, mode="nn", tm=tm, tn=tn_d, tk=d, epilogue=_ep_add, extras=(xs,),
                     out_dtypes=[F32], name="out_proj")[0]
        h2 = _rms_fwd(x1, row(ffn_norm_g[l]), tr=tr, name="ffn_norm")
        wg_t = gathered(l, "wg", h2)
        wu_t = gathered(l, "wu", h2)
        gate, up, act = _ffn_up(h2, wg_t, wu_t, tm=tm, tn=tn_ff, name="ffn_up")
        wd = gathered(l, "wd", act)
        x2 = _matmul(act, wd, mode="nn", tm=tm, tn=tn_d, tk=tk_ff, epilogue=_ep_add, extras=(x1,),
                     out_dtypes=[F32], name="ffn_down")[0]
        saved.append(dict(x0=xs, h=h, z=z, qk=qk, mix=mix, conv_y=conv_y, x1=x1, h2=h2, gate=gate, up=up,
                          act=act, win_t=win_t, wg_t=wg_t, wu_t=wu_t, wo=wo, wd=wd, ws_b=ws_b, bs_c=bs_c))
        xs = x2

    dx, dxb, head = _loss_head(xs, row(final_norm_g), target, tr=tr, name="loss_head")
    loss = lax.psum(head[1, 0], MESH_AXES)

    def scatter_begin(grad, n, l):
        handle, zero = _exchange_begin(grad.reshape(N_DEV, -1, d), scatter=True, name=f"scatter_{n}_{l}")
        scatters[l][n] = handle
        return zero

    small = {n: [None] * depth for n in SMALL if n != "final_norm_g"}
    big = {n: [None] * depth for n in ("w_in", "w_out", "w_gate", "w_up", "w_down")}
    for l in reversed(range(depth)):
        s = saved[l]
        dgate, dup = _matmul(dxb, s["wd"], mode="nt", tm=tm, tn=tn_ff, tk=d, epilogue=_ep_swiglu_bwd,
                             extras=(s["gate"], s["up"]), out_dtypes=[BF16, BF16], name="ffn_down_bwd")
        dwd = _matmul(s["act"], dxb, mode="tn", tm=tm_ff, tn=tn_dw, tk=tk_t, epilogue=_ep_plain,
                      out_dtypes=[BF16], name="ffn_down_wgrad")[0]
        scatters[l] = {}
        started = scatter_begin(dwd, "w_down", l)
        dh2 = _matmul(dgate, s["wg_t"], mode="nn", tm=tm, tn=tn_d, tk=tk_ff, epilogue=_ep_plain,
                      out_dtypes=[F32], name="ffn_gate_bwd")[0]
        dh2 = _matmul(dup, s["wu_t"], mode="nn", tm=tm, tn=tn_d, tk=tk_ff, epilogue=_ep_add, extras=(dh2,),
                      out_dtypes=[F32], name="ffn_up_bwd")[0]
        dwg_t = _matmul(dgate, s["h2"], mode="tn", tm=tm_ff, tn=tn_dw, tk=tk_t, epilogue=_ep_plain,
                        out_dtypes=[BF16], name="ffn_gate_wgrad")[0]
        dwu_t = _matmul(dup, s["h2"], mode="tn", tm=tm_ff, tn=tn_dw, tk=tk_t, epilogue=_ep_plain,
                        out_dtypes=[BF16], name="ffn_up_wgrad")[0]
        started = started + scatter_begin(dwg_t, "w_gate", l) + scatter_begin(dwu_t, "w_up", l)
        dx1, dx1b, dg2 = _rms_bwd(dh2, s["x1"], row(ffn_norm_g[l]) + started, dx, tr=tr, name="ffn_norm_bwd")

        dmix = _matmul(dx1b, s["wo"], mode="nt", tm=tm, tn=tn_d, tk=d, epilogue=_ep_plain,
                       out_dtypes=[BF16], name="out_proj_bwd")[0]
        dwo = _matmul(s["mix"], dx1b, mode="tn", tm=_pick(d, (1024,)), tn=tn_dw, tk=tk_t, epilogue=_ep_plain,
                      out_dtypes=[BF16], name="out_proj_wgrad")[0]
        started = scatter_begin(dwo, "w_out", l)
        dq, dk_acc, dv_acc, dsink = _attn_bwd(s["qk"], s["z"], dmix, sink[l], tabs, cfg, name="attn_bwd")
        dk, dv = _kv_finish(dk_acc, dv_acc, tabs, cfg, name="attn_bwd_kv")
        da, dcg, dcw, cst = _conv_bwd(s["z"], s["conv_y"], dmix, dw_pad[l], row(conv_ln_g[l]) + started,
                                      row(conv_ln_b[l]), cfg, tb=tb, name="conv_bwd")
        duv, dws, dbs, sst = _sgu_bwd(s["z"], dmix, row(sgu_ln_g[l]), row(sgu_ln_b[l]), s["ws_b"], s["bs_c"],
                                      cfg, name="sgu_bwd")
        dz = jnp.concatenate([dq, dk, dv, da, dcg, duv], axis=1)
        dh = _matmul(dz, s["win_t"], mode="nn", tm=tm, tn=tn_d, tk=tk_in, epilogue=_ep_plain,
                     out_dtypes=[F32], name="in_proj_bwd")[0]
        dwin_t = _matmul(dz, s["h"], mode="tn", tm=tm_in, tn=tn_dw, tk=tk_t, epilogue=_ep_plain,
                         out_dtypes=[BF16], name="in_proj_wgrad")[0]
        started = scatter_begin(dwin_t, "w_in", l)
        dx, dxb, dg1 = _rms_bwd(dh, s["x0"], row(mix_norm_g[l]) + started, dx1, tr=tr, name="mix_norm_bwd")

        small["mix_norm_g"][l] = dg1[0]
        small["ffn_norm_g"][l] = dg2[0]
        small["sink"][l] = dsink[:, :Q_PER_KV, 0].reshape(-1)
        small["conv_dw_w"][l] = dcw[:CONV_KERNEL]
        small["conv_dw_b"][l] = cst[0]
        small["conv_ln_g"][l] = cst[1]
        small["conv_ln_b"][l] = cst[2]
        small["sgu_ln_g"][l] = sst[0]
        small["sgu_ln_b"][l] = sst[1]
        small["sgu_w"][l] = dws
        small["sgu_b"][l] = dbs[:, :, 0]

    for l in reversed(range(depth)):
        for n in ("w_down", "w_gate", "w_up", "w_out", "w_in"):
            recv = _exchange_wait(scatters[l][n], dx, scatter=True, name=f"scatter_{n}_{l}_wait")
            total = _sum_shards(recv, tr=_pick(recv.shape[1], (64, 32, 16)), name="sum_grads")
            big[n][l] = total.T if n in ("w_in", "w_gate", "w_up") else total

    parts = {n: jnp.stack(v) for n, v in small.items()}
    parts["final_norm_g"] = head[0]
    shapes = {n: parts[n].shape for n in SMALL}
    summed = _unpack_small(_allreduce_small(_pack_small(parts), name="allreduce_small"), shapes)

    grads = {n: jnp.stack(v) for n, v in big.items()}
    for n in SMALL:
        grads[n] = summed[n]
    cshard_w = conv_dw_w.shape[2]
    grads["conv_dw_w"] = lax.dynamic_slice_in_dim(summed["conv_dw_w"], my_id * cshard_w, cshard_w, axis=2)

    deltas, new_m, new_v = {}, {}, {}
    for n in WEIGHT_NAMES:
        shape = w[n].shape
        cols = shape[-1]
        view = lambda a: a.reshape(-1, cols)
        dl, nm, nv = _adamw(view(w[n]), view(grads[n]), view(mom_m[n]), view(mom_v[n]), name="adamw")
        deltas[n], new_m[n], new_v[n] = dl.reshape(shape), nm.reshape(shape), nv.reshape(shape)

    return (loss, dx[None], *[grads[n] for n in WEIGHT_NAMES], *[deltas[n] for n in WEIGHT_NAMES],
            *[new_m[n] for n in WEIGHT_NAMES], *[new_v[n] for n in WEIGHT_NAMES])
```

```python
import functools
import math

import jax
import jax.numpy as jnp
from jax import lax
from jax.experimental import pallas as pl
from jax.experimental.pallas import tpu as pltpu

F32 = jnp.float32
BF16 = jnp.bfloat16

HEAD_DIM = 128
Q_PER_KV = 4
WINDOW = 128
BLOCK = 128
ROT_DIM = 32
ROPE_THETA = 500000.0
CONV_KERNEL = 31
CONV_PAD = (CONV_KERNEL - 1) // 2
CHUNK = 128
EPS = 1e-6

ADAM_LR = 0.001
ADAM_B1 = 0.9
ADAM_B2 = 0.999
ADAM_EPS = 1e-08
ADAM_WD = 0.01
ADAM_STEP = 10

N_DEV = 8
MESH_AXES = ("x", "y", "c")
VMEM_LIMIT_BYTES = 56 * 1024 * 1024
HALO_ROWS = 16
ADAMW_TILE_ELEMS = 256 * 1024
MESH_ID = pl.DeviceIdType.MESH


class Cfg:
    def __init__(self, d_model, seq):
        self.d = d_model
        self.t = seq
        self.attn = d_model // 2
        self.hq = self.attn // HEAD_DIM
        self.g = self.hq // Q_PER_KV
        self.kv = self.g * HEAD_DIM
        self.conv = d_model // 4
        self.sgu = d_model // 4
        self.sh = self.sgu // HEAD_DIM
        self.off_k = self.attn
        self.off_v = self.attn + self.kv
        self.off_ca = self.attn + 2 * self.kv
        self.off_cg = self.off_ca + self.conv
        self.off_u = self.off_cg + self.conv
        self.off_sv = self.off_u + self.sgu
        self.inw = self.off_sv + self.sgu


def _pick(dim, prefs):
    for p in prefs:
        if dim % p == 0:
            return p
    return dim


def _cparams(*sem):
    return pltpu.CompilerParams(dimension_semantics=sem, vmem_limit_bytes=VMEM_LIMIT_BYTES)


def _sigmoid(v):
    return 1.0 / (1.0 + jnp.exp(-v))


_DN = {
    "nn": (((1,), (0,)), ((), ())),
    "nt": (((1,), (1,)), ((), ())),
    "tn": (((0,), (0,)), ((), ())),
}


def _dot(a, b, mode):
    return lax.dot_general(a, b, _DN[mode], preferred_element_type=F32)


def _matmul(a, b, *, mode, tm, tn, tk, epilogue, out_dtypes, extras=(), name):
    if mode == "tn":
        kdim, m = a.shape
        n = b.shape[1]
    elif mode == "nn":
        m, kdim = a.shape
        n = b.shape[1]
    else:
        m, kdim = a.shape
        n = b.shape[0]
    assert m % tm == 0 and n % tn == 0 and kdim % tk == 0, (name, m, n, kdim, tm, tn, tk)
    gm, gn, gk = m // tm, n // tn, kdim // tk
    if mode == "tn":
        a_spec = pl.BlockSpec((tk, tm), lambda i, j, k: (k, i))
    else:
        a_spec = pl.BlockSpec((tm, tk), lambda i, j, k: (i, k))
    if mode == "nt":
        b_spec = pl.BlockSpec((tn, tk), lambda i, j, k: (j, k))
    else:
        b_spec = pl.BlockSpec((tk, tn), lambda i, j, k: (k, j))
    tile = pl.BlockSpec((tm, tn), lambda i, j, k: (i, j))
    ne, no = len(extras), len(out_dtypes)

    def body(a_ref, b_ref, *rest):
        ex = rest[:ne]
        outs = rest[ne:ne + no]
        part = _dot(a_ref[...], b_ref[...], mode)

        def finish(acc):
            vals = epilogue(acc, *[e[...] for e in ex])
            for o_ref, val in zip(outs, vals):
                o_ref[...] = val.astype(o_ref.dtype)

        if gk == 1:
            finish(part)
        else:
            acc_ref = rest[ne + no]
            k = pl.program_id(2)

            @pl.when(k == 0)
            def _():
                acc_ref[...] = part

            if gk > 2:
                @pl.when((k > 0) & (k < gk - 1))
                def _():
                    acc_ref[...] += part

            @pl.when(k == gk - 1)
            def _():
                finish(acc_ref[...] + part)

    return pl.pallas_call(
        body,
        grid=(gm, gn, gk),
        in_specs=[a_spec, b_spec] + [tile] * ne,
        out_specs=[tile] * no,
        out_shape=[jax.ShapeDtypeStruct((m, n), dt) for dt in out_dtypes],
        scratch_shapes=[pltpu.VMEM((tm, tn), F32)] if gk > 1 else [],
        compiler_params=_cparams("parallel", "parallel", "arbitrary"),
        name=name,
    )(a, b, *extras)


def _ep_plain(acc):
    return (acc,)


def _ep_add(acc, r):
    return (r.astype(F32) + acc,)


def _ep_swiglu_bwd(dact, gate, up):
    gate = gate.astype(F32)
    up = up.astype(F32)
    s = _sigmoid(gate)
    silu = gate * s
    dgate = dact * up * (s * (1.0 + gate * (1.0 - s)))
    dup = dact * silu
    return dgate, dup


def _ffn_up(h, wgt, wut, *, tm, tn, name):
    m, kdim = h.shape
    n = wgt.shape[0]

    def body(h_ref, g_ref, u_ref, gate_ref, up_ref, act_ref):
        hv = h_ref[...]
        gate = _dot(hv, g_ref[...], "nt")
        up = _dot(hv, u_ref[...], "nt")
        gate_ref[...] = gate.astype(BF16)
        up_ref[...] = up.astype(BF16)
        act_ref[...] = (gate * _sigmoid(gate) * up).astype(BF16)

    tile = pl.BlockSpec((tm, tn), lambda i, j: (i, j))
    wspec = pl.BlockSpec((tn, kdim), lambda i, j: (j, 0))
    return pl.pallas_call(
        body,
        grid=(m // tm, n // tn),
        in_specs=[pl.BlockSpec((tm, kdim), lambda i, j: (i, 0)), wspec, wspec],
        out_specs=[tile] * 3,
        out_shape=[jax.ShapeDtypeStruct((m, n), BF16)] * 3,
        compiler_params=_cparams("parallel", "parallel"),
        name=name,
    )(h, wgt, wut)


def _rms_fwd(x, g, *, tr, name):
    t, d = x.shape

    def body(x_ref, g_ref, h_ref):
        xv = x_ref[...]
        r = lax.rsqrt(jnp.mean(xv * xv, axis=-1, keepdims=True) + EPS)
        h_ref[...] = (xv * r * g_ref[...]).astype(BF16)

    row = pl.BlockSpec((tr, d), lambda i: (i, 0))
    return pl.pallas_call(
        body,
        grid=(t // tr,),
        in_specs=[row, pl.BlockSpec((1, d), lambda i: (0, 0))],
        out_specs=row,
        out_shape=jax.ShapeDtypeStruct((t, d), BF16),
        compiler_params=_cparams("parallel"),
        name=name,
    )(x, g)


def _rms_bwd_math(dy, xv, g):
    r = lax.rsqrt(jnp.mean(xv * xv, axis=-1, keepdims=True) + EPS)
    xh = xv * r
    dg = jnp.sum(dy * xh, axis=0, keepdims=True)
    dyg = dy * g
    dx = r * (dyg - xh * jnp.mean(dyg * xh, axis=-1, keepdims=True))
    return dx, dg


def _rms_bwd(dh, x, g, dres, *, tr, name):
    t, d = x.shape

    def body(dh_ref, x_ref, g_ref, dres_ref, dx_ref, dxb_ref, dg_ref):
        dx, dg = _rms_bwd_math(dh_ref[...], x_ref[...], g_ref[...])
        dx = dx + dres_ref[...]
        dx_ref[...] = dx
        dxb_ref[...] = dx.astype(BF16)

        @pl.when(pl.program_id(0) == 0)
        def _():
            dg_ref[...] = jnp.zeros_like(dg_ref)

        dg_ref[0:1, :] += dg

    row = pl.BlockSpec((tr, d), lambda i: (i, 0))
    vec = pl.BlockSpec((1, d), lambda i: (0, 0))
    return pl.pallas_call(
        body,
        grid=(t // tr,),
        in_specs=[row, row, vec, row],
        out_specs=[row, row, pl.BlockSpec((8, d), lambda i: (0, 0))],
        out_shape=[jax.ShapeDtypeStruct((t, d), F32), jax.ShapeDtypeStruct((t, d), BF16),
                   jax.ShapeDtypeStruct((8, d), F32)],
        compiler_params=_cparams("arbitrary"),
        name=name,
    )(dh, x, g, dres)


def _loss_head(x, g, target, *, tr, name):
    t, d = x.shape

    def body(x_ref, g_ref, t_ref, dx_ref, dxb_ref, st_ref):
        xv = x_ref[...]
        gv = g_ref[...]
        r = lax.rsqrt(jnp.mean(xv * xv, axis=-1, keepdims=True) + EPS)
        err = xv * r * gv - t_ref[...]
        sq = jnp.sum(jnp.sum(err * err, axis=1, keepdims=True), axis=0, keepdims=True)
        dx, dg = _rms_bwd_math(err * (1.0 / d), xv, gv)
        dx_ref[...] = dx
        dxb_ref[...] = dx.astype(BF16)

        @pl.when(pl.program_id(0) == 0)
        def _():
            st_ref[...] = jnp.zeros_like(st_ref)

        st_ref[0:1, :] += dg
        st_ref[1:2, :] += jnp.broadcast_to(sq * (0.5 / d), (1, d))

    row = pl.BlockSpec((tr, d), lambda i: (i, 0))
    return pl.pallas_call(
        body,
        grid=(t // tr,),
        in_specs=[row, pl.BlockSpec((1, d), lambda i: (0, 0)), row],
        out_specs=[row, row, pl.BlockSpec((8, d), lambda i: (0, 0))],
        out_shape=[jax.ShapeDtypeStruct((t, d), F32), jax.ShapeDtypeStruct((t, d), BF16),
                   jax.ShapeDtypeStruct((8, d), F32)],
        compiler_params=_cparams("arbitrary"),
        name=name,
    )(x, g, target)


def _rope_tables(t):
    half = ROT_DIM // 2
    pos = jnp.arange(t, dtype=F32)
    inv = ROPE_THETA ** (-jnp.arange(0, ROT_DIM, 2, dtype=F32) / ROT_DIM)
    ang = pos[:, None] * inv[None, :]
    cos, sin = jnp.cos(ang), jnp.sin(ang)
    rest = HEAD_DIM - ROT_DIM
    c = jnp.concatenate([cos, cos, jnp.ones((t, rest), F32)], axis=1)
    sa = jnp.concatenate([-sin, jnp.zeros((t, HEAD_DIM - half), F32)], axis=1)
    sb = jnp.concatenate([jnp.zeros((t, half), F32), sin, jnp.zeros((t, rest), F32)], axis=1)
    return c, sa, sb


def _rope_apply(v, c, sa, sb):
    half = ROT_DIM // 2
    return v * c + pltpu.roll(v, HEAD_DIM - half, axis=1) * sa + pltpu.roll(v, half, axis=1) * sb


def _rope_apply_t(dv, c, sa, sb):
    half = ROT_DIM // 2
    return dv * c + pltpu.roll(dv * sa, half, axis=1) + pltpu.roll(dv * sb, HEAD_DIM - half, axis=1)


def _rope_fwd(z, tabs, ncol, *, tr, name):
    t = z.shape[0]

    def body(z_ref, c_ref, sa_ref, sb_ref, o_ref):
        o_ref[...] = _rope_apply(z_ref[...].astype(F32), c_ref[...], sa_ref[...], sb_ref[...]).astype(BF16)

    blk = pl.BlockSpec((tr, HEAD_DIM), lambda i, j: (i, j))
    tab = pl.BlockSpec((tr, HEAD_DIM), lambda i, j: (i, 0))
    return pl.pallas_call(
        body,
        grid=(t // tr, ncol),
        in_specs=[blk, tab, tab, tab],
        out_specs=blk,
        out_shape=jax.ShapeDtypeStruct((t, ncol * HEAD_DIM), BF16),
        compiler_params=_cparams("parallel", "parallel"),
        name=name,
    )(z, *tabs)


def _attn_specs(cfg, nb):
    kcol = cfg.hq
    vcol = cfg.off_v // HEAD_DIM
    qw = Q_PER_KV * HEAD_DIM
    q_spec = pl.BlockSpec((BLOCK, qw), lambda g, n: (n, g))

    def kv(col, shift):
        def idx(g, n):
            return (jnp.clip(n + shift, 0, nb - 1), col + g)
        return pl.BlockSpec((BLOCK, HEAD_DIM), idx)

    k_specs = [kv(kcol, s) for s in (-1, 0, 1)]
    v_specs = [kv(vcol, s) for s in (-1, 0, 1)]
    return q_spec, k_specs, v_specs


def _attn_probs(q, k, valid, sk):
    scale = 1.0 / math.sqrt(HEAD_DIM)
    s = _dot(q, k, "nt") * scale
    s = jnp.where(valid, s, jnp.finfo(F32).min)
    m = jnp.maximum(jnp.max(s, axis=1, keepdims=True), sk)
    e = jnp.exp(s - m)
    es = jnp.exp(sk - m)
    inv = 1.0 / (jnp.sum(e, axis=1, keepdims=True) + es)
    return e * inv, es * inv


def _attn_valid(n, t):
    qpos = n * BLOCK + lax.broadcasted_iota(jnp.int32, (BLOCK, 3 * BLOCK), 0)
    kpos = (n - 1) * BLOCK + lax.broadcasted_iota(jnp.int32, (BLOCK, 3 * BLOCK), 1)
    return (kpos >= 0) & (kpos < t) & (jnp.abs(qpos - kpos) <= WINDOW)


def _attn_fwd(qk, z, sink, cfg, *, name):
    t = qk.shape[0]
    nb = t // BLOCK
    q_spec, k_specs, v_specs = _attn_specs(cfg, nb)

    def body(sink_ref, q_ref, kp, kc, kn, vp, vc, vn, o_ref):
        g = pl.program_id(0)
        n = pl.program_id(1)
        k = jnp.concatenate([kp[...], kc[...], kn[...]], axis=0)
        v = jnp.concatenate([vp[...], vc[...], vn[...]], axis=0)
        valid = _attn_valid(n, t)
        for r in range(Q_PER_KV):
            sl = slice(r * HEAD_DIM, (r + 1) * HEAD_DIM)
            p, _ = _attn_probs(q_ref[:, sl], k, valid, sink_ref[g * Q_PER_KV + r])
            o_ref[:, sl] = _dot(p.astype(BF16), v, "nn").astype(BF16)

    return pl.pallas_call(
        body,
        grid=(cfg.g, nb),
        in_specs=[pl.BlockSpec(memory_space=pltpu.SMEM), q_spec] + k_specs + v_specs,
        out_specs=q_spec,
        out_shape=jax.ShapeDtypeStruct((t, cfg.attn), BF16),
        compiler_params=_cparams("parallel", "parallel"),
        name=name,
    )(sink, qk, qk, qk, qk, z, z, z)


def _attn_bwd(qk, z, dmix, sink, tabs, cfg, *, name):
    t = qk.shape[0]
    nb = t // BLOCK
    q_spec, k_specs, v_specs = _attn_specs(cfg, nb)
    tab = pl.BlockSpec((BLOCK, HEAD_DIM), lambda g, n: (n, 0))
    acc_spec = pl.BlockSpec((None, t + 2 * BLOCK, HEAD_DIM), lambda g, n: (g, 0, 0))
    scale = 1.0 / math.sqrt(HEAD_DIM)

    def body(sink_ref, q_ref, kp, kc, kn, vp, vc, vn, do_ref, c_ref, sa_ref, sb_ref,
             dq_ref, dk_ref, dv_ref, ds_ref):
        g = pl.program_id(0)
        n = pl.program_id(1)

        @pl.when(n == 0)
        def _():
            dk_ref[...] = jnp.zeros_like(dk_ref)
            dv_ref[...] = jnp.zeros_like(dv_ref)
            ds_ref[...] = jnp.zeros_like(ds_ref)

        k = jnp.concatenate([kp[...], kc[...], kn[...]], axis=0)
        v = jnp.concatenate([vp[...], vc[...], vn[...]], axis=0)
        valid = _attn_valid(n, t)
        dk_acc = jnp.zeros((3 * BLOCK, HEAD_DIM), F32)
        dv_acc = jnp.zeros((3 * BLOCK, HEAD_DIM), F32)
        for r in range(Q_PER_KV):
            sl = slice(r * HEAD_DIM, (r + 1) * HEAD_DIM)
            q = q_ref[:, sl]
            do = do_ref[:, sl]
            p, ps = _attn_probs(q, k, valid, sink_ref[g * Q_PER_KV + r])
            dp = _dot(do, v, "nt")
            dv_acc = dv_acc + _dot(p.astype(BF16), do, "tn")
            delta = jnp.sum(p * dp, axis=1, keepdims=True)
            dsc = (p * (dp - delta) * scale).astype(BF16)
            dq = _dot(dsc, k, "nn")
            dk_acc = dk_acc + _dot(dsc, q, "tn")
            dq_ref[:, sl] = _rope_apply_t(dq, c_ref[...], sa_ref[...], sb_ref[...]).astype(BF16)
            dsink = -jnp.sum(ps * delta, axis=0, keepdims=True)
            ds_ref[r:r + 1, :] += jnp.broadcast_to(dsink, (1, HEAD_DIM))
        rows = pl.ds(pl.multiple_of(n * BLOCK, BLOCK), 3 * BLOCK)
        dk_ref[rows, :] += dk_acc
        dv_ref[rows, :] += dv_acc

    acc_shape = jax.ShapeDtypeStruct((cfg.g, t + 2 * BLOCK, HEAD_DIM), F32)
    return pl.pallas_call(
        body,
        grid=(cfg.g, nb),
        in_specs=[pl.BlockSpec(memory_space=pltpu.SMEM), q_spec] + k_specs + v_specs + [q_spec, tab, tab, tab],
        out_specs=[q_spec, acc_spec, acc_spec, pl.BlockSpec((None, 8, HEAD_DIM), lambda g, n: (g, 0, 0))],
        out_shape=[jax.ShapeDtypeStruct((t, cfg.attn), BF16), acc_shape, acc_shape,
                   jax.ShapeDtypeStruct((cfg.g, 8, HEAD_DIM), F32)],
        compiler_params=_cparams("arbitrary", "arbitrary"),
        name=name,
    )(sink, qk, qk, qk, qk, z, z, z, dmix, *tabs)


def _kv_finish(dk_acc, dv_acc, tabs, cfg, *, name):
    t = dk_acc.shape[1] - 2 * BLOCK

    def body(dk_ref, dv_ref, c_ref, sa_ref, sb_ref, ok_ref, ov_ref):
        ok_ref[...] = _rope_apply_t(dk_ref[...], c_ref[...], sa_ref[...], sb_ref[...]).astype(BF16)
        ov_ref[...] = dv_ref[...].astype(BF16)

    acc = pl.BlockSpec((None, BLOCK, HEAD_DIM), lambda g, i: (g, i + 1, 0))
    tab = pl.BlockSpec((BLOCK, HEAD_DIM), lambda g, i: (i, 0))
    out = pl.BlockSpec((BLOCK, HEAD_DIM), lambda g, i: (i, g))
    return pl.pallas_call(
        body,
        grid=(cfg.g, t // BLOCK),
        in_specs=[acc, acc, tab, tab, tab],
        out_specs=[out, out],
        out_shape=[jax.ShapeDtypeStruct((t, cfg.kv), BF16)] * 2,
        compiler_params=_cparams("parallel", "parallel"),
        name=name,
    )(dk_acc, dv_acc, *tabs)


def _halo_specs(width, col, tb, t):
    per = tb // HALO_ROWS
    last = t // HALO_ROWS - 1
    prev = pl.BlockSpec((HALO_ROWS, width), lambda i: (jnp.maximum(i * per - 1, 0), col))
    cur = pl.BlockSpec((tb, width), lambda i: (i, col))
    nxt = pl.BlockSpec((HALO_ROWS, width), lambda i: (jnp.minimum((i + 1) * per, last), col))
    return [prev, cur, nxt]


def _halo_load(refs):
    return jnp.concatenate([r[...].astype(F32) for r in refs], axis=0)


def _shift_rows(v, start, rows):
    total = v.shape[0]
    return pltpu.roll(v, (total - start) % total, axis=0)[0:rows]


def _conv_glu(a_refs, g_refs, i, tb, t):
    a = _halo_load(a_refs)
    g = _halo_load(g_refs)
    rows = i * tb - HALO_ROWS + lax.broadcasted_iota(jnp.int32, (tb + 2 * HALO_ROWS, 1), 0)
    valid = (rows >= 0) & (rows < t)
    sg = _sigmoid(g)
    return a, sg, jnp.where(valid, a * sg, 0.0), valid


def _conv_fwd(z, w, b, lg, lb, cfg, *, tb, name):
    t = z.shape[0]
    cw = cfg.conv
    vec = pl.BlockSpec((1, cw), lambda i: (0, 0))

    def body(ap, ac, an, gp, gc, gn, w_ref, b_ref, lg_ref, lb_ref, o_ref, y_ref):
        _, _, c, _ = _conv_glu((ap, ac, an), (gp, gc, gn), pl.program_id(0), tb, t)
        acc = jnp.zeros((tb, cw), F32)
        for j in range(CONV_KERNEL):
            acc = acc + w_ref[j:j + 1, :] * _shift_rows(c, j + HALO_ROWS - CONV_PAD, tb)
        y = acc + b_ref[...]
        y_ref[...] = y
        mu = jnp.mean(y, axis=-1, keepdims=True)
        dlt = y - mu
        var = jnp.mean(dlt * dlt, axis=-1, keepdims=True)
        yn = dlt * lax.rsqrt(var + EPS) * lg_ref[...] + lb_ref[...]
        o_ref[...] = (yn * _sigmoid(yn)).astype(BF16)

    out = pl.BlockSpec((tb, cw), lambda i: (i, 0))
    return pl.pallas_call(
        body,
        grid=(t // tb,),
        in_specs=(_halo_specs(cw, cfg.off_ca // cw, tb, t) + _halo_specs(cw, cfg.off_cg // cw, tb, t)
                  + [pl.BlockSpec((CONV_KERNEL + 1, cw), lambda i: (0, 0)), vec, vec, vec]),
        out_specs=[out, out],
        out_shape=[jax.ShapeDtypeStruct((t, cw), BF16), jax.ShapeDtypeStruct((t, cw), F32)],
        compiler_params=_cparams("parallel"),
        name=name,
    )(z, z, z, z, z, z, w, b, lg, lb)


def _conv_bwd(z, y, dmix, w, lg, lb, cfg, *, tb, name):
    t = z.shape[0]
    cw = cfg.conv
    vec = pl.BlockSpec((1, cw), lambda i: (0, 0))
    cen = slice(HALO_ROWS, HALO_ROWS + tb)

    def body(ap, ac, an, gp, gc, gn, yp, yc, yn_, dp, dc_, dn, w_ref, lg_ref, lb_ref,
             da_ref, dg_ref, dw_ref, st_ref):
        @pl.when(pl.program_id(0) == 0)
        def _():
            dw_ref[...] = jnp.zeros_like(dw_ref)
            st_ref[...] = jnp.zeros_like(st_ref)

        a, sg, c, valid = _conv_glu((ap, ac, an), (gp, gc, gn), pl.program_id(0), tb, t)
        yv = _halo_load((yp, yc, yn_))
        do = _halo_load((dp, dc_, dn))
        mu = jnp.mean(yv, axis=-1, keepdims=True)
        dlt = yv - mu
        rstd = lax.rsqrt(jnp.mean(dlt * dlt, axis=-1, keepdims=True) + EPS)
        xh = dlt * rstd
        lgv = lg_ref[...]
        yn = xh * lgv + lb_ref[...]
        s = _sigmoid(yn)
        dyn = do * (s * (1.0 + yn * (1.0 - s)))
        dxh = dyn * lgv
        dy = rstd * (dxh - jnp.mean(dxh, axis=-1, keepdims=True)
                     - xh * jnp.mean(dxh * xh, axis=-1, keepdims=True))
        dy = jnp.where(valid, dy, 0.0)
        dyc = dy[cen]
        st_ref[0:1, :] += jnp.sum(dyc, axis=0, keepdims=True)
        st_ref[1:2, :] += jnp.sum((dyn * xh)[cen], axis=0, keepdims=True)
        st_ref[2:3, :] += jnp.sum(dyn[cen], axis=0, keepdims=True)
        dc = jnp.zeros((tb, cw), F32)
        for j in range(CONV_KERNEL):
            dc = dc + w_ref[j:j + 1, :] * _shift_rows(dy, HALO_ROWS + CONV_PAD - j, tb)
            dw_ref[j:j + 1, :] += jnp.sum(dyc * _shift_rows(c, j + HALO_ROWS - CONV_PAD, tb),
                                          axis=0, keepdims=True)
        sgc = sg[cen]
        da_ref[...] = (dc * sgc).astype(BF16)
        dg_ref[...] = (dc * a[cen] * sgc * (1.0 - sgc)).astype(BF16)

    out = pl.BlockSpec((tb, cw), lambda i: (i, 0))
    wspec = pl.BlockSpec((CONV_KERNEL + 1, cw), lambda i: (0, 0))
    return pl.pallas_call(
        body,
        grid=(t // tb,),
        in_specs=(_halo_specs(cw, cfg.off_ca // cw, tb, t) + _halo_specs(cw, cfg.off_cg // cw, tb, t)
                  + _halo_specs(cw, 0, tb, t) + _halo_specs(cw, cfg.attn // cw, tb, t) + [wspec, vec, vec]),
        out_specs=[out, out, wspec, pl.BlockSpec((8, cw), lambda i: (0, 0))],
        out_shape=[jax.ShapeDtypeStruct((t, cw), BF16), jax.ShapeDtypeStruct((t, cw), BF16),
                   jax.ShapeDtypeStruct((CONV_KERNEL + 1, cw), F32), jax.ShapeDtypeStruct((8, cw), F32)],
        compiler_params=_cparams("arbitrary"),
        name=name,
    )(z, z, z, z, z, z, y, y, y, dmix, dmix, dmix, w, lg, lb)


_SQRT_HALF = 1.0 / math.sqrt(2.0)
_INV_SQRT_2PI = 1.0 / math.sqrt(2.0 * math.pi)


def _gelu(v):
    return 0.5 * v * (1.0 + lax.erf(v * _SQRT_HALF))


def _gelu_grad(v):
    return 0.5 * (1.0 + lax.erf(v * _SQRT_HALF)) + v * jnp.exp(-0.5 * v * v) * _INV_SQRT_2PI


def _sgu_norm(zv_ref, lg_ref, lb_ref):
    xv = zv_ref[...].astype(F32)
    v = _gelu(xv)
    mu = jnp.mean(v, axis=-1, keepdims=True)
    dlt = v - mu
    rstd = lax.rsqrt(jnp.mean(dlt * dlt, axis=-1, keepdims=True) + EPS)
    xh = dlt * rstd
    return xv, xh, rstd, xh * lg_ref[...] + lb_ref[...]


def _sgu_specs(cfg):
    sw = cfg.sgu
    zu = pl.BlockSpec((CHUNK, sw), lambda i: (i, cfg.off_u // sw))
    zv = pl.BlockSpec((CHUNK, sw), lambda i: (i, cfg.off_sv // sw))
    vec = pl.BlockSpec((1, sw), lambda i: (0, 0))
    ws = pl.BlockSpec((cfg.sh, CHUNK, CHUNK), lambda i: (0, 0, 0))
    bs = pl.BlockSpec((cfg.sh, CHUNK, 1), lambda i: (0, 0, 0))
    return zu, zv, vec, ws, bs


def _sgu_fwd(z, lg, lb, ws, bs, cfg, *, name):
    t = z.shape[0]
    sw = cfg.sgu
    zu, zv, vec, wspec, bspec = _sgu_specs(cfg)

    def body(zu_ref, zv_ref, lg_ref, lb_ref, ws_ref, bs_ref, o_ref):
        u = _gelu(zu_ref[...].astype(F32))
        _, _, _, vn = _sgu_norm(zv_ref, lg_ref, lb_ref)
        vnb = vn.astype(BF16)
        for h in range(cfg.sh):
            sl = slice(h * HEAD_DIM, (h + 1) * HEAD_DIM)
            sp = _dot(ws_ref[h], vnb[:, sl], "nn") + bs_ref[h]
            o_ref[:, sl] = (u[:, sl] * sp).astype(BF16)

    return pl.pallas_call(
        body,
        grid=(t // CHUNK,),
        in_specs=[zu, zv, vec, vec, wspec, bspec],
        out_specs=pl.BlockSpec((CHUNK, sw), lambda i: (i, 0)),
        out_shape=jax.ShapeDtypeStruct((t, sw), BF16),
        compiler_params=_cparams("parallel"),
        name=name,
    )(z, z, lg, lb, ws, bs)


def _sgu_bwd(z, dmix, lg, lb, ws, bs, cfg, *, name):
    t = z.shape[0]
    sw = cfg.sgu
    zu, zv, vec, wspec, bspec = _sgu_specs(cfg)
    do_spec = pl.BlockSpec((CHUNK, sw), lambda i: (i, (cfg.attn + cfg.conv) // sw))

    def body(zu_ref, zv_ref, do_ref, lg_ref, lb_ref, ws_ref, bs_ref, duv_ref, dws_ref, dbs_ref, st_ref):
        @pl.when(pl.program_id(0) == 0)
        def _():
            dws_ref[...] = jnp.zeros_like(dws_ref)
            dbs_ref[...] = jnp.zeros_like(dbs_ref)
            st_ref[...] = jnp.zeros_like(st_ref)

        xu = zu_ref[...].astype(F32)
        u = _gelu(xu)
        xv, xh, rstd, vn = _sgu_norm(zv_ref, lg_ref, lb_ref)
        vnb = vn.astype(BF16)
        do = do_ref[...].astype(F32)
        dvn_parts = []
        for h in range(cfg.sh):
            sl = slice(h * HEAD_DIM, (h + 1) * HEAD_DIM)
            wh = ws_ref[h]
            sp = _dot(wh, vnb[:, sl], "nn") + bs_ref[h]
            dsp = do[:, sl] * u[:, sl]
            dspb = dsp.astype(BF16)
            dvn_parts.append(_dot(wh, dspb, "tn"))
            dws_ref[h] += _dot(dspb, vnb[:, sl], "nt")
            dbs_ref[h] += jnp.sum(dsp, axis=1, keepdims=True)
            duv_ref[:, sl] = (do[:, sl] * sp * _gelu_grad(xu[:, sl])).astype(BF16)
        dvn = jnp.concatenate(dvn_parts, axis=1)
        st_ref[0:1, :] += jnp.sum(dvn * xh, axis=0, keepdims=True)
        st_ref[1:2, :] += jnp.sum(dvn, axis=0, keepdims=True)
        dxh = dvn * lg_ref[...]
        dv = rstd * (dxh - jnp.mean(dxh, axis=-1, keepdims=True)
                     - xh * jnp.mean(dxh * xh, axis=-1, keepdims=True))
        duv_ref[:, sw:] = (dv * _gelu_grad(xv)).astype(BF16)

    return pl.pallas_call(
        body,
        grid=(t // CHUNK,),
        in_specs=[zu, zv, do_spec, vec, vec, wspec, bspec],
        out_specs=[pl.BlockSpec((CHUNK, 2 * sw), lambda i: (i, 0)), wspec, bspec,
                   pl.BlockSpec((8, sw), lambda i: (0, 0))],
        out_shape=[jax.ShapeDtypeStruct((t, 2 * sw), BF16), jax.ShapeDtypeStruct((cfg.sh, CHUNK, CHUNK), F32),
                   jax.ShapeDtypeStruct((cfg.sh, CHUNK, 1), F32), jax.ShapeDtypeStruct((8, sw), F32)],
        compiler_params=_cparams("arbitrary"),
        name=name,
    )(z, z, dmix, lg, lb, ws, bs)


def _sum_shards(parts, *, tr, name):
    _, r, c = parts.shape

    def body(p_ref, o_ref):
        acc = p_ref[0].astype(F32)
        for q in range(1, N_DEV):
            acc = acc + p_ref[q].astype(F32)
        o_ref[...] = acc

    return pl.pallas_call(
        body,
        grid=(r // tr,),
        in_specs=[pl.BlockSpec((N_DEV, tr, c), lambda i: (0, i, 0))],
        out_specs=pl.BlockSpec((tr, c), lambda i: (i, 0)),
        out_shape=jax.ShapeDtypeStruct((r, c), F32),
        compiler_params=_cparams("parallel"),
        name=name,
    )(parts)


def _adamw(w, g, m, v, *, name):
    r, c = w.shape
    tr = _pick(r, [p for p in (1024, 512, 256, 128, 64, 32, 16, 8) if p * c <= ADAMW_TILE_ELEMS])

    def body(w_ref, g_ref, m_ref, v_ref, d_ref, nm_ref, nv_ref):
        gv = g_ref[...]
        nm = ADAM_B1 * m_ref[...] + (1.0 - ADAM_B1) * gv
        nv = ADAM_B2 * v_ref[...] + (1.0 - ADAM_B2) * (gv * gv)
        m_hat = nm / (1.0 - ADAM_B1 ** ADAM_STEP)
        v_hat = nv / (1.0 - ADAM_B2 ** ADAM_STEP)
        d_ref[...] = -ADAM_LR * (m_hat / (jnp.sqrt(v_hat) + ADAM_EPS) + ADAM_WD * w_ref[...])
        nm_ref[...] = nm
        nv_ref[...] = nv

    blk = pl.BlockSpec((tr, c), lambda i: (i, 0))
    return pl.pallas_call(
        body,
        grid=(r // tr,),
        in_specs=[blk] * 4,
        out_specs=[blk] * 3,
        out_shape=[jax.ShapeDtypeStruct((r, c), F32)] * 3,
        compiler_params=_cparams("parallel"),
        name=name,
    )(w, g, m, v)


def _my_coords():
    return tuple(lax.axis_index(a) for a in MESH_AXES)


def _peer_coords(me, p):
    return tuple(1 - v if (p >> (2 - a)) & 1 else v for a, v in enumerate(me))


def _flat_id(coords):
    return 4 * coords[0] + 2 * coords[1] + coords[2]


def _exchange(arrs, *, scatter, name):
    na = len(arrs)

    def body(*refs):
        ins = refs[:na]
        outs = refs[na:2 * na]
        send_sems, recv_sems, local_sems = refs[2 * na:]
        me = _my_coords()
        my_id = _flat_id(me)

        local = []
        for k in range(na):
            src = ins[k].at[my_id] if scatter else ins[k]
            cp = pltpu.make_async_copy(src, outs[k].at[my_id], local_sems.at[k])
            cp.start()
            local.append(cp)

        def remote(p, k):
            peer = _peer_coords(me, p)
            peer_id = _flat_id(peer)
            sem = (p - 1) * na + k
            src = ins[k].at[peer_id] if scatter else ins[k]
            send = pltpu.make_async_remote_copy(
                src_ref=src, dst_ref=outs[k].at[my_id], send_sem=send_sems.at[sem],
                recv_sem=recv_sems.at[sem], device_id=peer, device_id_type=MESH_ID)
            recv = pltpu.make_async_remote_copy(
                src_ref=src, dst_ref=outs[k].at[peer_id], send_sem=send_sems.at[sem],
                recv_sem=recv_sems.at[sem], device_id=peer, device_id_type=MESH_ID)
            return send, recv

        pairs = [remote(p, k) for p in range(1, N_DEV) for k in range(na)]
        for send, _ in pairs:
            send.start()
        for _, recv in pairs:
            recv.wait_recv()
        for send, _ in pairs:
            send.wait_send()
        for cp in local:
            cp.wait()

    def out_of(a):
        return jax.ShapeDtypeStruct(a.shape if scatter else (N_DEV,) + a.shape, a.dtype)

    hbm = pl.BlockSpec(memory_space=pl.ANY)
    nsem = (N_DEV - 1) * na
    return pl.pallas_call(
        body,
        in_specs=[hbm] * na,
        out_specs=[hbm] * na,
        out_shape=[out_of(a) for a in arrs],
        scratch_shapes=[pltpu.SemaphoreType.DMA((nsem,)), pltpu.SemaphoreType.DMA((nsem,)),
                        pltpu.SemaphoreType.DMA((na,))],
        name=name,
    )(*arrs)


_HBM = pl.BlockSpec(memory_space=pltpu.HBM)
_SEM = pl.BlockSpec(memory_space=pltpu.SEMAPHORE)
_EFFECT = pltpu.SideEffectType.DATAFLOW_SIDE_EFFECTING


def _place_own(land, src, *, scatter, name):
    _, r, c = land.shape
    tr = _pick(r, (256, 128, 64, 32, 16))
    my_id = _flat_id(_my_coords()).astype(jnp.int32).reshape(1)

    def body(me_ref, land_ref, src_ref, out_ref):
        out_ref[...] = src_ref[...]

    if scatter:
        src_spec = pl.BlockSpec((None, tr, c), lambda i, me: (me[0], i, 0))
    else:
        src_spec = pl.BlockSpec((tr, c), lambda i, me: (i, 0))
    return pl.pallas_call(
        body,
        grid_spec=pltpu.PrefetchScalarGridSpec(
            num_scalar_prefetch=1,
            grid=(r // tr,),
            in_specs=[pl.BlockSpec(memory_space=pl.ANY), src_spec],
            out_specs=pl.BlockSpec((None, tr, c), lambda i, me: (me[0], i, 0)),
        ),
        out_shape=jax.ShapeDtypeStruct(land.shape, land.dtype),
        input_output_aliases={1: 0},
        compiler_params=_cparams("parallel"),
        name=name,
    )(my_id, land, src)


def _peer_copy(src_ref, land_ref, send_sems, recv_sems, me, p, scatter, arrival):
    peer = _peer_coords(me, p)
    peer_id = _flat_id(peer)
    return pltpu.make_async_remote_copy(
        src_ref=src_ref.at[peer_id] if scatter else src_ref,
        dst_ref=land_ref.at[peer_id if arrival else _flat_id(me)],
        send_sem=send_sems.at[p - 1], recv_sem=recv_sems.at[p - 1], device_id=peer, device_id_type=MESH_ID)


def _exchange_start(src, land, *, scatter, name):
    def body(src_ref, land_ref, send_sems, recv_sems, src_thru, land_thru, token):
        me = _my_coords()
        for p in range(1, N_DEV):
            _peer_copy(src_ref, land_ref, send_sems, recv_sems, me, p, scatter, False).start()
        token[...] = jnp.zeros_like(token)

    nsem = N_DEV - 1
    return pl.pallas_call(
        body,
        name=name,
        out_shape=(pltpu.SemaphoreType.DMA((nsem,)), pltpu.SemaphoreType.DMA((nsem,)),
                   pltpu.HBM(src.shape, src.dtype), pltpu.HBM(land.shape, land.dtype),
                   jax.ShapeDtypeStruct((8, 128), F32)),
        in_specs=(_HBM, _HBM),
        out_specs=(_SEM, _SEM, _HBM, _HBM, pl.BlockSpec(memory_space=pltpu.VMEM)),
        input_output_aliases={0: 2, 1: 3},
        compiler_params=pltpu.CompilerParams(has_side_effects=_EFFECT),
    )(pltpu.with_memory_space_constraint(src, pltpu.HBM), pltpu.with_memory_space_constraint(land, pltpu.HBM))


def _exchange_wait(handle, after, *, scatter, name):
    send_sems, recv_sems, src_thru, land_thru = handle

    def body(src_ref, land_ref, send_sems, recv_sems, after_ref, src_dead, got_ref):
        me = _my_coords()
        for p in range(1, N_DEV):
            _peer_copy(src_ref, land_ref, send_sems, recv_sems, me, p, scatter, False).wait_send()
            _peer_copy(src_ref, land_ref, send_sems, recv_sems, me, p, scatter, True).wait_recv()

    return pl.pallas_call(
        body,
        name=name,
        out_shape=(pltpu.HBM(src_thru.shape, src_thru.dtype), pltpu.HBM(land_thru.shape, land_thru.dtype)),
        in_specs=(_HBM, _HBM, _SEM, _SEM, pl.BlockSpec(memory_space=pl.ANY)),
        out_specs=(_HBM, _HBM),
        input_output_aliases={0: 0, 1: 1},
        compiler_params=pltpu.CompilerParams(has_side_effects=_EFFECT),
    )(src_thru, land_thru, send_sems, recv_sems, after)[1]


def _exchange_begin(src, *, scatter, name):
    shape = src.shape if scatter else (N_DEV,) + src.shape
    land = _place_own(lax.empty(shape, src.dtype), src, scatter=scatter, name=name + "_own")
    *handle, token = _exchange_start(src, land, scatter=scatter, name=name + "_start")
    return tuple(handle), token[0, 0]


def _allreduce_small(flat, *, name):
    r, c = flat.shape

    def body(in_ref, out_ref, buf, send_sems, recv_sems):
        me = _my_coords()
        my_id = _flat_id(me)
        buf[my_id] = in_ref[...]

        def remote(p):
            peer = _peer_coords(me, p)
            send = pltpu.make_async_remote_copy(
                src_ref=in_ref, dst_ref=buf.at[my_id], send_sem=send_sems.at[p - 1],
                recv_sem=recv_sems.at[p - 1], device_id=peer, device_id_type=MESH_ID)
            recv = pltpu.make_async_remote_copy(
                src_ref=in_ref, dst_ref=buf.at[_flat_id(peer)], send_sem=send_sems.at[p - 1],
                recv_sem=recv_sems.at[p - 1], device_id=peer, device_id_type=MESH_ID)
            return send, recv

        pairs = [remote(p) for p in range(1, N_DEV)]
        for send, _ in pairs:
            send.start()
        for _, recv in pairs:
            recv.wait_recv()
        for send, _ in pairs:
            send.wait_send()
        acc = buf[0]
        for q in range(1, N_DEV):
            acc = acc + buf[q]
        out_ref[...] = acc

    vmem = pl.BlockSpec(memory_space=pltpu.VMEM)
    return pl.pallas_call(
        body,
        in_specs=[vmem],
        out_specs=vmem,
        out_shape=jax.ShapeDtypeStruct((r, c), F32),
        scratch_shapes=[pltpu.VMEM((N_DEV, r, c), F32), pltpu.SemaphoreType.DMA((N_DEV - 1,)),
                        pltpu.SemaphoreType.DMA((N_DEV - 1,))],
        compiler_params=pltpu.CompilerParams(vmem_limit_bytes=VMEM_LIMIT_BYTES),
        name=name,
    )(flat)


WEIGHT_NAMES = ("mix_norm_g", "w_in", "sink", "conv_dw_w", "conv_dw_b", "conv_ln_g", "conv_ln_b",
                "sgu_ln_g", "sgu_ln_b", "sgu_w", "sgu_b", "w_out", "ffn_norm_g", "w_gate", "w_up",
                "w_down", "final_norm_g")
SHARDED = ("w_in", "conv_dw_w", "w_out", "w_gate", "w_up", "w_down")
SMALL = tuple(n for n in WEIGHT_NAMES if n not in ("w_in", "w_out", "w_gate", "w_up", "w_down"))


def _pack_small(parts):
    flat = jnp.concatenate([parts[n].reshape(-1) for n in SMALL])
    pad = (-flat.shape[0]) % (8 * 128)
    return jnp.pad(flat, (0, pad)).reshape(-1, 128)


def _unpack_small(packed, shapes):
    flat = packed.reshape(-1)
    out, pos = {}, 0
    for n in SMALL:
        size = math.prod(shapes[n])
        out[n] = flat[pos:pos + size].reshape(shapes[n])
        pos += size
    return out


def kernel(x, mix_norm_g, w_in, sink, conv_dw_w, conv_dw_b, conv_ln_g, conv_ln_b, sgu_ln_g, sgu_ln_b, sgu_w, sgu_b, w_out, ffn_norm_g, w_gate, w_up, w_down, final_norm_g, loss_target, m_mix_norm_g, m_w_in, m_sink, m_conv_dw_w, m_conv_dw_b, m_conv_ln_g, m_conv_ln_b, m_sgu_ln_g, m_sgu_ln_b, m_sgu_w, m_sgu_b, m_w_out, m_ffn_norm_g, m_w_gate, m_w_up, m_w_down, m_final_norm_g, v_mix_norm_g, v_w_in, v_sink, v_conv_dw_w, v_conv_dw_b, v_conv_ln_g, v_conv_ln_b, v_sgu_ln_g, v_sgu_ln_b, v_sgu_w, v_sgu_b, v_w_out, v_ffn_norm_g, v_w_gate, v_w_up, v_w_down, v_final_norm_g):
    w = dict(mix_norm_g=mix_norm_g, w_in=w_in, sink=sink, conv_dw_w=conv_dw_w, conv_dw_b=conv_dw_b,
             conv_ln_g=conv_ln_g, conv_ln_b=conv_ln_b, sgu_ln_g=sgu_ln_g, sgu_ln_b=sgu_ln_b, sgu_w=sgu_w,
             sgu_b=sgu_b, w_out=w_out, ffn_norm_g=ffn_norm_g, w_gate=w_gate, w_up=w_up, w_down=w_down,
             final_norm_g=final_norm_g)
    mom_m = dict(zip(WEIGHT_NAMES, (m_mix_norm_g, m_w_in, m_sink, m_conv_dw_w, m_conv_dw_b, m_conv_ln_g,
                                    m_conv_ln_b, m_sgu_ln_g, m_sgu_ln_b, m_sgu_w, m_sgu_b, m_w_out,
                                    m_ffn_norm_g, m_w_gate, m_w_up, m_w_down, m_final_norm_g)))
    mom_v = dict(zip(WEIGHT_NAMES, (v_mix_norm_g, v_w_in, v_sink, v_conv_dw_w, v_conv_dw_b, v_conv_ln_g,
                                    v_conv_ln_b, v_sgu_ln_g, v_sgu_ln_b, v_sgu_w, v_sgu_b, v_w_out,
                                    v_ffn_norm_g, v_w_gate, v_w_up, v_w_down, v_final_norm_g)))

    _, t, d = x.shape
    depth = w_in.shape[0]
    cfg = Cfg(d, t)
    ff = w_gate.shape[2] * N_DEV
    my_id = _flat_id(_my_coords())
    xs = x[0]
    target = loss_target[0]

    tm = _pick(t, (1024, 512))
    tr = _pick(t, (256, 128))
    tb = _pick(t, (256, 128))
    tn_in = _pick(cfg.inw, (896, 512, 448))
    tn_ff = _pick(ff, (512, 1408, 704))
    tk_ff = ff
    tk_in = cfg.inw
    tn_d = _pick(d, (512,))
    tk_t = _pick(t, (2048, 1024, 512))
    tm_in = _pick(cfg.inw, (896, 448))
    tm_ff = _pick(ff, (1408, 704))
    tn_dw = _pick(d, (1024,))

    tabs = _rope_tables(t)

    cflat = conv_dw_w.reshape(-1)
    cshard = jnp.pad(cflat, (0, (-cflat.shape[0]) % (8 * 128))).reshape(-1, 128)
    dw_all = _exchange([cshard], scatter=False, name="gather_conv_w")[0]
    dw_all = dw_all.reshape(N_DEV, -1)[:, :cflat.shape[0]].reshape(N_DEV, depth, CONV_KERNEL, -1)
    dw_all = dw_all.transpose(1, 2, 0, 3).reshape(depth, CONV_KERNEL, cfg.conv)
    dw_pad = jnp.pad(dw_all, ((0, 0), (0, 1), (0, 0)))

    def row(v):
        return v.reshape(1, -1)

    gathers = [None] * depth

    def gather_begin(l, zero):
        shards = dict(win=w_in[l].T, wo=w_out[l], wg=w_gate[l].T, wu=w_up[l].T, wd=w_down[l])
        gathers[l] = {}
        for k, v in shards.items():
            gathers[l][k], started = _exchange_begin((v + zero).astype(BF16), scatter=False,
                                                     name=f"gather_{k}_{l}")
            zero = zero + started
        return zero

    def gathered(l, k, after):
        full = _exchange_wait(gathers[l][k], after, scatter=False, name=f"gather_{k}_{l}_wait")
        return full.reshape(-1, d)

    saved = []
    scatters = [None] * depth
    dw_pad, started = lax.optimization_barrier((dw_pad, jnp.zeros((), F32)))
    started = gather_begin(0, started)
    for l in range(depth):
        h = _rms_fwd(xs, row(mix_norm_g[l]) + started, tr=tr, name="mix_norm")
        win_t = gathered(l, "win", h)
        z = _matmul(h, win_t, mode="nt", tm=tm, tn=tn_in, tk=d, epilogue=_ep_plain,
                    out_dtypes=[BF16], name="in_proj")[0]
        started = jnp.zeros((), F32)
        if l + 1 < depth:
            z, started = lax.optimization_barrier((z, started))
            started = gather_begin(l + 1, started)
        qk = _rope_fwd(z, tabs, cfg.hq + cfg.g, tr=tm, name="rope")
        attn = _attn_fwd(qk, z, sink[l], cfg, name="attn_fwd")
        conv, conv_y = _conv_fwd(z, dw_pad[l], row(conv_dw_b[l]), row(conv_ln_g[l]) + started,
                                 row(conv_ln_b[l]), cfg, tb=tb, name="conv_fwd")
        ws_b = sgu_w[l].astype(BF16)
        bs_c = sgu_b[l][:, :, None]
        sgu = _sgu_fwd(z, row(sgu_ln_g[l]), row(sgu_ln_b[l]), ws_b, bs_c, cfg, name="sgu_fwd")
        mix = jnp.concatenate([attn, conv, sgu], axis=1)
        wo = gathered(l, "wo", mix)
        x1 = _matmul(mix, wo, mode="nn", tm=tm, tn=tn_d, tk=d, epilogue=_ep_add, extras=(xs,),
                     out_dtypes=[F32], name="out_proj")[0]
        h2 = _rms_fwd(x1, row(ffn_norm_g[l]), tr=tr, name="ffn_norm")
        wg_t = gathered(l, "wg", h2)
        wu_t = gathered(l, "wu", h2)
        gate, up, act = _ffn_up(h2, wg_t, wu_t, tm=tm, tn=tn_ff, name="ffn_up")
        wd = gathered(l, "wd", act)
        x2 = _matmul(act, wd, mode="nn", tm=tm, tn=tn_d, tk=tk_ff, epilogue=_ep_add, extras=(x1,),
                     out_dtypes=[F32], name="ffn_down")[0]
        saved.append(dict(x0=xs, h=h, z=z, qk=qk, mix=mix, conv_y=conv_y, x1=x1, h2=h2, gate=gate, up=up,
                          act=act, win_t=win_t, wg_t=wg_t, wu_t=wu_t, wo=wo, wd=wd, ws_b=ws_b, bs_c=bs_c))
        xs = x2

    dx, dxb, head = _loss_head(xs, row(final_norm_g), target, tr=tr, name="loss_head")
    loss = lax.psum(head[1, 0], MESH_AXES)

    def scatter_begin(grad, n, l):
        handle, zero = _exchange_begin(grad.reshape(N_DEV, -1, d), scatter=True, name=f"scatter_{n}_{l}")
        scatters[l][n] = handle
        return zero

    small = {n: [None] * depth for n in SMALL if n != "final_norm_g"}
    big = {n: [None] * depth for n in ("w_in", "w_out", "w_gate", "w_up", "w_down")}
    for l in reversed(range(depth)):
        s = saved[l]
        dgate, dup = _matmul(dxb, s["wd"], mode="nt", tm=tm, tn=tn_ff, tk=d, epilogue=_ep_swiglu_bwd,
                             extras=(s["gate"], s["up"]), out_dtypes=[BF16, BF16], name="ffn_down_bwd")
        dwd = _matmul(s["act"], dxb, mode="tn", tm=tm_ff, tn=tn_dw, tk=tk_t, epilogue=_ep_plain,
                      out_dtypes=[BF16], name="ffn_down_wgrad")[0]
        scatters[l] = {}
        started = scatter_begin(dwd, "w_down", l)
        dh2 = _matmul(dgate, s["wg_t"], mode="nn", tm=tm, tn=tn_d, tk=tk_ff, epilogue=_ep_plain,
                      out_dtypes=[F32], name="ffn_gate_bwd")[0]
        dh2 = _matmul(dup, s["wu_t"], mode="nn", tm=tm, tn=tn_d, tk=tk_ff, epilogue=_ep_add, extras=(dh2,),
                      out_dtypes=[F32], name="ffn_up_bwd")[0]
        dwg_t = _matmul(dgate, s["h2"], mode="tn", tm=tm_ff, tn=tn_dw, tk=tk_t, epilogue=_ep_plain,
                        out_dtypes=[BF16], name="ffn_gate_wgrad")[0]
        dwu_t = _matmul(dup, s["h2"], mode="tn", tm=tm_ff, tn=tn_dw, tk=tk_t, epilogue=_ep_plain,
                        out_dtypes=[BF16], name="ffn_up_wgrad")[0]
        started = started + scatter_begin(dwg_t, "w_gate", l) + scatter_begin(dwu_t, "w_up", l)
        dx1, dx1b, dg2 = _rms_bwd(dh2, s["x1"], row(ffn_norm_g[l]) + started, dx, tr=tr, name="ffn_norm_bwd")

        dmix = _matmul(dx1b, s["wo"], mode="nt", tm=tm, tn=tn_d, tk=d, epilogue=_ep_plain,
                       out_dtypes=[BF16], name="out_proj_bwd")[0]
        dwo = _matmul(s["mix"], dx1b, mode="tn", tm=_pick(d, (1024,)), tn=tn_dw, tk=tk_t, epilogue=_ep_plain,
                      out_dtypes=[BF16], name="out_proj_wgrad")[0]
        started = scatter_begin(dwo, "w_out", l)
        dq, dk_acc, dv_acc, dsink = _attn_bwd(s["qk"], s["z"], dmix, sink[l], tabs, cfg, name="attn_bwd")
        dk, dv = _kv_finish(dk_acc, dv_acc, tabs, cfg, name="attn_bwd_kv")
        da, dcg, dcw, cst = _conv_bwd(s["z"], s["conv_y"], dmix, dw_pad[l], row(conv_ln_g[l]) + started,
                                      row(conv_ln_b[l]), cfg, tb=tb, name="conv_bwd")
        duv, dws, dbs, sst = _sgu_bwd(s["z"], dmix, row(sgu_ln_g[l]), row(sgu_ln_b[l]), s["ws_b"], s["bs_c"],
                                      cfg, name="sgu_bwd")
        dz = jnp.concatenate([dq, dk, dv, da, dcg, duv], axis=1)
        dh = _matmul(dz, s["win_t"], mode="nn", tm=tm, tn=tn_d, tk=tk_in, epilogue=_ep_plain,
                     out_dtypes=[F32], name="in_proj_bwd")[0]
        dwin_t = _matmul(dz, s["h"], mode="tn", tm=tm_in, tn=tn_dw, tk=tk_t, epilogue=_ep_plain,
                         out_dtypes=[BF16], name="in_proj_wgrad")[0]
        started = scatter_begin(dwin_t, "w_in", l)
        dx, dxb, dg1 = _rms_bwd(dh, s["x0"], row(mix_norm_g[l]) + started, dx1, tr=tr, name="mix_norm_bwd")

        small["mix_norm_g"][l] = dg1[0]
        small["ffn_norm_g"][l] = dg2[0]
        small["sink"][l] = dsink[:, :Q_PER_KV, 0].reshape(-1)
        small["conv_dw_w"][l] = dcw[:CONV_KERNEL]
        small["conv_dw_b"][l] = cst[0]
        small["conv_ln_g"][l] = cst[1]
        small["conv_ln_b"][l] = cst[2]
        small["sgu_ln_g"][l] = sst[0]
        small["sgu_ln_b"][l] = sst[1]
        small["sgu_w"][l] = dws
        small["sgu_b"][l] = dbs[:, :, 0]

    for l in reversed(range(depth)):
        for n in ("w_down", "w_gate", "w_up", "w_out", "w_in"):
            recv = _exchange_wait(scatters[l][n], dx, scatter=True, name=f"scatter_{n}_{l}_wait")
            total = _sum_shards(recv, tr=_pick(recv.shape[1], (64, 32, 16)), name="sum_grads")
            big[n][l] = total.T if n in ("w_in", "w_gate", "w_up") else total

    parts = {n: jnp.stack(v) for n, v in small.items()}
    parts["final_norm_g"] = head[0]
    shapes = {n: parts[n].shape for n in SMALL}
    summed = _unpack_small(_allreduce_small(_pack_small(parts), name="allreduce_small"), shapes)

    grads = {n: jnp.stack(v) for n, v in big.items()}
    for n in SMALL:
        grads[n] = summed[n]
    cshard_w = conv_dw_w.shape[2]
    grads["conv_dw_w"] = lax.dynamic_slice_in_dim(summed["conv_dw_w"], my_id * cshard_w, cshard_w, axis=2)

    deltas, new_m, new_v = {}, {}, {}
    for n in WEIGHT_NAMES:
        shape = w[n].shape
        cols = shape[-1]
        view = lambda a: a.reshape(-1, cols)
        dl, nm, nv = _adamw(view(w[n]), view(grads[n]), view(mom_m[n]), view(mom_v[n]), name="adamw")
        deltas[n], new_m[n], new_v[n] = dl.reshape(shape), nm.reshape(shape), nv.reshape(shape)

    return (loss, dx[None], *[grads[n] for n in WEIGHT_NAMES], *[deltas[n] for n in WEIGHT_NAMES],
            *[new_m[n] for n in WEIGHT_NAMES], *[new_v[n] for n in WEIGHT_NAMES])
```

```python
import functools
import math

import jax
import jax.numpy as jnp
from jax import lax
from jax.experimental import pallas as pl
from jax.experimental.pallas import tpu as pltpu

F32 = jnp.float32
BF16 = jnp.bfloat16

HEAD_DIM = 128
Q_PER_KV = 4
WINDOW = 128
BLOCK = 128
ROT_DIM = 32
ROPE_THETA = 500000.0
CONV_KERNEL = 31
CONV_PAD = (CONV_KERNEL - 1) // 2
CHUNK = 128
EPS = 1e-6

ADAM_LR = 0.001
ADAM_B1 = 0.9
ADAM_B2 = 0.999
ADAM_EPS = 1e-08
ADAM_WD = 0.01
ADAM_STEP = 10

N_DEV = 8
MESH_AXES = ("x", "y", "c")
VMEM_LIMIT_BYTES = 56 * 1024 * 1024
HALO_ROWS = 16
ADAMW_TILE_ELEMS = 256 * 1024
MESH_ID = pl.DeviceIdType.MESH


class Cfg:
    def __init__(self, d_model, seq):
        self.d = d_model
        self.t = seq
        self.attn = d_model // 2
        self.hq = self.attn // HEAD_DIM
        self.g = self.hq // Q_PER_KV
        self.kv = self.g * HEAD_DIM
        self.conv = d_model // 4
        self.sgu = d_model // 4
        self.sh = self.sgu // HEAD_DIM
        self.off_k = self.attn
        self.off_v = self.attn + self.kv
        self.off_ca = self.attn + 2 * self.kv
        self.off_cg = self.off_ca + self.conv
        self.off_u = self.off_cg + self.conv
        self.off_sv = self.off_u + self.sgu
        self.inw = self.off_sv + self.sgu


def _pick(dim, prefs):
    for p in prefs:
        if dim % p == 0:
            return p
    return dim


def _cparams(*sem):
    return pltpu.CompilerParams(dimension_semantics=sem, vmem_limit_bytes=VMEM_LIMIT_BYTES)


def _sigmoid(v):
    return 0.5 * jnp.tanh(0.5 * v) + 0.5


_DN = {
    "nn": (((1,), (0,)), ((), ())),
    "nt": (((1,), (1,)), ((), ())),
    "tn": (((0,), (0,)), ((), ())),
}


def _dot(a, b, mode):
    return lax.dot_general(a, b, _DN[mode], preferred_element_type=F32)


def _matmul(a, b, *, mode, tm, tn, tk, epilogue, out_dtypes, extras=(), name):
    if mode == "tn":
        kdim, m = a.shape
        n = b.shape[1]
    elif mode == "nn":
        m, kdim = a.shape
        n = b.shape[1]
    else:
        m, kdim = a.shape
        n = b.shape[0]
    assert m % tm == 0 and n % tn == 0 and kdim % tk == 0, (name, m, n, kdim, tm, tn, tk)
    gm, gn, gk = m // tm, n // tn, kdim // tk
    if mode == "tn":
        a_spec = pl.BlockSpec((tk, tm), lambda i, j, k: (k, i))
    else:
        a_spec = pl.BlockSpec((tm, tk), lambda i, j, k: (i, k))
    if mode == "nt":
        b_spec = pl.BlockSpec((tn, tk), lambda i, j, k: (j, k))
    else:
        b_spec = pl.BlockSpec((tk, tn), lambda i, j, k: (k, j))
    tile = pl.BlockSpec((tm, tn), lambda i, j, k: (i, j))
    ne, no = len(extras), len(out_dtypes)

    def body(a_ref, b_ref, *rest):
        ex = rest[:ne]
        outs = rest[ne:ne + no]
        part = _dot(a_ref[...], b_ref[...], mode)

        def finish(acc):
            vals = epilogue(acc, *[e[...] for e in ex])
            for o_ref, val in zip(outs, vals):
                o_ref[...] = val.astype(o_ref.dtype)

        if gk == 1:
            finish(part)
        else:
            acc_ref = rest[ne + no]
            k = pl.program_id(2)

            @pl.when(k == 0)
            def _():
                acc_ref[...] = part

            if gk > 2:
                @pl.when((k > 0) & (k < gk - 1))
                def _():
                    acc_ref[...] += part

            @pl.when(k == gk - 1)
            def _():
                finish(acc_ref[...] + part)

    return pl.pallas_call(
        body,
        grid=(gm, gn, gk),
        in_specs=[a_spec, b_spec] + [tile] * ne,
        out_specs=[tile] * no,
        out_shape=[jax.ShapeDtypeStruct((m, n), dt) for dt in out_dtypes],
        scratch_shapes=[pltpu.VMEM((tm, tn), F32)] if gk > 1 else [],
        compiler_params=_cparams("parallel", "parallel", "arbitrary"),
        name=name,
    )(a, b, *extras)


def _ep_plain(acc):
    return (acc,)


def _ep_add(acc, r):
    return (r.astype(F32) + acc,)


def _ep_swiglu_bwd(dact, gate, up):
    gate = gate.astype(F32)
    up = up.astype(F32)
    s = _sigmoid(gate)
    silu = gate * s
    dgate = dact * up * (s * (1.0 + gate * (1.0 - s)))
    dup = dact * silu
    return dgate, dup


def _ffn_up(h, wgt, wut, *, tm, tn, name):
    m, kdim = h.shape
    n = wgt.shape[0]

    def body(h_ref, g_ref, u_ref, gate_ref, up_ref, act_ref):
        hv = h_ref[...]
        gate = _dot(hv, g_ref[...], "nt")
        up = _dot(hv, u_ref[...], "nt")
        gate_ref[...] = gate.astype(BF16)
        up_ref[...] = up.astype(BF16)
        act_ref[...] = (gate * _sigmoid(gate) * up).astype(BF16)

    tile = pl.BlockSpec((tm, tn), lambda i, j: (i, j))
    wspec = pl.BlockSpec((tn, kdim), lambda i, j: (j, 0))
    return pl.pallas_call(
        body,
        grid=(m // tm, n // tn),
        in_specs=[pl.BlockSpec((tm, kdim), lambda i, j: (i, 0)), wspec, wspec],
        out_specs=[tile] * 3,
        out_shape=[jax.ShapeDtypeStruct((m, n), BF16)] * 3,
        compiler_params=_cparams("parallel", "parallel"),
        name=name,
    )(h, wgt, wut)


def _rms_fwd(x, g, *, tr, name):
    t, d = x.shape

    def body(x_ref, g_ref, h_ref):
        xv = x_ref[...]
        r = lax.rsqrt(jnp.mean(xv * xv, axis=-1, keepdims=True) + EPS)
        h_ref[...] = (xv * r * g_ref[...]).astype(BF16)

    row = pl.BlockSpec((tr, d), lambda i: (i, 0))
    return pl.pallas_call(
        body,
        grid=(t // tr,),
        in_specs=[row, pl.BlockSpec((1, d), lambda i: (0, 0))],
        out_specs=row,
        out_shape=jax.ShapeDtypeStruct((t, d), BF16),
        compiler_params=_cparams("parallel"),
        name=name,
    )(x, g)


def _rms_bwd_math(dy, xv, g):
    r = lax.rsqrt(jnp.mean(xv * xv, axis=-1, keepdims=True) + EPS)
    xh = xv * r
    dg = jnp.sum(dy * xh, axis=0, keepdims=True)
    dyg = dy * g
    dx = r * (dyg - xh * jnp.mean(dyg * xh, axis=-1, keepdims=True))
    return dx, dg


def _rms_bwd(dh, x, g, dres, *, tr, name):
    t, d = x.shape

    def body(dh_ref, x_ref, g_ref, dres_ref, dx_ref, dxb_ref, dg_ref):
        dx, dg = _rms_bwd_math(dh_ref[...], x_ref[...], g_ref[...])
        dx = dx + dres_ref[...]
        dx_ref[...] = dx
        dxb_ref[...] = dx.astype(BF16)

        @pl.when(pl.program_id(0) == 0)
        def _():
            dg_ref[...] = jnp.zeros_like(dg_ref)

        dg_ref[0:1, :] += dg

    row = pl.BlockSpec((tr, d), lambda i: (i, 0))
    vec = pl.BlockSpec((1, d), lambda i: (0, 0))
    return pl.pallas_call(
        body,
        grid=(t // tr,),
        in_specs=[row, row, vec, row],
        out_specs=[row, row, pl.BlockSpec((8, d), lambda i: (0, 0))],
        out_shape=[jax.ShapeDtypeStruct((t, d), F32), jax.ShapeDtypeStruct((t, d), BF16),
                   jax.ShapeDtypeStruct((8, d), F32)],
        compiler_params=_cparams("arbitrary"),
        name=name,
    )(dh, x, g, dres)


def _loss_head(x, g, target, *, tr, name):
    t, d = x.shape

    def body(x_ref, g_ref, t_ref, dx_ref, dxb_ref, st_ref):
        xv = x_ref[...]
        gv = g_ref[...]
        r = lax.rsqrt(jnp.mean(xv * xv, axis=-1, keepdims=True) + EPS)
        err = xv * r * gv - t_ref[...]
        sq = jnp.sum(jnp.sum(err * err, axis=1, keepdims=True), axis=0, keepdims=True)
        dx, dg = _rms_bwd_math(err * (1.0 / d), xv, gv)
        dx_ref[...] = dx
        dxb_ref[...] = dx.astype(BF16)

        @pl.when(pl.program_id(0) == 0)
        def _():
            st_ref[...] = jnp.zeros_like(st_ref)

        st_ref[0:1, :] += dg
        st_ref[1:2, :] += jnp.broadcast_to(sq * (0.5 / d), (1, d))

    row = pl.BlockSpec((tr, d), lambda i: (i, 0))
    return pl.pallas_call(
        body,
        grid=(t // tr,),
        in_specs=[row, pl.BlockSpec((1, d), lambda i: (0, 0)), row],
        out_specs=[row, row, pl.BlockSpec((8, d), lambda i: (0, 0))],
        out_shape=[jax.ShapeDtypeStruct((t, d), F32), jax.ShapeDtypeStruct((t, d), BF16),
                   jax.ShapeDtypeStruct((8, d), F32)],
        compiler_params=_cparams("arbitrary"),
        name=name,
    )(x, g, target)


def _rope_tables(t):
    half = ROT_DIM // 2
    pos = jnp.arange(t, dtype=F32)
    inv = ROPE_THETA ** (-jnp.arange(0, ROT_DIM, 2, dtype=F32) / ROT_DIM)
    ang = pos[:, None] * inv[None, :]
    cos, sin = jnp.cos(ang), jnp.sin(ang)
    rest = HEAD_DIM - ROT_DIM
    c = jnp.concatenate([cos, cos, jnp.ones((t, rest), F32)], axis=1)
    sa = jnp.concatenate([-sin, jnp.zeros((t, HEAD_DIM - half), F32)], axis=1)
    sb = jnp.concatenate([jnp.zeros((t, half), F32), sin, jnp.zeros((t, rest), F32)], axis=1)
    return c, sa, sb


def _rope_apply(v, c, sa, sb):
    half = ROT_DIM // 2
    return v * c + pltpu.roll(v, HEAD_DIM - half, axis=1) * sa + pltpu.roll(v, half, axis=1) * sb


def _rope_apply_t(dv, c, sa, sb):
    half = ROT_DIM // 2
    return dv * c + pltpu.roll(dv * sa, half, axis=1) + pltpu.roll(dv * sb, HEAD_DIM - half, axis=1)


def _rope_fwd(z, tabs, ncol, *, tr, name):
    t = z.shape[0]

    def body(z_ref, c_ref, sa_ref, sb_ref, o_ref):
        o_ref[...] = _rope_apply(z_ref[...].astype(F32), c_ref[...], sa_ref[...], sb_ref[...]).astype(BF16)

    blk = pl.BlockSpec((tr, HEAD_DIM), lambda i, j: (i, j))
    tab = pl.BlockSpec((tr, HEAD_DIM), lambda i, j: (i, 0))
    return pl.pallas_call(
        body,
        grid=(t // tr, ncol),
        in_specs=[blk, tab, tab, tab],
        out_specs=blk,
        out_shape=jax.ShapeDtypeStruct((t, ncol * HEAD_DIM), BF16),
        compiler_params=_cparams("parallel", "parallel"),
        name=name,
    )(z, *tabs)


def _attn_specs(cfg, nb):
    kcol = cfg.hq
    vcol = cfg.off_v // HEAD_DIM
    qw = Q_PER_KV * HEAD_DIM
    q_spec = pl.BlockSpec((BLOCK, qw), lambda g, n: (n, g))

    def kv(col, shift):
        def idx(g, n):
            return (jnp.clip(n + shift, 0, nb - 1), col + g)
        return pl.BlockSpec((BLOCK, HEAD_DIM), idx)

    k_specs = [kv(kcol, s) for s in (-1, 0, 1)]
    v_specs = [kv(vcol, s) for s in (-1, 0, 1)]
    return q_spec, k_specs, v_specs


def _attn_probs(q, k, valid, sk):
    scale = 1.0 / math.sqrt(HEAD_DIM)
    s = _dot(q, k, "nt") * scale
    s = jnp.where(valid, s, jnp.finfo(F32).min)
    m = jnp.maximum(jnp.max(s, axis=1, keepdims=True), sk)
    e = jnp.exp(s - m)
    es = jnp.exp(sk - m)
    inv = 1.0 / (jnp.sum(e, axis=1, keepdims=True) + es)
    return e * inv, es * inv


def _attn_valid(n, t):
    shape = (Q_PER_KV * BLOCK, 3 * BLOCK)
    qpos = n * BLOCK + (lax.broadcasted_iota(jnp.int32, shape, 0) & (BLOCK - 1))
    kpos = (n - 1) * BLOCK + lax.broadcasted_iota(jnp.int32, shape, 1)
    return (kpos >= 0) & (kpos < t) & (jnp.abs(qpos - kpos) <= WINDOW)


def _stack_heads(ref):
    return jnp.concatenate([ref[:, r * HEAD_DIM:(r + 1) * HEAD_DIM] for r in range(Q_PER_KV)], axis=0)


def _sink_column(sink_ref, g):
    head = lax.broadcasted_iota(jnp.int32, (Q_PER_KV * BLOCK, 1), 0) // BLOCK
    col = jnp.full((Q_PER_KV * BLOCK, 1), sink_ref[g * Q_PER_KV], F32)
    for r in range(1, Q_PER_KV):
        col = jnp.where(head == r, sink_ref[g * Q_PER_KV + r], col)
    return col


def _attn_fwd(qk, z, sink, cfg, *, name):
    t = qk.shape[0]
    nb = t // BLOCK
    q_spec, k_specs, v_specs = _attn_specs(cfg, nb)

    def body(sink_ref, q_ref, kp, kc, kn, vp, vc, vn, o_ref):
        g = pl.program_id(0)
        n = pl.program_id(1)
        k = jnp.concatenate([kp[...], kc[...], kn[...]], axis=0)
        v = jnp.concatenate([vp[...], vc[...], vn[...]], axis=0)
        p, _ = _attn_probs(_stack_heads(q_ref), k, _attn_valid(n, t), _sink_column(sink_ref, g))
        o = _dot(p.astype(BF16), v, "nn")
        for r in range(Q_PER_KV):
            o_ref[:, r * HEAD_DIM:(r + 1) * HEAD_DIM] = o[r * BLOCK:(r + 1) * BLOCK].astype(BF16)

    return pl.pallas_call(
        body,
        grid=(cfg.g, nb),
        in_specs=[pl.BlockSpec(memory_space=pltpu.SMEM), q_spec] + k_specs + v_specs,
        out_specs=q_spec,
        out_shape=jax.ShapeDtypeStruct((t, cfg.attn), BF16),
        compiler_params=_cparams("parallel", "parallel"),
        name=name,
    )(sink, qk, qk, qk, qk, z, z, z)


def _attn_bwd(qk, z, dmix, sink, tabs, cfg, *, name):
    t = qk.shape[0]
    nb = t // BLOCK
    q_spec, k_specs, v_specs = _attn_specs(cfg, nb)
    tab = pl.BlockSpec((BLOCK, HEAD_DIM), lambda g, n: (n, 0))
    acc_spec = pl.BlockSpec((None, t + 2 * BLOCK, HEAD_DIM), lambda g, n: (g, 0, 0))
    scale = 1.0 / math.sqrt(HEAD_DIM)

    def body(sink_ref, q_ref, kp, kc, kn, vp, vc, vn, do_ref, c_ref, sa_ref, sb_ref,
             dq_ref, dk_ref, dv_ref, ds_ref):
        g = pl.program_id(0)
        n = pl.program_id(1)

        @pl.when(n == 0)
        def _():
            dk_ref[...] = jnp.zeros_like(dk_ref)
            dv_ref[...] = jnp.zeros_like(dv_ref)
            ds_ref[...] = jnp.zeros_like(ds_ref)

        k = jnp.concatenate([kp[...], kc[...], kn[...]], axis=0)
        v = jnp.concatenate([vp[...], vc[...], vn[...]], axis=0)
        q = _stack_heads(q_ref)
        do = _stack_heads(do_ref)
        p, ps = _attn_probs(q, k, _attn_valid(n, t), _sink_column(sink_ref, g))
        dp = _dot(do, v, "nt")
        delta = jnp.sum(p * dp, axis=1, keepdims=True)
        dsc = (p * (dp - delta) * scale).astype(BF16)
        dq = _dot(dsc, k, "nn")
        dsink = ps * delta
        for r in range(Q_PER_KV):
            rows = slice(r * BLOCK, (r + 1) * BLOCK)
            dq_ref[:, r * HEAD_DIM:(r + 1) * HEAD_DIM] = _rope_apply_t(
                dq[rows], c_ref[...], sa_ref[...], sb_ref[...]).astype(BF16)
            ds_ref[r:r + 1, :] -= jnp.broadcast_to(jnp.sum(dsink[rows], axis=0, keepdims=True), (1, HEAD_DIM))
        rows = pl.ds(pl.multiple_of(n * BLOCK, BLOCK), 3 * BLOCK)
        dk_ref[rows, :] += _dot(dsc, q, "tn")
        dv_ref[rows, :] += _dot(p.astype(BF16), do, "tn")

    acc_shape = jax.ShapeDtypeStruct((cfg.g, t + 2 * BLOCK, HEAD_DIM), F32)
    return pl.pallas_call(
        body,
        grid=(cfg.g, nb),
        in_specs=[pl.BlockSpec(memory_space=pltpu.SMEM), q_spec] + k_specs + v_specs + [q_spec, tab, tab, tab],
        out_specs=[q_spec, acc_spec, acc_spec, pl.BlockSpec((None, 8, HEAD_DIM), lambda g, n: (g, 0, 0))],
        out_shape=[jax.ShapeDtypeStruct((t, cfg.attn), BF16), acc_shape, acc_shape,
                   jax.ShapeDtypeStruct((cfg.g, 8, HEAD_DIM), F32)],
        compiler_params=_cparams("arbitrary", "arbitrary"),
        name=name,
    )(sink, qk, qk, qk, qk, z, z, z, dmix, *tabs)


def _kv_finish(dk_acc, dv_acc, tabs, cfg, *, name):
    t = dk_acc.shape[1] - 2 * BLOCK

    def body(dk_ref, dv_ref, c_ref, sa_ref, sb_ref, ok_ref, ov_ref):
        ok_ref[...] = _rope_apply_t(dk_ref[...], c_ref[...], sa_ref[...], sb_ref[...]).astype(BF16)
        ov_ref[...] = dv_ref[...].astype(BF16)

    acc = pl.BlockSpec((None, BLOCK, HEAD_DIM), lambda g, i: (g, i + 1, 0))
    tab = pl.BlockSpec((BLOCK, HEAD_DIM), lambda g, i: (i, 0))
    out = pl.BlockSpec((BLOCK, HEAD_DIM), lambda g, i: (i, g))
    return pl.pallas_call(
        body,
        grid=(cfg.g, t // BLOCK),
        in_specs=[acc, acc, tab, tab, tab],
        out_specs=[out, out],
        out_shape=[jax.ShapeDtypeStruct((t, cfg.kv), BF16)] * 2,
        compiler_params=_cparams("parallel", "parallel"),
        name=name,
    )(dk_acc, dv_acc, *tabs)


def _halo_specs(width, col, tb, t):
    per = tb // HALO_ROWS
    last = t // HALO_ROWS - 1
    prev = pl.BlockSpec((HALO_ROWS, width), lambda i: (jnp.maximum(i * per - 1, 0), col))
    cur = pl.BlockSpec((tb, width), lambda i: (i, col))
    nxt = pl.BlockSpec((HALO_ROWS, width), lambda i: (jnp.minimum((i + 1) * per, last), col))
    return [prev, cur, nxt]


def _halo_load(refs):
    return jnp.concatenate([r[...].astype(F32) for r in refs], axis=0)


def _shift_rows(v, start, rows):
    total = v.shape[0]
    return pltpu.roll(v, (total - start) % total, axis=0)[0:rows]


def _conv_glu(a_refs, g_refs, i, tb, t):
    a = _halo_load(a_refs)
    g = _halo_load(g_refs)
    rows = i * tb - HALO_ROWS + lax.broadcasted_iota(jnp.int32, (tb + 2 * HALO_ROWS, 1), 0)
    valid = (rows >= 0) & (rows < t)
    sg = _sigmoid(g)
    return a, sg, jnp.where(valid, a * sg, 0.0), valid


def _conv_fwd(z, w, b, lg, lb, cfg, *, tb, name):
    t = z.shape[0]
    cw = cfg.conv
    vec = pl.BlockSpec((1, cw), lambda i: (0, 0))

    def body(ap, ac, an, gp, gc, gn, w_ref, b_ref, lg_ref, lb_ref, o_ref, y_ref):
        _, _, c, _ = _conv_glu((ap, ac, an), (gp, gc, gn), pl.program_id(0), tb, t)
        acc = jnp.zeros((tb, cw), F32)
        for j in range(CONV_KERNEL):
            acc = acc + w_ref[j:j + 1, :] * _shift_rows(c, j + HALO_ROWS - CONV_PAD, tb)
        y = acc + b_ref[...]
        y_ref[...] = y
        mu = jnp.mean(y, axis=-1, keepdims=True)
        dlt = y - mu
        var = jnp.mean(dlt * dlt, axis=-1, keepdims=True)
        yn = dlt * lax.rsqrt(var + EPS) * lg_ref[...] + lb_ref[...]
        o_ref[...] = (yn * _sigmoid(yn)).astype(BF16)

    out = pl.BlockSpec((tb, cw), lambda i: (i, 0))
    return pl.pallas_call(
        body,
        grid=(t // tb,),
        in_specs=(_halo_specs(cw, cfg.off_ca // cw, tb, t) + _halo_specs(cw, cfg.off_cg // cw, tb, t)
                  + [pl.BlockSpec((CONV_KERNEL + 1, cw), lambda i: (0, 0)), vec, vec, vec]),
        out_specs=[out, out],
        out_shape=[jax.ShapeDtypeStruct((t, cw), BF16), jax.ShapeDtypeStruct((t, cw), F32)],
        compiler_params=_cparams("parallel"),
        name=name,
    )(z, z, z, z, z, z, w, b, lg, lb)


def _conv_bwd(z, y, dmix, w, lg, lb, cfg, *, tb, name):
    t = z.shape[0]
    cw = cfg.conv
    vec = pl.BlockSpec((1, cw), lambda i: (0, 0))
    cen = slice(HALO_ROWS, HALO_ROWS + tb)

    def body(ap, ac, an, gp, gc, gn, yp, yc, yn_, dp, dc_, dn, w_ref, lg_ref, lb_ref,
             da_ref, dg_ref, dw_ref, st_ref):
        @pl.when(pl.program_id(0) == 0)
        def _():
            dw_ref[...] = jnp.zeros_like(dw_ref)
            st_ref[...] = jnp.zeros_like(st_ref)

        a, sg, c, valid = _conv_glu((ap, ac, an), (gp, gc, gn), pl.program_id(0), tb, t)
        yv = _halo_load((yp, yc, yn_))
        do = _halo_load((dp, dc_, dn))
        mu = jnp.mean(yv, axis=-1, keepdims=True)
        dlt = yv - mu
        rstd = lax.rsqrt(jnp.mean(dlt * dlt, axis=-1, keepdims=True) + EPS)
        xh = dlt * rstd
        lgv = lg_ref[...]
        yn = xh * lgv + lb_ref[...]
        s = _sigmoid(yn)
        dyn = do * (s * (1.0 + yn * (1.0 - s)))
        dxh = dyn * lgv
        dy = rstd * (dxh - jnp.mean(dxh, axis=-1, keepdims=True)
                     - xh * jnp.mean(dxh * xh, axis=-1, keepdims=True))
        dy = jnp.where(valid, dy, 0.0)
        dyc = dy[cen]
        st_ref[0:1, :] += jnp.sum(dyc, axis=0, keepdims=True)
        st_ref[1:2, :] += jnp.sum((dyn * xh)[cen], axis=0, keepdims=True)
        st_ref[2:3, :] += jnp.sum(dyn[cen], axis=0, keepdims=True)
        dc = jnp.zeros((tb, cw), F32)
        for j in range(CONV_KERNEL):
            dc = dc + w_ref[j:j + 1, :] * _shift_rows(dy, HALO_ROWS + CONV_PAD - j, tb)
            dw_ref[j:j + 1, :] += jnp.sum(dyc * _shift_rows(c, j + HALO_ROWS - CONV_PAD, tb),
                                          axis=0, keepdims=True)
        sgc = sg[cen]
        da_ref[...] = (dc * sgc).astype(BF16)
        dg_ref[...] = (dc * a[cen] * sgc * (1.0 - sgc)).astype(BF16)

    out = pl.BlockSpec((tb, cw), lambda i: (i, 0))
    wspec = pl.BlockSpec((CONV_KERNEL + 1, cw), lambda i: (0, 0))
    return pl.pallas_call(
        body,
        grid=(t // tb,),
        in_specs=(_halo_specs(cw, cfg.off_ca // cw, tb, t) + _halo_specs(cw, cfg.off_cg // cw, tb, t)
                  + _halo_specs(cw, 0, tb, t) + _halo_specs(cw, cfg.attn // cw, tb, t) + [wspec, vec, vec]),
        out_specs=[out, out, wspec, pl.BlockSpec((8, cw), lambda i: (0, 0))],
        out_shape=[jax.ShapeDtypeStruct((t, cw), BF16), jax.ShapeDtypeStruct((t, cw), BF16),
                   jax.ShapeDtypeStruct((CONV_KERNEL + 1, cw), F32), jax.ShapeDtypeStruct((8, cw), F32)],
        compiler_params=_cparams("arbitrary"),
        name=name,
    )(z, z, z, z, z, z, y, y, y, dmix, dmix, dmix, w, lg, lb)


_SQRT_HALF = 1.0 / math.sqrt(2.0)
_INV_SQRT_2PI = 1.0 / math.sqrt(2.0 * math.pi)


def _gelu(v):
    return 0.5 * v * (1.0 + lax.erf(v * _SQRT_HALF))


def _gelu_grad(v):
    return 0.5 * (1.0 + lax.erf(v * _SQRT_HALF)) + v * jnp.exp(-0.5 * v * v) * _INV_SQRT_2PI


def _sgu_norm(zv_ref, lg_ref, lb_ref):
    xv = zv_ref[...].astype(F32)
    v = _gelu(xv)
    mu = jnp.mean(v, axis=-1, keepdims=True)
    dlt = v - mu
    rstd = lax.rsqrt(jnp.mean(dlt * dlt, axis=-1, keepdims=True) + EPS)
    xh = dlt * rstd
    return xv, xh, rstd, xh * lg_ref[...] + lb_ref[...]


def _sgu_specs(cfg):
    sw = cfg.sgu
    zu = pl.BlockSpec((CHUNK, sw), lambda i: (i, cfg.off_u // sw))
    zv = pl.BlockSpec((CHUNK, sw), lambda i: (i, cfg.off_sv // sw))
    vec = pl.BlockSpec((1, sw), lambda i: (0, 0))
    ws = pl.BlockSpec((cfg.sh, CHUNK, CHUNK), lambda i: (0, 0, 0))
    bs = pl.BlockSpec((cfg.sh, CHUNK, 1), lambda i: (0, 0, 0))
    return zu, zv, vec, ws, bs


def _sgu_fwd(z, lg, lb, ws, bs, cfg, *, name):
    t = z.shape[0]
    sw = cfg.sgu
    zu, zv, vec, wspec, bspec = _sgu_specs(cfg)

    def body(zu_ref, zv_ref, lg_ref, lb_ref, ws_ref, bs_ref, o_ref):
        u = _gelu(zu_ref[...].astype(F32))
        _, _, _, vn = _sgu_norm(zv_ref, lg_ref, lb_ref)
        vnb = vn.astype(BF16)
        for h in range(cfg.sh):
            sl = slice(h * HEAD_DIM, (h + 1) * HEAD_DIM)
            sp = _dot(ws_ref[h], vnb[:, sl], "nn") + bs_ref[h]
            o_ref[:, sl] = (u[:, sl] * sp).astype(BF16)

    return pl.pallas_call(
        body,
        grid=(t // CHUNK,),
        in_specs=[zu, zv, vec, vec, wspec, bspec],
        out_specs=pl.BlockSpec((CHUNK, sw), lambda i: (i, 0)),
        out_shape=jax.ShapeDtypeStruct((t, sw), BF16),
        compiler_params=_cparams("parallel"),
        name=name,
    )(z, z, lg, lb, ws, bs)


def _sgu_bwd(z, dmix, lg, lb, ws, bs, cfg, *, name):
    t = z.shape[0]
    sw = cfg.sgu
    zu, zv, vec, wspec, bspec = _sgu_specs(cfg)
    do_spec = pl.BlockSpec((CHUNK, sw), lambda i: (i, (cfg.attn + cfg.conv) // sw))

    def body(zu_ref, zv_ref, do_ref, lg_ref, lb_ref, ws_ref, bs_ref, duv_ref, dws_ref, dbs_ref, st_ref):
        @pl.when(pl.program_id(0) == 0)
        def _():
            dws_ref[...] = jnp.zeros_like(dws_ref)
            dbs_ref[...] = jnp.zeros_like(dbs_ref)
            st_ref[...] = jnp.zeros_like(st_ref)

        xu = zu_ref[...].astype(F32)
        u = _gelu(xu)
        xv, xh, rstd, vn = _sgu_norm(zv_ref, lg_ref, lb_ref)
        vnb = vn.astype(BF16)
        do = do_ref[...].astype(F32)
        dvn_parts = []
        for h in range(cfg.sh):
            sl = slice(h * HEAD_DIM, (h + 1) * HEAD_DIM)
            wh = ws_ref[h]
            sp = _dot(wh, vnb[:, sl], "nn") + bs_ref[h]
            dsp = do[:, sl] * u[:, sl]
            dspb = dsp.astype(BF16)
            dvn_parts.append(_dot(wh, dspb, "tn"))
            dws_ref[h] += _dot(dspb, vnb[:, sl], "nt")
            dbs_ref[h] += jnp.sum(dsp, axis=1, keepdims=True)
            duv_ref[:, sl] = (do[:, sl] * sp * _gelu_grad(xu[:, sl])).astype(BF16)
        dvn = jnp.concatenate(dvn_parts, axis=1)
        st_ref[0:1, :] += jnp.sum(dvn * xh, axis=0, keepdims=True)
        st_ref[1:2, :] += jnp.sum(dvn, axis=0, keepdims=True)
        dxh = dvn * lg_ref[...]
        dv = rstd * (dxh - jnp.mean(dxh, axis=-1, keepdims=True)
                     - xh * jnp.mean(dxh * xh, axis=-1, keepdims=True))
        duv_ref[:, sw:] = (dv * _gelu_grad(xv)).astype(BF16)

    return pl.pallas_call(
        body,
        grid=(t // CHUNK,),
        in_specs=[zu, zv, do_spec, vec, vec, wspec, bspec],
        out_specs=[pl.BlockSpec((CHUNK, 2 * sw), lambda i: (i, 0)), wspec, bspec,
                   pl.BlockSpec((8, sw), lambda i: (0, 0))],
        out_shape=[jax.ShapeDtypeStruct((t, 2 * sw), BF16), jax.ShapeDtypeStruct((cfg.sh, CHUNK, CHUNK), F32),
                   jax.ShapeDtypeStruct((cfg.sh, CHUNK, 1), F32), jax.ShapeDtypeStruct((8, sw), F32)],
        compiler_params=_cparams("arbitrary"),
        name=name,
    )(z, z, dmix, lg, lb, ws, bs)


def _sum_shards(own, landed, *, tr, name):
    _, r, c = own.shape
    my_id = _flat_id(_my_coords()).astype(jnp.int32).reshape(1)

    def body(me_ref, *refs):
        o_ref = refs[N_DEV]
        acc = refs[0][...].astype(F32)
        for p_ref in refs[1:N_DEV]:
            acc = acc + p_ref[...].astype(F32)
        o_ref[...] = acc

    def slab(p):
        return pl.BlockSpec((None, tr, c), lambda i, me: (me[0] ^ p, i, 0))

    return pl.pallas_call(
        body,
        grid_spec=pltpu.PrefetchScalarGridSpec(
            num_scalar_prefetch=1,
            grid=(r // tr,),
            in_specs=[slab(p) for p in range(N_DEV)],
            out_specs=pl.BlockSpec((tr, c), lambda i, me: (i, 0)),
        ),
        out_shape=jax.ShapeDtypeStruct((r, c), F32),
        compiler_params=_cparams("parallel"),
        name=name,
    )(my_id, own, *([landed] * (N_DEV - 1)))


def _adamw(w, g, m, v, *, name):
    r, c = w.shape
    tr = _pick(r, [p for p in (1024, 512, 256, 128, 64, 32, 16, 8) if p * c <= ADAMW_TILE_ELEMS])

    def body(w_ref, g_ref, m_ref, v_ref, d_ref, nm_ref, nv_ref):
        gv = g_ref[...]
        nm = ADAM_B1 * m_ref[...] + (1.0 - ADAM_B1) * gv
        nv = ADAM_B2 * v_ref[...] + (1.0 - ADAM_B2) * (gv * gv)
        m_hat = nm / (1.0 - ADAM_B1 ** ADAM_STEP)
        v_hat = nv / (1.0 - ADAM_B2 ** ADAM_STEP)
        d_ref[...] = -ADAM_LR * (m_hat / (jnp.sqrt(v_hat) + ADAM_EPS) + ADAM_WD * w_ref[...])
        nm_ref[...] = nm
        nv_ref[...] = nv

    blk = pl.BlockSpec((tr, c), lambda i: (i, 0))
    return pl.pallas_call(
        body,
        grid=(r // tr,),
        in_specs=[blk] * 4,
        out_specs=[blk] * 3,
        out_shape=[jax.ShapeDtypeStruct((r, c), F32)] * 3,
        compiler_params=_cparams("parallel"),
        name=name,
    )(w, g, m, v)


def _my_coords():
    return tuple(lax.axis_index(a) for a in MESH_AXES)


def _peer_coords(me, p):
    return tuple(1 - v if (p >> (2 - a)) & 1 else v for a, v in enumerate(me))


def _flat_id(coords):
    return 4 * coords[0] + 2 * coords[1] + coords[2]


def _exchange(arrs, *, scatter, name):
    na = len(arrs)

    def body(*refs):
        ins = refs[:na]
        outs = refs[na:2 * na]
        send_sems, recv_sems, local_sems = refs[2 * na:]
        me = _my_coords()
        my_id = _flat_id(me)

        local = []
        for k in range(na):
            src = ins[k].at[my_id] if scatter else ins[k]
            cp = pltpu.make_async_copy(src, outs[k].at[my_id], local_sems.at[k])
            cp.start()
            local.append(cp)

        def remote(p, k):
            peer = _peer_coords(me, p)
            peer_id = _flat_id(peer)
            sem = (p - 1) * na + k
            src = ins[k].at[peer_id] if scatter else ins[k]
            send = pltpu.make_async_remote_copy(
                src_ref=src, dst_ref=outs[k].at[my_id], send_sem=send_sems.at[sem],
                recv_sem=recv_sems.at[sem], device_id=peer, device_id_type=MESH_ID)
            recv = pltpu.make_async_remote_copy(
                src_ref=src, dst_ref=outs[k].at[peer_id], send_sem=send_sems.at[sem],
                recv_sem=recv_sems.at[sem], device_id=peer, device_id_type=MESH_ID)
            return send, recv

        pairs = [remote(p, k) for p in range(1, N_DEV) for k in range(na)]
        for send, _ in pairs:
            send.start()
        for _, recv in pairs:
            recv.wait_recv()
        for send, _ in pairs:
            send.wait_send()
        for cp in local:
            cp.wait()

    def out_of(a):
        return jax.ShapeDtypeStruct(a.shape if scatter else (N_DEV,) + a.shape, a.dtype)

    hbm = pl.BlockSpec(memory_space=pl.ANY)
    nsem = (N_DEV - 1) * na
    return pl.pallas_call(
        body,
        in_specs=[hbm] * na,
        out_specs=[hbm] * na,
        out_shape=[out_of(a) for a in arrs],
        scratch_shapes=[pltpu.SemaphoreType.DMA((nsem,)), pltpu.SemaphoreType.DMA((nsem,)),
                        pltpu.SemaphoreType.DMA((na,))],
        name=name,
    )(*arrs)


_HBM = pl.BlockSpec(memory_space=pltpu.HBM)
_SEM = pl.BlockSpec(memory_space=pltpu.SEMAPHORE)
_EFFECT = pltpu.SideEffectType.DATAFLOW_SIDE_EFFECTING


def _place_own(land, src, *, scatter, name):
    _, r, c = land.shape
    tr = _pick(r, (256, 128, 64, 32, 16))
    my_id = _flat_id(_my_coords()).astype(jnp.int32).reshape(1)

    def body(me_ref, land_ref, src_ref, out_ref):
        out_ref[...] = src_ref[...]

    if scatter:
        src_spec = pl.BlockSpec((None, tr, c), lambda i, me: (me[0], i, 0))
    else:
        src_spec = pl.BlockSpec((tr, c), lambda i, me: (i, 0))
    return pl.pallas_call(
        body,
        grid_spec=pltpu.PrefetchScalarGridSpec(
            num_scalar_prefetch=1,
            grid=(r // tr,),
            in_specs=[pl.BlockSpec(memory_space=pl.ANY), src_spec],
            out_specs=pl.BlockSpec((None, tr, c), lambda i, me: (me[0], i, 0)),
        ),
        out_shape=jax.ShapeDtypeStruct(land.shape, land.dtype),
        input_output_aliases={1: 0},
        compiler_params=_cparams("parallel"),
        name=name,
    )(my_id, land, src)


def _peer_copy(src_ref, land_ref, send_sems, recv_sems, me, p, scatter, arrival):
    peer = _peer_coords(me, p)
    peer_id = _flat_id(peer)
    return pltpu.make_async_remote_copy(
        src_ref=src_ref.at[peer_id] if scatter else src_ref,
        dst_ref=land_ref.at[peer_id if arrival else _flat_id(me)],
        send_sem=send_sems.at[p - 1], recv_sem=recv_sems.at[p - 1], device_id=peer, device_id_type=MESH_ID)


def _exchange_start(src, land, *, scatter, name):
    def body(src_ref, land_ref, send_sems, recv_sems, src_thru, land_thru, token):
        me = _my_coords()
        for p in range(1, N_DEV):
            _peer_copy(src_ref, land_ref, send_sems, recv_sems, me, p, scatter, False).start()
        token[...] = jnp.zeros_like(token)

    nsem = N_DEV - 1
    return pl.pallas_call(
        body,
        name=name,
        out_shape=(pltpu.SemaphoreType.DMA((nsem,)), pltpu.SemaphoreType.DMA((nsem,)),
                   pltpu.HBM(src.shape, src.dtype), pltpu.HBM(land.shape, land.dtype),
                   jax.ShapeDtypeStruct((8, 128), F32)),
        in_specs=(_HBM, _HBM),
        out_specs=(_SEM, _SEM, _HBM, _HBM, pl.BlockSpec(memory_space=pltpu.VMEM)),
        input_output_aliases={0: 2, 1: 3},
        compiler_params=pltpu.CompilerParams(has_side_effects=_EFFECT),
    )(pltpu.with_memory_space_constraint(src, pltpu.HBM), pltpu.with_memory_space_constraint(land, pltpu.HBM))


def _exchange_wait(handle, after, *, scatter, name):
    send_sems, recv_sems, src_thru, land_thru = handle

    def body(src_ref, land_ref, send_sems, recv_sems, after_ref, src_dead, got_ref):
        me = _my_coords()
        for p in range(1, N_DEV):
            _peer_copy(src_ref, land_ref, send_sems, recv_sems, me, p, scatter, False).wait_send()
            _peer_copy(src_ref, land_ref, send_sems, recv_sems, me, p, scatter, True).wait_recv()

    return pl.pallas_call(
        body,
        name=name,
        out_shape=(pltpu.HBM(src_thru.shape, src_thru.dtype), pltpu.HBM(land_thru.shape, land_thru.dtype)),
        in_specs=(_HBM, _HBM, _SEM, _SEM, pl.BlockSpec(memory_space=pl.ANY)),
        out_specs=(_HBM, _HBM),
        input_output_aliases={0: 0, 1: 1},
        compiler_params=pltpu.CompilerParams(has_side_effects=_EFFECT),
    )(src_thru, land_thru, send_sems, recv_sems, after)


def _exchange_begin(src, *, scatter, name):
    if scatter:
        land = lax.empty(src.shape, src.dtype)
    else:
        land = _place_own(lax.empty((N_DEV,) + src.shape, src.dtype), src, scatter=False, name=name + "_own")
    *handle, token = _exchange_start(src, land, scatter=scatter, name=name + "_start")
    return tuple(handle), token[0, 0]


_SIBLING_MASK = 1
_CHIP_MASKS = (2, 4, 6)
_DIRECT_MASKS = (_SIBLING_MASK,) + _CHIP_MASKS


def _direct_copy(src_ref, land_ref, send_sems, recv_sems, me, a, j, arrival):
    p = _DIRECT_MASKS[j]
    peer = _peer_coords(me, p)
    sem = a * len(_DIRECT_MASKS) + j
    return pltpu.make_async_remote_copy(
        src_ref=src_ref, dst_ref=land_ref.at[_flat_id(peer) if arrival else _flat_id(me)],
        send_sem=send_sems.at[sem], recv_sem=recv_sems.at[sem], device_id=peer, device_id_type=MESH_ID)


def _relay_copy(land_ref, send_sems, recv_sems, me, a, j, arrival):
    sibling = _peer_coords(me, _SIBLING_MASK)
    holder = sibling if arrival else me
    slab = land_ref.at[_flat_id(_peer_coords(holder, _CHIP_MASKS[j]))]
    sem = a * len(_CHIP_MASKS) + j
    return pltpu.make_async_remote_copy(
        src_ref=slab, dst_ref=slab, send_sem=send_sems.at[sem], recv_sem=recv_sems.at[sem],
        device_id=sibling, device_id_type=MESH_ID)


def _hbm_like(arrs):
    return tuple(pltpu.HBM(a.shape, a.dtype) for a in arrs)


def _gather_direct_start(srcs, lands, *, name):
    k = len(srcs)

    def body(*refs):
        send_sems, recv_sems = refs[2 * k:2 * k + 2]
        me = _my_coords()
        for a in range(k):
            for j in range(len(_DIRECT_MASKS)):
                _direct_copy(refs[a], refs[k + a], send_sems, recv_sems, me, a, j, False).start()
        refs[-1][...] = jnp.zeros_like(refs[-1])

    nsem = k * len(_DIRECT_MASKS)
    hbm_in = [pltpu.with_memory_space_constraint(a, pltpu.HBM) for a in (*srcs, *lands)]
    return pl.pallas_call(
        body,
        name=name,
        out_shape=(pltpu.SemaphoreType.DMA((nsem,)), pltpu.SemaphoreType.DMA((nsem,)),
                   *_hbm_like(srcs), *_hbm_like(lands), jax.ShapeDtypeStruct((8, 128), F32)),
        in_specs=(_HBM,) * (2 * k),
        out_specs=(_SEM, _SEM) + (_HBM,) * (2 * k) + (pl.BlockSpec(memory_space=pltpu.VMEM),),
        input_output_aliases={i: 2 + i for i in range(2 * k)},
        compiler_params=pltpu.CompilerParams(has_side_effects=_EFFECT),
    )(*hbm_in)


def _gather_direct_wait(send_sems, recv_sems, srcs, lands, after, *, name):
    k = len(srcs)

    def body(*refs):
        send_sems, recv_sems = refs[2 * k:2 * k + 2]
        me = _my_coords()
        for a in range(k):
            for j in range(len(_DIRECT_MASKS)):
                _direct_copy(refs[a], refs[k + a], send_sems, recv_sems, me, a, j, False).wait_send()
                _direct_copy(refs[a], refs[k + a], send_sems, recv_sems, me, a, j, True).wait_recv()

    return pl.pallas_call(
        body,
        name=name,
        out_shape=(*_hbm_like(srcs), *_hbm_like(lands)),
        in_specs=(_HBM,) * (2 * k) + (_SEM, _SEM, pl.BlockSpec(memory_space=pl.ANY)),
        out_specs=(_HBM,) * (2 * k),
        input_output_aliases={i: i for i in range(2 * k)},
        compiler_params=pltpu.CompilerParams(has_side_effects=_EFFECT),
    )(*srcs, *lands, send_sems, recv_sems, after)[k:]


def _gather_relay_start(lands, *, name):
    k = len(lands)

    def body(*refs):
        send_sems, recv_sems = refs[k:k + 2]
        me = _my_coords()
        for a in range(k):
            for j in range(len(_CHIP_MASKS)):
                _relay_copy(refs[a], send_sems, recv_sems, me, a, j, False).start()
        refs[-1][...] = jnp.zeros_like(refs[-1])

    nsem = k * len(_CHIP_MASKS)
    return pl.pallas_call(
        body,
        name=name,
        out_shape=(pltpu.SemaphoreType.DMA((nsem,)), pltpu.SemaphoreType.DMA((nsem,)),
                   *_hbm_like(lands), jax.ShapeDtypeStruct((8, 128), F32)),
        in_specs=(_HBM,) * k,
        out_specs=(_SEM, _SEM) + (_HBM,) * k + (pl.BlockSpec(memory_space=pltpu.VMEM),),
        input_output_aliases={i: 2 + i for i in range(k)},
        compiler_params=pltpu.CompilerParams(has_side_effects=_EFFECT),
    )(*lands)


def _gather_relay_wait(send_sems, recv_sems, lands, after, *, name):
    k = len(lands)

    def body(*refs):
        send_sems, recv_sems = refs[k:k + 2]
        me = _my_coords()
        for a in range(k):
            for j in range(len(_CHIP_MASKS)):
                _relay_copy(refs[a], send_sems, recv_sems, me, a, j, False).wait_send()
                _relay_copy(refs[a], send_sems, recv_sems, me, a, j, True).wait_recv()

    return pl.pallas_call(
        body,
        name=name,
        out_shape=_hbm_like(lands),
        in_specs=(_HBM,) * k + (_SEM, _SEM, pl.BlockSpec(memory_space=pl.ANY)),
        out_specs=(_HBM,) * k,
        input_output_aliases={i: i for i in range(k)},
        compiler_params=pltpu.CompilerParams(has_side_effects=_EFFECT),
    )(*lands, send_sems, recv_sems, after)


def _gather_group_begin(srcs, *, name):
    lands = [_place_own(lax.empty((N_DEV,) + s.shape, s.dtype), s, scatter=False, name=f"{name}_own{i}")
             for i, s in enumerate(srcs)]
    k = len(srcs)
    out = _gather_direct_start(srcs, lands, name=name + "_start")
    return (out[0], out[1], out[2:2 + k], out[2 + k:2 + 2 * k]), out[-1][0, 0]


def _gather_group_relay(handle, after, *, name):
    lands = _gather_direct_wait(*handle, after, name=name + "_landed")
    out = _gather_relay_start(lands, name=name + "_relay")
    return (out[0], out[1], out[2:-1]), out[-1][0, 0]


def _gather_group_end(handle, after, *, name):
    return _gather_relay_wait(*handle, after, name=name + "_done")


def _allreduce_small(flat, *, name):
    r, c = flat.shape

    def body(in_ref, out_ref, buf, send_sems, recv_sems):
        me = _my_coords()
        my_id = _flat_id(me)
        buf[my_id] = in_ref[...]

        def remote(p):
            peer = _peer_coords(me, p)
            send = pltpu.make_async_remote_copy(
                src_ref=in_ref, dst_ref=buf.at[my_id], send_sem=send_sems.at[p - 1],
                recv_sem=recv_sems.at[p - 1], device_id=peer, device_id_type=MESH_ID)
            recv = pltpu.make_async_remote_copy(
                src_ref=in_ref, dst_ref=buf.at[_flat_id(peer)], send_sem=send_sems.at[p - 1],
                recv_sem=recv_sems.at[p - 1], device_id=peer, device_id_type=MESH_ID)
            return send, recv

        pairs = [remote(p) for p in range(1, N_DEV)]
        for send, _ in pairs:
            send.start()
        for _, recv in pairs:
            recv.wait_recv()
        for send, _ in pairs:
            send.wait_send()
        acc = buf[0]
        for q in range(1, N_DEV):
            acc = acc + buf[q]
        out_ref[...] = acc

    vmem = pl.BlockSpec(memory_space=pltpu.VMEM)
    return pl.pallas_call(
        body,
        in_specs=[vmem],
        out_specs=vmem,
        out_shape=jax.ShapeDtypeStruct((r, c), F32),
        scratch_shapes=[pltpu.VMEM((N_DEV, r, c), F32), pltpu.SemaphoreType.DMA((N_DEV - 1,)),
                        pltpu.SemaphoreType.DMA((N_DEV - 1,))],
        compiler_params=pltpu.CompilerParams(vmem_limit_bytes=VMEM_LIMIT_BYTES),
        name=name,
    )(flat)


WEIGHT_NAMES = ("mix_norm_g", "w_in", "sink", "conv_dw_w", "conv_dw_b", "conv_ln_g", "conv_ln_b",
                "sgu_ln_g", "sgu_ln_b", "sgu_w", "sgu_b", "w_out", "ffn_norm_g", "w_gate", "w_up",
                "w_down", "final_norm_g")
SHARDED = ("w_in", "conv_dw_w", "w_out", "w_gate", "w_up", "w_down")
SMALL = tuple(n for n in WEIGHT_NAMES if n not in ("w_in", "w_out", "w_gate", "w_up", "w_down"))


def _pack_small(parts):
    flat = jnp.concatenate([parts[n].reshape(-1) for n in SMALL])
    pad = (-flat.shape[0]) % (8 * 128)
    return jnp.pad(flat, (0, pad)).reshape(-1, 128)


def _unpack_small(packed, shapes):
    flat = packed.reshape(-1)
    out, pos = {}, 0
    for n in SMALL:
        size = math.prod(shapes[n])
        out[n] = flat[pos:pos + size].reshape(shapes[n])
        pos += size
    return out


def kernel(x, mix_norm_g, w_in, sink, conv_dw_w, conv_dw_b, conv_ln_g, conv_ln_b, sgu_ln_g, sgu_ln_b, sgu_w, sgu_b, w_out, ffn_norm_g, w_gate, w_up, w_down, final_norm_g, loss_target, m_mix_norm_g, m_w_in, m_sink, m_conv_dw_w, m_conv_dw_b, m_conv_ln_g, m_conv_ln_b, m_sgu_ln_g, m_sgu_ln_b, m_sgu_w, m_sgu_b, m_w_out, m_ffn_norm_g, m_w_gate, m_w_up, m_w_down, m_final_norm_g, v_mix_norm_g, v_w_in, v_sink, v_conv_dw_w, v_conv_dw_b, v_conv_ln_g, v_conv_ln_b, v_sgu_ln_g, v_sgu_ln_b, v_sgu_w, v_sgu_b, v_w_out, v_ffn_norm_g, v_w_gate, v_w_up, v_w_down, v_final_norm_g):
    w = dict(mix_norm_g=mix_norm_g, w_in=w_in, sink=sink, conv_dw_w=conv_dw_w, conv_dw_b=conv_dw_b,
             conv_ln_g=conv_ln_g, conv_ln_b=conv_ln_b, sgu_ln_g=sgu_ln_g, sgu_ln_b=sgu_ln_b, sgu_w=sgu_w,
             sgu_b=sgu_b, w_out=w_out, ffn_norm_g=ffn_norm_g, w_gate=w_gate, w_up=w_up, w_down=w_down,
             final_norm_g=final_norm_g)
    mom_m = dict(zip(WEIGHT_NAMES, (m_mix_norm_g, m_w_in, m_sink, m_conv_dw_w, m_conv_dw_b, m_conv_ln_g,
                                    m_conv_ln_b, m_sgu_ln_g, m_sgu_ln_b, m_sgu_w, m_sgu_b, m_w_out,
                                    m_ffn_norm_g, m_w_gate, m_w_up, m_w_down, m_final_norm_g)))
    mom_v = dict(zip(WEIGHT_NAMES, (v_mix_norm_g, v_w_in, v_sink, v_conv_dw_w, v_conv_dw_b, v_conv_ln_g,
                                    v_conv_ln_b, v_sgu_ln_g, v_sgu_ln_b, v_sgu_w, v_sgu_b, v_w_out,
                                    v_ffn_norm_g, v_w_gate, v_w_up, v_w_down, v_final_norm_g)))

    _, t, d = x.shape
    depth = w_in.shape[0]
    cfg = Cfg(d, t)
    ff = w_gate.shape[2] * N_DEV
    my_id = _flat_id(_my_coords())
    xs = x[0]
    target = loss_target[0]

    tm = _pick(t, (1024, 512))
    tr = _pick(t, (256, 128))
    tb = _pick(t, (256, 128))
    tn_in = _pick(cfg.inw, (896, 512, 448))
    tn_ff = _pick(ff, (512, 1408, 704))
    tk_ff = ff
    tk_in = cfg.inw
    tn_d = _pick(d, (512,))
    tk_t = _pick(t, (2048, 1024, 512))
    tm_in = _pick(cfg.inw, (896, 448))
    tm_ff = _pick(ff, (1408, 704))
    tn_dw = _pick(d, (1024,))

    tabs = _rope_tables(t)

    cflat = conv_dw_w.reshape(-1)
    cshard = jnp.pad(cflat, (0, (-cflat.shape[0]) % (8 * 128))).reshape(-1, 128)
    dw_all = _exchange([cshard], scatter=False, name="gather_conv_w")[0]
    dw_all = dw_all.reshape(N_DEV, -1)[:, :cflat.shape[0]].reshape(N_DEV, depth, CONV_KERNEL, -1)
    dw_all = dw_all.transpose(1, 2, 0, 3).reshape(depth, CONV_KERNEL, cfg.conv)
    dw_pad = jnp.pad(dw_all, ((0, 0), (0, 1), (0, 0)))

    def row(v):
        return v.reshape(1, -1)

    first, rest = ("win",), ("wo", "wg", "wu", "wd")

    def gather_begin(l, names, zero):
        shards = dict(win=w_in[l].T, wo=w_out[l], wg=w_gate[l].T, wu=w_up[l].T, wd=w_down[l])
        handle, started = _gather_group_begin([(shards[k] + zero).astype(BF16) for k in names],
                                              name=f"gather_{names[0]}_{l}")
        return handle, zero + started

    def gather_end(handle, names, after, l):
        full = _gather_group_end(handle, after, name=f"gather_{names[0]}_{l}")
        return {k: f.reshape(-1, d) for k, f in zip(names, full)}

    saved = []
    scatters = [None] * depth
    dw_pad, started = lax.optimization_barrier((dw_pad, jnp.zeros((), F32)))
    landing_first, started = gather_begin(0, first, started)
    landing_rest, started = gather_begin(0, rest, started)
    relayed = None
    for l in range(depth):
        h = _rms_fwd(xs, row(mix_norm_g[l]) + started, tr=tr, name="mix_norm")
        if l == 0:
            relayed, _ = _gather_group_relay(landing_first, h, name="gather_win_0")
            wts = gather_end(relayed, first, h, 0)
        else:
            wts = gather_end(relayed, first + rest, h, l)
        win_t = wts["win"]
        z = _matmul(h, win_t, mode="nt", tm=tm, tn=tn_in, tk=d, epilogue=_ep_plain,
                    out_dtypes=[BF16], name="in_proj")[0]
        started = jnp.zeros((), F32)
        if l + 1 < depth:
            z, started = lax.optimization_barrier((z, started))
            landing_next, started = gather_begin(l + 1, first + rest, started)
        qk = _rope_fwd(z, tabs, cfg.hq + cfg.g, tr=tm, name="rope")
        attn = _attn_fwd(qk, z, sink[l], cfg, name="attn_fwd")
        conv, conv_y = _conv_fwd(z, dw_pad[l], row(conv_dw_b[l]), row(conv_ln_g[l]) + started,
                                 row(conv_ln_b[l]), cfg, tb=tb, name="conv_fwd")
        ws_b = sgu_w[l].astype(BF16)
        bs_c = sgu_b[l][:, :, None]
        sgu = _sgu_fwd(z, row(sgu_ln_g[l]), row(sgu_ln_b[l]), ws_b, bs_c, cfg, name="sgu_fwd")
        mix = jnp.concatenate([attn, conv, sgu], axis=1)
        if l == 0:
            relayed, _ = _gather_group_relay(landing_rest, mix, name="gather_wo_0")
            wts.update(gather_end(relayed, rest, mix, 0))
        wo, wg_t, wu_t, wd = wts["wo"], wts["wg"], wts["wu"], wts["wd"]
        x1 = _matmul(mix, wo, mode="nn", tm=tm, tn=tn_d, tk=d, epilogue=_ep_add, extras=(xs,),
                     out_dtypes=[F32], name="out_proj")[0]
        h2 = _rms_fwd(x1, row(ffn_norm_g[l]), tr=tr, name="ffn_norm")
        gate, up, act = _ffn_up(h2, wg_t, wu_t, tm=tm, tn=tn_ff, name="ffn_up")
        if l + 1 < depth:
            relayed, zero = _gather_group_relay(landing_next, act, name=f"gather_win_{l + 1}")
            act, zero = lax.optimization_barrier((act, zero))
            started = started + zero
        x2 = _matmul(act, wd, mode="nn", tm=tm, tn=tn_d, tk=tk_ff, epilogue=_ep_add, extras=(x1,),
                     out_dtypes=[F32], name="ffn_down")[0]
        saved.append(dict(x0=xs, h=h, z=z, qk=qk, mix=mix, conv_y=conv_y, x1=x1, h2=h2, gate=gate, up=up,
                          act=act, win_t=win_t, wg_t=wg_t, wu_t=wu_t, wo=wo, wd=wd, ws_b=ws_b, bs_c=bs_c))
        xs = x2

    dx, dxb, head = _loss_head(xs, row(final_norm_g), target, tr=tr, name="loss_head")
    loss = lax.psum(head[1, 0], MESH_AXES)

    def scatter_begin(grad, n, l):
        handle, zero = _exchange_begin(grad.reshape(N_DEV, -1, d), scatter=True, name=f"scatter_{n}_{l}")
        scatters[l][n] = handle
        return zero

    small = {n: [None] * depth for n in SMALL if n != "final_norm_g"}
    big = {n: [None] * depth for n in ("w_in", "w_out", "w_gate", "w_up", "w_down")}
    for l in reversed(range(depth)):
        s = saved[l]
        dgate, dup = _matmul(dxb, s["wd"], mode="nt", tm=tm, tn=tn_ff, tk=d, epilogue=_ep_swiglu_bwd,
                             extras=(s["gate"], s["up"]), out_dtypes=[BF16, BF16], name="ffn_down_bwd")
        dwd = _matmul(s["act"], dxb, mode="tn", tm=tm_ff, tn=tn_dw, tk=tk_t, epilogue=_ep_plain,
                      out_dtypes=[BF16], name="ffn_down_wgrad")[0]
        scatters[l] = {}
        started = scatter_begin(dwd, "w_down", l)
        dh2 = _matmul(dgate, s["wg_t"], mode="nn", tm=tm, tn=tn_d, tk=tk_ff, epilogue=_ep_plain,
                      out_dtypes=[F32], name="ffn_gate_bwd")[0]
        dh2 = _matmul(dup, s["wu_t"], mode="nn", tm=tm, tn=tn_d, tk=tk_ff, epilogue=_ep_add, extras=(dh2,),
                      out_dtypes=[F32], name="ffn_up_bwd")[0]
        dwg_t = _matmul(dgate, s["h2"], mode="tn", tm=tm_ff, tn=tn_dw, tk=tk_t, epilogue=_ep_plain,
                        out_dtypes=[BF16], name="ffn_gate_wgrad")[0]
        dwu_t = _matmul(dup, s["h2"], mode="tn", tm=tm_ff, tn=tn_dw, tk=tk_t, epilogue=_ep_plain,
                        out_dtypes=[BF16], name="ffn_up_wgrad")[0]
        started = started + scatter_begin(dwg_t, "w_gate", l) + scatter_begin(dwu_t, "w_up", l)
        dx1, dx1b, dg2 = _rms_bwd(dh2, s["x1"], row(ffn_norm_g[l]) + started, dx, tr=tr, name="ffn_norm_bwd")

        dmix = _matmul(dx1b, s["wo"], mode="nt", tm=tm, tn=tn_d, tk=d, epilogue=_ep_plain,
                       out_dtypes=[BF16], name="out_proj_bwd")[0]
        dwo = _matmul(s["mix"], dx1b, mode="tn", tm=_pick(d, (1024,)), tn=tn_dw, tk=tk_t, epilogue=_ep_plain,
                      out_dtypes=[BF16], name="out_proj_wgrad")[0]
        started = scatter_begin(dwo, "w_out", l)
        dq, dk_acc, dv_acc, dsink = _attn_bwd(s["qk"], s["z"], dmix, sink[l], tabs, cfg, name="attn_bwd")
        dk, dv = _kv_finish(dk_acc, dv_acc, tabs, cfg, name="attn_bwd_kv")
        da, dcg, dcw, cst = _conv_bwd(s["z"], s["conv_y"], dmix, dw_pad[l], row(conv_ln_g[l]) + started,
                                      row(conv_ln_b[l]), cfg, tb=tb, name="conv_bwd")
        duv, dws, dbs, sst = _sgu_bwd(s["z"], dmix, row(sgu_ln_g[l]), row(sgu_ln_b[l]), s["ws_b"], s["bs_c"],
                                      cfg, name="sgu_bwd")
        dz = jnp.concatenate([dq, dk, dv, da, dcg, duv], axis=1)
        dh = _matmul(dz, s["win_t"], mode="nn", tm=tm, tn=tn_d, tk=tk_in, epilogue=_ep_plain,
                     out_dtypes=[F32], name="in_proj_bwd")[0]
        dwin_t = _matmul(dz, s["h"], mode="tn", tm=tm_in, tn=tn_dw, tk=tk_t, epilogue=_ep_plain,
                         out_dtypes=[BF16], name="in_proj_wgrad")[0]
        started = scatter_begin(dwin_t, "w_in", l)
        dx, dxb, dg1 = _rms_bwd(dh, s["x0"], row(mix_norm_g[l]) + started, dx1, tr=tr, name="mix_norm_bwd")

        small["mix_norm_g"][l] = dg1[0]
        small["ffn_norm_g"][l] = dg2[0]
        small["sink"][l] = dsink[:, :Q_PER_KV, 0].reshape(-1)
        small["conv_dw_w"][l] = dcw[:CONV_KERNEL]
        small["conv_dw_b"][l] = cst[0]
        small["conv_ln_g"][l] = cst[1]
        small["conv_ln_b"][l] = cst[2]
        small["sgu_ln_g"][l] = sst[0]
        small["sgu_ln_b"][l] = sst[1]
        small["sgu_w"][l] = dws
        small["sgu_b"][l] = dbs[:, :, 0]

    grads, deltas, new_m, new_v = {}, {}, {}, {}

    def adamw(n):
        shape = w[n].shape
        view = lambda a: a.reshape(-1, shape[-1])
        dl, nm, nv = _adamw(view(w[n]), view(grads[n]), view(mom_m[n]), view(mom_v[n]), name="adamw")
        deltas[n], new_m[n], new_v[n] = dl.reshape(shape), nm.reshape(shape), nv.reshape(shape)

    after = dx
    for n in ("w_down", "w_gate", "w_up", "w_out", "w_in"):
        for l in reversed(range(depth)):
            own, landed = _exchange_wait(scatters[l][n], after, scatter=True, name=f"scatter_{n}_{l}_wait")
            total = _sum_shards(own, landed, tr=_pick(own.shape[1], (64, 32, 16)), name="sum_grads")
            big[n][l] = total.T if n in ("w_in", "w_gate", "w_up") else total
        grads[n] = jnp.stack(big[n])
        adamw(n)
        after = new_v[n]

    parts = {n: jnp.stack(v) for n, v in small.items()}
    parts["final_norm_g"] = head[0]
    shapes = {n: parts[n].shape for n in SMALL}
    summed = _unpack_small(_allreduce_small(_pack_small(parts), name="allreduce_small"), shapes)
    for n in SMALL:
        grads[n] = summed[n]
    cshard_w = conv_dw_w.shape[2]
    grads["conv_dw_w"] = lax.dynamic_slice_in_dim(summed["conv_dw_w"], my_id * cshard_w, cshard_w, axis=2)
    for n in SMALL:
        adamw(n)

    return (loss, dx[None], *[grads[n] for n in WEIGHT_NAMES], *[deltas[n] for n in WEIGHT_NAMES],
            *[new_m[n] for n in WEIGHT_NAMES], *[new_v[n] for n in WEIGHT_NAMES])
```

```python
import functools
import math

import jax
import jax.numpy as jnp
from jax import lax
from jax.experimental import pallas as pl
from jax.experimental.pallas import tpu as pltpu

F32 = jnp.float32
BF16 = jnp.bfloat16

HEAD_DIM = 128
Q_PER_KV = 4
WINDOW = 128
BLOCK = 128
ROT_DIM = 32
ROPE_THETA = 500000.0
CONV_KERNEL = 31
CONV_PAD = (CONV_KERNEL - 1) // 2
CHUNK = 128
EPS = 1e-6

ADAM_LR = 0.001
ADAM_B1 = 0.9
ADAM_B2 = 0.999
ADAM_EPS = 1e-08
ADAM_WD = 0.01
ADAM_STEP = 10

N_DEV = 8
MESH_AXES = ("x", "y", "c")
VMEM_LIMIT_BYTES = 56 * 1024 * 1024
HALO_ROWS = 16
KV_PAD = 512
ADAMW_TILE_ELEMS = 256 * 1024
MESH_ID = pl.DeviceIdType.MESH


class Cfg:
    def __init__(self, d_model, seq):
        self.d = d_model
        self.t = seq
        self.attn = d_model // 2
        self.hq = self.attn // HEAD_DIM
        self.g = self.hq // Q_PER_KV
        self.kv = self.g * HEAD_DIM
        self.conv = d_model // 4
        self.sgu = d_model // 4
        self.sh = self.sgu // HEAD_DIM
        self.off_k = self.attn
        self.off_v = self.attn + self.kv
        self.off_ca = self.attn + 2 * self.kv
        self.off_cg = self.off_ca + self.conv
        self.off_u = self.off_cg + self.conv
        self.off_sv = self.off_u + self.sgu
        self.inw = self.off_sv + self.sgu


def _pick(dim, prefs):
    for p in prefs:
        if dim % p == 0:
            return p
    return dim


def _cparams(*sem):
    return pltpu.CompilerParams(dimension_semantics=sem, vmem_limit_bytes=VMEM_LIMIT_BYTES)


def _sigmoid(v):
    return 0.5 * jnp.tanh(0.5 * v) + 0.5


_DN = {
    "nn": (((1,), (0,)), ((), ())),
    "nt": (((1,), (1,)), ((), ())),
    "tn": (((0,), (0,)), ((), ())),
}


def _dot(a, b, mode):
    return lax.dot_general(a, b, _DN[mode], preferred_element_type=F32)


def _matmul(a, b, *, mode, tm, tn, tk, epilogue, out_dtypes, extras=(), name):
    if mode == "tn":
        kdim, m = a.shape
        n = b.shape[1]
    elif mode == "nn":
        m, kdim = a.shape
        n = b.shape[1]
    else:
        m, kdim = a.shape
        n = b.shape[0]
    assert m % tm == 0 and n % tn == 0 and kdim % tk == 0, (name, m, n, kdim, tm, tn, tk)
    gm, gn, gk = m // tm, n // tn, kdim // tk
    if mode == "tn":
        a_spec = pl.BlockSpec((tk, tm), lambda i, j, k: (k, i))
    else:
        a_spec = pl.BlockSpec((tm, tk), lambda i, j, k: (i, k))
    if mode == "nt":
        b_spec = pl.BlockSpec((tn, tk), lambda i, j, k: (j, k))
    else:
        b_spec = pl.BlockSpec((tk, tn), lambda i, j, k: (k, j))
    tile = pl.BlockSpec((tm, tn), lambda i, j, k: (i, j))
    ne, no = len(extras), len(out_dtypes)

    def body(a_ref, b_ref, *rest):
        ex = rest[:ne]
        outs = rest[ne:ne + no]
        part = _dot(a_ref[...], b_ref[...], mode)

        def finish(acc):
            vals = epilogue(acc, *[e[...] for e in ex])
            for o_ref, val in zip(outs, vals):
                o_ref[...] = val.astype(o_ref.dtype)

        if gk == 1:
            finish(part)
        else:
            acc_ref = rest[ne + no]
            k = pl.program_id(2)

            @pl.when(k == 0)
            def _():
                acc_ref[...] = part

            if gk > 2:
                @pl.when((k > 0) & (k < gk - 1))
                def _():
                    acc_ref[...] += part

            @pl.when(k == gk - 1)
            def _():
                finish(acc_ref[...] + part)

    return pl.pallas_call(
        body,
        grid=(gm, gn, gk),
        in_specs=[a_spec, b_spec] + [tile] * ne,
        out_specs=[tile] * no,
        out_shape=[jax.ShapeDtypeStruct((m, n), dt) for dt in out_dtypes],
        scratch_shapes=[pltpu.VMEM((tm, tn), F32)] if gk > 1 else [],
        compiler_params=_cparams("parallel", "parallel", "arbitrary"),
        name=name,
    )(a, b, *extras)


def _ep_plain(acc):
    return (acc,)


def _ep_add(acc, r):
    return (r.astype(F32) + acc,)


def _ep_swiglu_bwd(dact, gate, up):
    gate = gate.astype(F32)
    up = up.astype(F32)
    s = _sigmoid(gate)
    silu = gate * s
    dgate = dact * up * (s * (1.0 + gate * (1.0 - s)))
    dup = dact * silu
    return dgate, dup


def _ffn_up(h, wgt, wut, *, tm, tn, name):
    m, kdim = h.shape
    n = wgt.shape[0]

    def body(h_ref, g_ref, u_ref, gate_ref, up_ref, act_ref):
        hv = h_ref[...]
        gate = _dot(hv, g_ref[...], "nt")
        up = _dot(hv, u_ref[...], "nt")
        gate_ref[...] = gate.astype(BF16)
        up_ref[...] = up.astype(BF16)
        act_ref[...] = (gate * _sigmoid(gate) * up).astype(BF16)

    tile = pl.BlockSpec((tm, tn), lambda i, j: (i, j))
    wspec = pl.BlockSpec((tn, kdim), lambda i, j: (j, 0))
    return pl.pallas_call(
        body,
        grid=(m // tm, n // tn),
        in_specs=[pl.BlockSpec((tm, kdim), lambda i, j: (i, 0)), wspec, wspec],
        out_specs=[tile] * 3,
        out_shape=[jax.ShapeDtypeStruct((m, n), BF16)] * 3,
        compiler_params=_cparams("parallel", "parallel"),
        name=name,
    )(h, wgt, wut)


def _rms_fwd(x, g, *, tr, name):
    t, d = x.shape

    def body(x_ref, g_ref, h_ref):
        xv = x_ref[...]
        r = lax.rsqrt(jnp.mean(xv * xv, axis=-1, keepdims=True) + EPS)
        h_ref[...] = (xv * r * g_ref[...]).astype(BF16)

    row = pl.BlockSpec((tr, d), lambda i: (i, 0))
    return pl.pallas_call(
        body,
        grid=(t // tr,),
        in_specs=[row, pl.BlockSpec((1, d), lambda i: (0, 0))],
        out_specs=row,
        out_shape=jax.ShapeDtypeStruct((t, d), BF16),
        compiler_params=_cparams("parallel"),
        name=name,
    )(x, g)


def _rms_bwd_math(dy, xv, g):
    r = lax.rsqrt(jnp.mean(xv * xv, axis=-1, keepdims=True) + EPS)
    xh = xv * r
    dg = jnp.sum(dy * xh, axis=0, keepdims=True)
    dyg = dy * g
    dx = r * (dyg - xh * jnp.mean(dyg * xh, axis=-1, keepdims=True))
    return dx, dg


def _rms_bwd(dh, x, g, dres, *, tr, name):
    t, d = x.shape

    def body(dh_ref, x_ref, g_ref, dres_ref, dx_ref, dxb_ref, dg_ref):
        dx, dg = _rms_bwd_math(dh_ref[...].astype(F32), x_ref[...], g_ref[...])
        dx = dx + dres_ref[...]
        dx_ref[...] = dx
        dxb_ref[...] = dx.astype(BF16)

        @pl.when(pl.program_id(0) == 0)
        def _():
            dg_ref[...] = jnp.zeros_like(dg_ref)

        dg_ref[0:1, :] += dg

    row = pl.BlockSpec((tr, d), lambda i: (i, 0))
    vec = pl.BlockSpec((1, d), lambda i: (0, 0))
    return pl.pallas_call(
        body,
        grid=(t // tr,),
        in_specs=[row, row, vec, row],
        out_specs=[row, row, pl.BlockSpec((8, d), lambda i: (0, 0))],
        out_shape=[jax.ShapeDtypeStruct((t, d), F32), jax.ShapeDtypeStruct((t, d), BF16),
                   jax.ShapeDtypeStruct((8, d), F32)],
        compiler_params=_cparams("arbitrary"),
        name=name,
    )(dh, x, g, dres)


def _loss_head(x, g, target, *, tr, name):
    t, d = x.shape

    def body(x_ref, g_ref, t_ref, dx_ref, dxb_ref, st_ref):
        xv = x_ref[...]
        gv = g_ref[...]
        r = lax.rsqrt(jnp.mean(xv * xv, axis=-1, keepdims=True) + EPS)
        err = xv * r * gv - t_ref[...]
        sq = jnp.sum(jnp.sum(err * err, axis=1, keepdims=True), axis=0, keepdims=True)
        dx, dg = _rms_bwd_math(err * (1.0 / d), xv, gv)
        dx_ref[...] = dx
        dxb_ref[...] = dx.astype(BF16)

        @pl.when(pl.program_id(0) == 0)
        def _():
            st_ref[...] = jnp.zeros_like(st_ref)

        st_ref[0:1, :] += dg
        st_ref[1:2, :] += jnp.broadcast_to(sq * (0.5 / d), (1, d))

    row = pl.BlockSpec((tr, d), lambda i: (i, 0))
    return pl.pallas_call(
        body,
        grid=(t // tr,),
        in_specs=[row, pl.BlockSpec((1, d), lambda i: (0, 0)), row],
        out_specs=[row, row, pl.BlockSpec((8, d), lambda i: (0, 0))],
        out_shape=[jax.ShapeDtypeStruct((t, d), F32), jax.ShapeDtypeStruct((t, d), BF16),
                   jax.ShapeDtypeStruct((8, d), F32)],
        compiler_params=_cparams("arbitrary"),
        name=name,
    )(x, g, target)


def _rope_tables(t):
    half = ROT_DIM // 2
    pos = jnp.arange(t, dtype=F32)
    inv = ROPE_THETA ** (-jnp.arange(0, ROT_DIM, 2, dtype=F32) / ROT_DIM)
    ang = pos[:, None] * inv[None, :]
    cos, sin = jnp.cos(ang), jnp.sin(ang)
    rest = HEAD_DIM - ROT_DIM
    c = jnp.concatenate([cos, cos, jnp.ones((t, rest), F32)], axis=1)
    sa = jnp.concatenate([-sin, jnp.zeros((t, HEAD_DIM - half), F32)], axis=1)
    sb = jnp.concatenate([jnp.zeros((t, half), F32), sin, jnp.zeros((t, rest), F32)], axis=1)
    return c, sa, sb


def _rope_apply(v, c, sa, sb):
    half = ROT_DIM // 2
    return v * c + pltpu.roll(v, HEAD_DIM - half, axis=1) * sa + pltpu.roll(v, half, axis=1) * sb


def _rope_apply_t(dv, c, sa, sb):
    half = ROT_DIM // 2
    return dv * c + pltpu.roll(dv * sa, half, axis=1) + pltpu.roll(dv * sb, HEAD_DIM - half, axis=1)


def _in_proj(h, win_t, tabs, cfg, *, tm, tn, name):
    m, kdim = h.shape
    n = win_t.shape[0]
    per_tile = tn // HEAD_DIM
    n_rot = cfg.hq + cfg.g
    rot_tiles = -(-n_rot // per_tile)

    def body(h_ref, w_ref, c_ref, sa_ref, sb_ref, z_ref):
        j = pl.program_id(1)
        acc = _dot(h_ref[...], w_ref[...], "nt")

        @pl.when(j < rot_tiles)
        def _():
            for b in range(per_tile):
                sl = slice(b * HEAD_DIM, (b + 1) * HEAD_DIM)
                rot = _rope_apply(acc[:, sl], c_ref[...], sa_ref[...], sb_ref[...])
                z_ref[:, sl] = jnp.where(j * per_tile + b < n_rot, rot, acc[:, sl]).astype(BF16)

        @pl.when(j >= rot_tiles)
        def _():
            z_ref[...] = acc.astype(BF16)

    tab = pl.BlockSpec((tm, HEAD_DIM), lambda i, j: (i, 0))
    return pl.pallas_call(
        body,
        grid=(m // tm, n // tn),
        in_specs=[pl.BlockSpec((tm, kdim), lambda i, j: (i, 0)), pl.BlockSpec((tn, kdim), lambda i, j: (j, 0)),
                  tab, tab, tab],
        out_specs=pl.BlockSpec((tm, tn), lambda i, j: (i, j)),
        out_shape=jax.ShapeDtypeStruct((m, n), BF16),
        compiler_params=_cparams("parallel", "parallel"),
        name=name,
    )(h, win_t, *tabs)


def _attn_specs(cfg, nb):
    kcol = cfg.hq
    vcol = cfg.off_v // HEAD_DIM
    qw = Q_PER_KV * HEAD_DIM
    q_spec = pl.BlockSpec((BLOCK, qw), lambda g, n: (n, g))

    def kv(col, shift):
        def idx(g, n):
            return (jnp.clip(n + shift, 0, nb - 1), col + g)
        return pl.BlockSpec((BLOCK, HEAD_DIM), idx)

    k_specs = [kv(kcol, s) for s in (-1, 0, 1)]
    v_specs = [kv(vcol, s) for s in (-1, 0, 1)]
    return q_spec, k_specs, v_specs


def _attn_probs(q, k, valid, sk):
    scale = 1.0 / math.sqrt(HEAD_DIM)
    s = _dot(q, k, "nt") * scale
    s = jnp.where(valid, s, jnp.finfo(F32).min)
    m = jnp.maximum(jnp.max(s, axis=1, keepdims=True), sk)
    e = jnp.exp(s - m)
    es = jnp.exp(sk - m)
    inv = 1.0 / (jnp.sum(e, axis=1, keepdims=True) + es)
    return e * inv, es * inv


def _attn_valid(n, t):
    shape = (Q_PER_KV * BLOCK, 3 * BLOCK)
    qpos = n * BLOCK + (lax.broadcasted_iota(jnp.int32, shape, 0) & (BLOCK - 1))
    kpos = (n - 1) * BLOCK + lax.broadcasted_iota(jnp.int32, shape, 1)
    return (kpos >= 0) & (kpos < t) & (jnp.abs(qpos - kpos) <= WINDOW)


def _stack_heads(ref):
    return jnp.concatenate([ref[:, r * HEAD_DIM:(r + 1) * HEAD_DIM] for r in range(Q_PER_KV)], axis=0)


def _sink_column(sink_ref, g):
    head = lax.broadcasted_iota(jnp.int32, (Q_PER_KV * BLOCK, 1), 0) // BLOCK
    col = jnp.full((Q_PER_KV * BLOCK, 1), sink_ref[g * Q_PER_KV], F32)
    for r in range(1, Q_PER_KV):
        col = jnp.where(head == r, sink_ref[g * Q_PER_KV + r], col)
    return col


def _attn_fwd(qk, z, sink, cfg, *, name):
    t = qk.shape[0]
    nb = t // BLOCK
    q_spec, k_specs, v_specs = _attn_specs(cfg, nb)

    def body(sink_ref, q_ref, kp, kc, kn, vp, vc, vn, o_ref):
        g = pl.program_id(0)
        n = pl.program_id(1)
        k = jnp.concatenate([kp[...], kc[...], kn[...]], axis=0)
        v = jnp.concatenate([vp[...], vc[...], vn[...]], axis=0)
        p, _ = _attn_probs(_stack_heads(q_ref), k, _attn_valid(n, t), _sink_column(sink_ref, g))
        o = _dot(p.astype(BF16), v, "nn")
        for r in range(Q_PER_KV):
            o_ref[:, r * HEAD_DIM:(r + 1) * HEAD_DIM] = o[r * BLOCK:(r + 1) * BLOCK].astype(BF16)

    return pl.pallas_call(
        body,
        grid=(cfg.g, nb),
        in_specs=[pl.BlockSpec(memory_space=pltpu.SMEM), q_spec] + k_specs + v_specs,
        out_specs=q_spec,
        out_shape=jax.ShapeDtypeStruct((t, cfg.d), BF16),
        compiler_params=_cparams("parallel", "parallel"),
        name=name,
    )(sink, qk, qk, qk, qk, z, z, z)


def _attn_bwd(qk, z, dmix, sink, tabs, cfg, *, name):
    t = qk.shape[0]
    nb = t // BLOCK
    q_spec, k_specs, v_specs = _attn_specs(cfg, nb)
    tab = pl.BlockSpec((BLOCK, HEAD_DIM), lambda g, n: (n, 0))
    acc_spec = pl.BlockSpec((None, t + 2 * KV_PAD, HEAD_DIM), lambda g, n: (g, 0, 0))
    scale = 1.0 / math.sqrt(HEAD_DIM)

    def body(sink_ref, q_ref, kp, kc, kn, vp, vc, vn, do_ref, c_ref, sa_ref, sb_ref,
             dq_ref, dk_ref, dv_ref, ds_ref):
        g = pl.program_id(0)
        n = pl.program_id(1)

        @pl.when(n == 0)
        def _():
            dk_ref[...] = jnp.zeros_like(dk_ref)
            dv_ref[...] = jnp.zeros_like(dv_ref)
            ds_ref[...] = jnp.zeros_like(ds_ref)

        k = jnp.concatenate([kp[...], kc[...], kn[...]], axis=0)
        v = jnp.concatenate([vp[...], vc[...], vn[...]], axis=0)
        q = _stack_heads(q_ref)
        do = _stack_heads(do_ref)
        p, ps = _attn_probs(q, k, _attn_valid(n, t), _sink_column(sink_ref, g))
        dp = _dot(do, v, "nt")
        delta = jnp.sum(p * dp, axis=1, keepdims=True)
        dsc = (p * (dp - delta) * scale).astype(BF16)
        dq = _dot(dsc, k, "nn")
        dsink = ps * delta
        for r in range(Q_PER_KV):
            rows = slice(r * BLOCK, (r + 1) * BLOCK)
            dq_ref[:, r * HEAD_DIM:(r + 1) * HEAD_DIM] = _rope_apply_t(
                dq[rows], c_ref[...], sa_ref[...], sb_ref[...]).astype(BF16)
            ds_ref[r:r + 1, :] -= jnp.broadcast_to(jnp.sum(dsink[rows], axis=0, keepdims=True), (1, HEAD_DIM))
        rows = pl.ds(pl.multiple_of(n * BLOCK + (KV_PAD - BLOCK), BLOCK), 3 * BLOCK)
        dk_ref[rows, :] += _dot(dsc, q, "tn")
        dv_ref[rows, :] += _dot(p.astype(BF16), do, "tn")

    acc_shape = jax.ShapeDtypeStruct((cfg.g, t + 2 * KV_PAD, HEAD_DIM), F32)
    return pl.pallas_call(
        body,
        grid=(cfg.g, nb),
        in_specs=[pl.BlockSpec(memory_space=pltpu.SMEM), q_spec] + k_specs + v_specs + [q_spec, tab, tab, tab],
        out_specs=[q_spec, acc_spec, acc_spec, pl.BlockSpec((None, 8, HEAD_DIM), lambda g, n: (g, 0, 0))],
        out_shape=[jax.ShapeDtypeStruct((t, cfg.attn), BF16), acc_shape, acc_shape,
                   jax.ShapeDtypeStruct((cfg.g, 8, HEAD_DIM), F32)],
        compiler_params=_cparams("arbitrary", "arbitrary"),
        name=name,
    )(sink, qk, qk, qk, qk, z, z, z, dmix, *tabs)


def _kv_finish(dk_acc, dv_acc, tabs, cfg, *, name):
    t = dk_acc.shape[1] - 2 * KV_PAD

    def body(dk_ref, dv_ref, c_ref, sa_ref, sb_ref, ok_ref, ov_ref):
        ok_ref[...] = _rope_apply_t(dk_ref[...], c_ref[...], sa_ref[...], sb_ref[...]).astype(BF16)
        ov_ref[...] = dv_ref[...].astype(BF16)

    acc = pl.BlockSpec((None, KV_PAD, HEAD_DIM), lambda g, i: (g, i + 1, 0))
    tab = pl.BlockSpec((KV_PAD, HEAD_DIM), lambda g, i: (i, 0))
    out = pl.BlockSpec((KV_PAD, HEAD_DIM), lambda g, i: (i, g))
    return pl.pallas_call(
        body,
        grid=(cfg.g, t // KV_PAD),
        in_specs=[acc, acc, tab, tab, tab],
        out_specs=[out, out],
        out_shape=[jax.ShapeDtypeStruct((t, cfg.kv), BF16)] * 2,
        compiler_params=_cparams("parallel", "parallel"),
        name=name,
    )(dk_acc, dv_acc, *tabs)


def _halo_specs(width, col, tb, t):
    per = tb // HALO_ROWS
    last = t // HALO_ROWS - 1
    prev = pl.BlockSpec((HALO_ROWS, width), lambda i: (jnp.maximum(i * per - 1, 0), col))
    cur = pl.BlockSpec((tb, width), lambda i: (i, col))
    nxt = pl.BlockSpec((HALO_ROWS, width), lambda i: (jnp.minimum((i + 1) * per, last), col))
    return [prev, cur, nxt]


def _halo_load(refs):
    return jnp.concatenate([r[...].astype(F32) for r in refs], axis=0)


F32_SUBLANES = 8


CONV_ROWS = 64
LANES = 128


def _store_phases(ph_ref, v):
    total = v.shape[0]
    ph_ref[0] = v
    for p in range(1, F32_SUBLANES):
        ph_ref[p] = pltpu.roll(v, total - p, axis=0)


def _shifted(ph_ref, start, r0, lanes):
    base = start - start % F32_SUBLANES + r0
    return ph_ref[start % F32_SUBLANES, base:base + CONV_ROWS, lanes]


def _conv_blocks(tb, cw):
    return [(r0, slice(l0, l0 + LANES)) for l0 in range(0, cw, LANES) for r0 in range(0, tb, CONV_ROWS)]


def _conv_glu(a_refs, g_refs, i, tb, t):
    a = _halo_load(a_refs)
    g = _halo_load(g_refs)
    rows = i * tb - HALO_ROWS + lax.broadcasted_iota(jnp.int32, (tb + 2 * HALO_ROWS, 1), 0)
    valid = (rows >= 0) & (rows < t)
    sg = _sigmoid(g)
    return a, sg, jnp.where(valid, a * sg, 0.0), valid


def _conv_fwd(z, mix, w, b, lg, lb, cfg, *, tb, name):
    t = z.shape[0]
    cw = cfg.conv
    vec = pl.BlockSpec((1, cw), lambda i: (0, 0))

    def body(ap, ac, an, gp, gc, gn, w_ref, b_ref, lg_ref, lb_ref, mix_ref, o_ref, y_ref, c_ph):
        _, _, c, _ = _conv_glu((ap, ac, an), (gp, gc, gn), pl.program_id(0), tb, t)
        _store_phases(c_ph, c)
        for r0, lanes in _conv_blocks(tb, cw):
            acc = jnp.zeros((CONV_ROWS, LANES), F32)
            for j in range(CONV_KERNEL):
                acc = acc + w_ref[j:j + 1, lanes] * _shifted(c_ph, j + HALO_ROWS - CONV_PAD, r0, lanes)
            y_ref[r0:r0 + CONV_ROWS, lanes] = acc + b_ref[:, lanes]
        y = y_ref[...]
        mu = jnp.mean(y, axis=-1, keepdims=True)
        dlt = y - mu
        var = jnp.mean(dlt * dlt, axis=-1, keepdims=True)
        yn = dlt * lax.rsqrt(var + EPS) * lg_ref[...] + lb_ref[...]
        o_ref[...] = (yn * _sigmoid(yn)).astype(BF16)

    return pl.pallas_call(
        body,
        grid=(t // tb,),
        in_specs=(_halo_specs(cw, cfg.off_ca // cw, tb, t) + _halo_specs(cw, cfg.off_cg // cw, tb, t)
                  + [pl.BlockSpec((CONV_KERNEL + 1, cw), lambda i: (0, 0)), vec, vec, vec,
                     pl.BlockSpec(memory_space=pl.ANY)]),
        out_specs=[pl.BlockSpec((tb, cw), lambda i: (i, cfg.attn // cw)), pl.BlockSpec((tb, cw), lambda i: (i, 0))],
        out_shape=[jax.ShapeDtypeStruct(mix.shape, BF16), jax.ShapeDtypeStruct((t, cw), F32)],
        input_output_aliases={10: 0},
        scratch_shapes=[pltpu.VMEM((F32_SUBLANES, tb + 2 * HALO_ROWS, cw), F32)],
        compiler_params=_cparams("parallel"),
        name=name,
    )(z, z, z, z, z, z, w, b, lg, lb, mix)


def _conv_bwd(z, y, dmix, w, lg, lb, cfg, *, tb, name):
    t = z.shape[0]
    cw = cfg.conv
    vec = pl.BlockSpec((1, cw), lambda i: (0, 0))
    cen = slice(HALO_ROWS, HALO_ROWS + tb)

    def body(ap, ac, an, gp, gc, gn, yp, yc, yn_, dp, dc_, dn, w_ref, lg_ref, lb_ref,
             da_ref, dg_ref, dw_ref, st_ref, dy_ph, c_ph):
        @pl.when(pl.program_id(0) == 0)
        def _():
            dw_ref[...] = jnp.zeros_like(dw_ref)
            st_ref[...] = jnp.zeros_like(st_ref)

        a, sg, c, valid = _conv_glu((ap, ac, an), (gp, gc, gn), pl.program_id(0), tb, t)
        yv = _halo_load((yp, yc, yn_))
        do = _halo_load((dp, dc_, dn))
        mu = jnp.mean(yv, axis=-1, keepdims=True)
        dlt = yv - mu
        rstd = lax.rsqrt(jnp.mean(dlt * dlt, axis=-1, keepdims=True) + EPS)
        xh = dlt * rstd
        lgv = lg_ref[...]
        yn = xh * lgv + lb_ref[...]
        s = _sigmoid(yn)
        dyn = do * (s * (1.0 + yn * (1.0 - s)))
        dxh = dyn * lgv
        dy = rstd * (dxh - jnp.mean(dxh, axis=-1, keepdims=True)
                     - xh * jnp.mean(dxh * xh, axis=-1, keepdims=True))
        dy = jnp.where(valid, dy, 0.0)
        dyc = dy[cen]
        st_ref[0:1, :] += jnp.sum(dyc, axis=0, keepdims=True)
        st_ref[1:2, :] += jnp.sum((dyn * xh)[cen], axis=0, keepdims=True)
        st_ref[2:3, :] += jnp.sum(dyn[cen], axis=0, keepdims=True)
        _store_phases(dy_ph, dy)
        _store_phases(c_ph, c)
        for r0, lanes in _conv_blocks(tb, cw):
            dc = jnp.zeros((CONV_ROWS, LANES), F32)
            for j in range(CONV_KERNEL):
                dc = dc + w_ref[j:j + 1, lanes] * _shifted(dy_ph, HALO_ROWS + CONV_PAD - j, r0, lanes)
            rows = slice(HALO_ROWS + r0, HALO_ROWS + r0 + CONV_ROWS)
            sgc = sg[rows, lanes]
            da_ref[r0:r0 + CONV_ROWS, lanes] = (dc * sgc).astype(BF16)
            dg_ref[r0:r0 + CONV_ROWS, lanes] = (dc * a[rows, lanes] * sgc * (1.0 - sgc)).astype(BF16)
        for l0 in range(0, cw, LANES):
            lanes = slice(l0, l0 + LANES)
            for j in range(CONV_KERNEL):
                part = jnp.zeros((CONV_ROWS, LANES), F32)
                for r0 in range(0, tb, CONV_ROWS):
                    part = part + (_shifted(dy_ph, HALO_ROWS, r0, lanes)
                                   * _shifted(c_ph, j + HALO_ROWS - CONV_PAD, r0, lanes))
                dw_ref[j:j + 1, lanes] += jnp.sum(part, axis=0, keepdims=True)

    out = pl.BlockSpec((tb, cw), lambda i: (i, 0))
    wspec = pl.BlockSpec((CONV_KERNEL + 1, cw), lambda i: (0, 0))
    return pl.pallas_call(
        body,
        grid=(t // tb,),
        in_specs=(_halo_specs(cw, cfg.off_ca // cw, tb, t) + _halo_specs(cw, cfg.off_cg // cw, tb, t)
                  + _halo_specs(cw, 0, tb, t) + _halo_specs(cw, cfg.attn // cw, tb, t) + [wspec, vec, vec]),
        out_specs=[out, out, wspec, pl.BlockSpec((8, cw), lambda i: (0, 0))],
        out_shape=[jax.ShapeDtypeStruct((t, cw), BF16), jax.ShapeDtypeStruct((t, cw), BF16),
                   jax.ShapeDtypeStruct((CONV_KERNEL + 1, cw), F32), jax.ShapeDtypeStruct((8, cw), F32)],
        scratch_shapes=[pltpu.VMEM((F32_SUBLANES, tb + 2 * HALO_ROWS, cw), F32)] * 2,
        compiler_params=_cparams("arbitrary"),
        name=name,
    )(z, z, z, z, z, z, y, y, y, dmix, dmix, dmix, w, lg, lb)


_SQRT_HALF = 1.0 / math.sqrt(2.0)
_INV_SQRT_2PI = 1.0 / math.sqrt(2.0 * math.pi)


def _gelu(v):
    return 0.5 * v * (1.0 + lax.erf(v * _SQRT_HALF))


def _gelu_grad(v):
    return 0.5 * (1.0 + lax.erf(v * _SQRT_HALF)) + v * jnp.exp(-0.5 * v * v) * _INV_SQRT_2PI


def _sgu_norm(zv_ref, lg_ref, lb_ref):
    xv = zv_ref[...].astype(F32)
    v = _gelu(xv)
    mu = jnp.mean(v, axis=-1, keepdims=True)
    dlt = v - mu
    rstd = lax.rsqrt(jnp.mean(dlt * dlt, axis=-1, keepdims=True) + EPS)
    xh = dlt * rstd
    return xv, xh, rstd, xh * lg_ref[...] + lb_ref[...]


def _sgu_specs(cfg):
    sw = cfg.sgu
    zu = pl.BlockSpec((CHUNK, sw), lambda i: (i, cfg.off_u // sw))
    zv = pl.BlockSpec((CHUNK, sw), lambda i: (i, cfg.off_sv // sw))
    vec = pl.BlockSpec((1, sw), lambda i: (0, 0))
    ws = pl.BlockSpec((cfg.sh, CHUNK, CHUNK), lambda i: (0, 0, 0))
    bs = pl.BlockSpec((cfg.sh, CHUNK, 1), lambda i: (0, 0, 0))
    return zu, zv, vec, ws, bs


def _sgu_fwd(z, mix, lg, lb, ws, bs, cfg, *, name):
    t = z.shape[0]
    sw = cfg.sgu
    zu, zv, vec, wspec, bspec = _sgu_specs(cfg)

    def body(zu_ref, zv_ref, lg_ref, lb_ref, ws_ref, bs_ref, mix_ref, o_ref):
        u = _gelu(zu_ref[...].astype(F32))
        _, _, _, vn = _sgu_norm(zv_ref, lg_ref, lb_ref)
        vnb = vn.astype(BF16)
        for h in range(cfg.sh):
            sl = slice(h * HEAD_DIM, (h + 1) * HEAD_DIM)
            sp = _dot(ws_ref[h], vnb[:, sl], "nn") + bs_ref[h]
            o_ref[:, sl] = (u[:, sl] * sp).astype(BF16)

    return pl.pallas_call(
        body,
        grid=(t // CHUNK,),
        in_specs=[zu, zv, vec, vec, wspec, bspec, pl.BlockSpec(memory_space=pl.ANY)],
        out_specs=pl.BlockSpec((CHUNK, sw), lambda i: (i, (cfg.attn + cfg.conv) // sw)),
        out_shape=jax.ShapeDtypeStruct(mix.shape, BF16),
        input_output_aliases={6: 0},
        compiler_params=_cparams("parallel"),
        name=name,
    )(z, z, lg, lb, ws, bs, mix)


def _sgu_bwd(z, dmix, lg, lb, ws, bs, cfg, *, name):
    t = z.shape[0]
    sw = cfg.sgu
    zu, zv, vec, wspec, bspec = _sgu_specs(cfg)
    do_spec = pl.BlockSpec((CHUNK, sw), lambda i: (i, (cfg.attn + cfg.conv) // sw))

    def body(zu_ref, zv_ref, do_ref, lg_ref, lb_ref, ws_ref, bs_ref, duv_ref, dws_ref, dbs_ref, st_ref):
        @pl.when(pl.program_id(0) == 0)
        def _():
            dws_ref[...] = jnp.zeros_like(dws_ref)
            dbs_ref[...] = jnp.zeros_like(dbs_ref)
            st_ref[...] = jnp.zeros_like(st_ref)

        xu = zu_ref[...].astype(F32)
        u = _gelu(xu)
        xv, xh, rstd, vn = _sgu_norm(zv_ref, lg_ref, lb_ref)
        vnb = vn.astype(BF16)
        do = do_ref[...].astype(F32)
        dvn_parts = []
        for h in range(cfg.sh):
            sl = slice(h * HEAD_DIM, (h + 1) * HEAD_DIM)
            wh = ws_ref[h]
            sp = _dot(wh, vnb[:, sl], "nn") + bs_ref[h]
            dsp = do[:, sl] * u[:, sl]
            dspb = dsp.astype(BF16)
            dvn_parts.append(_dot(wh, dspb, "tn"))
            dws_ref[h] += _dot(dspb, vnb[:, sl], "nt")
            dbs_ref[h] += jnp.sum(dsp, axis=1, keepdims=True)
            duv_ref[:, sl] = (do[:, sl] * sp * _gelu_grad(xu[:, sl])).astype(BF16)
        dvn = jnp.concatenate(dvn_parts, axis=1)
        st_ref[0:1, :] += jnp.sum(dvn * xh, axis=0, keepdims=True)
        st_ref[1:2, :] += jnp.sum(dvn, axis=0, keepdims=True)
        dxh = dvn * lg_ref[...]
        dv = rstd * (dxh - jnp.mean(dxh, axis=-1, keepdims=True)
                     - xh * jnp.mean(dxh * xh, axis=-1, keepdims=True))
        duv_ref[:, sw:] = (dv * _gelu_grad(xv)).astype(BF16)

    return pl.pallas_call(
        body,
        grid=(t // CHUNK,),
        in_specs=[zu, zv, do_spec, vec, vec, wspec, bspec],
        out_specs=[pl.BlockSpec((CHUNK, 2 * sw), lambda i: (i, 0)), wspec, bspec,
                   pl.BlockSpec((8, sw), lambda i: (0, 0))],
        out_shape=[jax.ShapeDtypeStruct((t, 2 * sw), BF16), jax.ShapeDtypeStruct((cfg.sh, CHUNK, CHUNK), F32),
                   jax.ShapeDtypeStruct((cfg.sh, CHUNK, 1), F32), jax.ShapeDtypeStruct((8, sw), F32)],
        compiler_params=_cparams("arbitrary"),
        name=name,
    )(z, z, dmix, lg, lb, ws, bs)


def _sum_shards(own, landed, *, tr, name):
    _, r, c = own.shape
    my_id = _flat_id(_my_coords()).astype(jnp.int32).reshape(1)

    def body(me_ref, *refs):
        o_ref = refs[N_DEV]
        acc = refs[0][...].astype(F32)
        for p_ref in refs[1:N_DEV]:
            acc = acc + p_ref[...].astype(F32)
        o_ref[...] = acc

    def slab(p):
        return pl.BlockSpec((None, tr, c), lambda i, me: (me[0] ^ p, i, 0))

    return pl.pallas_call(
        body,
        grid_spec=pltpu.PrefetchScalarGridSpec(
            num_scalar_prefetch=1,
            grid=(r // tr,),
            in_specs=[slab(p) for p in range(N_DEV)],
            out_specs=pl.BlockSpec((tr, c), lambda i, me: (i, 0)),
        ),
        out_shape=jax.ShapeDtypeStruct((r, c), F32),
        compiler_params=_cparams("parallel"),
        name=name,
    )(my_id, own, *([landed] * (N_DEV - 1)))


def _adamw(w, g, m, v, *, name):
    r, c = w.shape
    tr = _pick(r, [p for p in (1024, 512, 256, 128, 64, 32, 16, 8) if p * c <= ADAMW_TILE_ELEMS])

    def body(w_ref, g_ref, m_ref, v_ref, d_ref, nm_ref, nv_ref):
        gv = g_ref[...]
        nm = ADAM_B1 * m_ref[...] + (1.0 - ADAM_B1) * gv
        nv = ADAM_B2 * v_ref[...] + (1.0 - ADAM_B2) * (gv * gv)
        m_hat = nm / (1.0 - ADAM_B1 ** ADAM_STEP)
        v_hat = nv / (1.0 - ADAM_B2 ** ADAM_STEP)
        d_ref[...] = -ADAM_LR * (m_hat / (jnp.sqrt(v_hat) + ADAM_EPS) + ADAM_WD * w_ref[...])
        nm_ref[...] = nm
        nv_ref[...] = nv

    blk = pl.BlockSpec((tr, c), lambda i: (i, 0))
    return pl.pallas_call(
        body,
        grid=(r // tr,),
        in_specs=[blk] * 4,
        out_specs=[blk] * 3,
        out_shape=[jax.ShapeDtypeStruct((r, c), F32)] * 3,
        compiler_params=_cparams("parallel"),
        name=name,
    )(w, g, m, v)


def _my_coords():
    return tuple(lax.axis_index(a) for a in MESH_AXES)


def _peer_coords(me, p):
    return tuple(1 - v if (p >> (2 - a)) & 1 else v for a, v in enumerate(me))


def _flat_id(coords):
    return 4 * coords[0] + 2 * coords[1] + coords[2]


def _exchange(arrs, *, scatter, name):
    na = len(arrs)

    def body(*refs):
        ins = refs[:na]
        outs = refs[na:2 * na]
        send_sems, recv_sems, local_sems = refs[2 * na:]
        me = _my_coords()
        my_id = _flat_id(me)

        local = []
        for k in range(na):
            src = ins[k].at[my_id] if scatter else ins[k]
            cp = pltpu.make_async_copy(src, outs[k].at[my_id], local_sems.at[k])
            cp.start()
            local.append(cp)

        def remote(p, k):
            peer = _peer_coords(me, p)
            peer_id = _flat_id(peer)
            sem = (p - 1) * na + k
            src = ins[k].at[peer_id] if scatter else ins[k]
            send = pltpu.make_async_remote_copy(
                src_ref=src, dst_ref=outs[k].at[my_id], send_sem=send_sems.at[sem],
                recv_sem=recv_sems.at[sem], device_id=peer, device_id_type=MESH_ID)
            recv = pltpu.make_async_remote_copy(
                src_ref=src, dst_ref=outs[k].at[peer_id], send_sem=send_sems.at[sem],
                recv_sem=recv_sems.at[sem], device_id=peer, device_id_type=MESH_ID)
            return send, recv

        pairs = [remote(p, k) for p in range(1, N_DEV) for k in range(na)]
        for send, _ in pairs:
            send.start()
        for _, recv in pairs:
            recv.wait_recv()
        for send, _ in pairs:
            send.wait_send()
        for cp in local:
            cp.wait()

    def out_of(a):
        return jax.ShapeDtypeStruct(a.shape if scatter else (N_DEV,) + a.shape, a.dtype)

    hbm = pl.BlockSpec(memory_space=pl.ANY)
    nsem = (N_DEV - 1) * na
    return pl.pallas_call(
        body,
        in_specs=[hbm] * na,
        out_specs=[hbm] * na,
        out_shape=[out_of(a) for a in arrs],
        scratch_shapes=[pltpu.SemaphoreType.DMA((nsem,)), pltpu.SemaphoreType.DMA((nsem,)),
                        pltpu.SemaphoreType.DMA((na,))],
        name=name,
    )(*arrs)


_HBM = pl.BlockSpec(memory_space=pltpu.HBM)
_SEM = pl.BlockSpec(memory_space=pltpu.SEMAPHORE)
_EFFECT = pltpu.SideEffectType.DATAFLOW_SIDE_EFFECTING


def _place_own(land, src, *, scatter, name):
    _, r, c = land.shape
    tr = _pick(r, (256, 128, 64, 32, 16))
    my_id = _flat_id(_my_coords()).astype(jnp.int32).reshape(1)

    def body(me_ref, land_ref, src_ref, out_ref):
        out_ref[...] = src_ref[...]

    if scatter:
        src_spec = pl.BlockSpec((None, tr, c), lambda i, me: (me[0], i, 0))
    else:
        src_spec = pl.BlockSpec((tr, c), lambda i, me: (i, 0))
    return pl.pallas_call(
        body,
        grid_spec=pltpu.PrefetchScalarGridSpec(
            num_scalar_prefetch=1,
            grid=(r // tr,),
            in_specs=[pl.BlockSpec(memory_space=pl.ANY), src_spec],
            out_specs=pl.BlockSpec((None, tr, c), lambda i, me: (me[0], i, 0)),
        ),
        out_shape=jax.ShapeDtypeStruct(land.shape, land.dtype),
        input_output_aliases={1: 0},
        compiler_params=_cparams("parallel"),
        name=name,
    )(my_id, land, src)


def _peer_copy(src_ref, land_ref, send_sems, recv_sems, me, p, scatter, arrival):
    peer = _peer_coords(me, p)
    peer_id = _flat_id(peer)
    return pltpu.make_async_remote_copy(
        src_ref=src_ref.at[peer_id] if scatter else src_ref,
        dst_ref=land_ref.at[peer_id if arrival else _flat_id(me)],
        send_sem=send_sems.at[p - 1], recv_sem=recv_sems.at[p - 1], device_id=peer, device_id_type=MESH_ID)


def _exchange_start(src, land, *, scatter, name):
    def body(src_ref, land_ref, send_sems, recv_sems, src_thru, land_thru, token):
        me = _my_coords()
        for p in range(1, N_DEV):
            _peer_copy(src_ref, land_ref, send_sems, recv_sems, me, p, scatter, False).start()
        token[...] = jnp.zeros_like(token)

    nsem = N_DEV - 1
    return pl.pallas_call(
        body,
        name=name,
        out_shape=(pltpu.SemaphoreType.DMA((nsem,)), pltpu.SemaphoreType.DMA((nsem,)),
                   pltpu.HBM(src.shape, src.dtype), pltpu.HBM(land.shape, land.dtype),
                   jax.ShapeDtypeStruct((8, 128), F32)),
        in_specs=(_HBM, _HBM),
        out_specs=(_SEM, _SEM, _HBM, _HBM, pl.BlockSpec(memory_space=pltpu.VMEM)),
        input_output_aliases={0: 2, 1: 3},
        compiler_params=pltpu.CompilerParams(has_side_effects=_EFFECT),
    )(pltpu.with_memory_space_constraint(src, pltpu.HBM), pltpu.with_memory_space_constraint(land, pltpu.HBM))


def _exchange_wait(handle, after, *, scatter, name):
    send_sems, recv_sems, src_thru, land_thru = handle

    def body(src_ref, land_ref, send_sems, recv_sems, after_ref, src_dead, got_ref):
        me = _my_coords()
        for p in range(1, N_DEV):
            _peer_copy(src_ref, land_ref, send_sems, recv_sems, me, p, scatter, False).wait_send()
            _peer_copy(src_ref, land_ref, send_sems, recv_sems, me, p, scatter, True).wait_recv()

    return pl.pallas_call(
        body,
        name=name,
        out_shape=(pltpu.HBM(src_thru.shape, src_thru.dtype), pltpu.HBM(land_thru.shape, land_thru.dtype)),
        in_specs=(_HBM, _HBM, _SEM, _SEM, pl.BlockSpec(memory_space=pl.ANY)),
        out_specs=(_HBM, _HBM),
        input_output_aliases={0: 0, 1: 1},
        compiler_params=pltpu.CompilerParams(has_side_effects=_EFFECT),
    )(src_thru, land_thru, send_sems, recv_sems, after)


def _exchange_begin(src, *, scatter, name):
    if scatter:
        land = lax.empty(src.shape, src.dtype)
    else:
        land = _place_own(lax.empty((N_DEV,) + src.shape, src.dtype), src, scatter=False, name=name + "_own")
    *handle, token = _exchange_start(src, land, scatter=scatter, name=name + "_start")
    return tuple(handle), token[0, 0]


_SIBLING_MASK = 1
_CHIP_MASKS = (2, 4, 6)
_DIRECT_MASKS = (_SIBLING_MASK,) + _CHIP_MASKS


def _direct_copy(src_ref, land_ref, send_sems, recv_sems, me, a, j, arrival):
    p = _DIRECT_MASKS[j]
    peer = _peer_coords(me, p)
    sem = a * len(_DIRECT_MASKS) + j
    return pltpu.make_async_remote_copy(
        src_ref=src_ref, dst_ref=land_ref.at[_flat_id(peer) if arrival else _flat_id(me)],
        send_sem=send_sems.at[sem], recv_sem=recv_sems.at[sem], device_id=peer, device_id_type=MESH_ID)


def _relay_copy(land_ref, send_sems, recv_sems, me, a, j, arrival):
    sibling = _peer_coords(me, _SIBLING_MASK)
    holder = sibling if arrival else me
    slab = land_ref.at[_flat_id(_peer_coords(holder, _CHIP_MASKS[j]))]
    sem = a * len(_CHIP_MASKS) + j
    return pltpu.make_async_remote_copy(
        src_ref=slab, dst_ref=slab, send_sem=send_sems.at[sem], recv_sem=recv_sems.at[sem],
        device_id=sibling, device_id_type=MESH_ID)


def _hbm_like(arrs):
    return tuple(pltpu.HBM(a.shape, a.dtype) for a in arrs)


def _gather_direct_start(srcs, lands, *, name):
    k = len(srcs)

    def body(*refs):
        send_sems, recv_sems = refs[2 * k:2 * k + 2]
        me = _my_coords()
        for a in range(k):
            for j in range(len(_DIRECT_MASKS)):
                _direct_copy(refs[a], refs[k + a], send_sems, recv_sems, me, a, j, False).start()
        refs[-1][...] = jnp.zeros_like(refs[-1])

    nsem = k * len(_DIRECT_MASKS)
    hbm_in = [pltpu.with_memory_space_constraint(a, pltpu.HBM) for a in (*srcs, *lands)]
    return pl.pallas_call(
        body,
        name=name,
        out_shape=(pltpu.SemaphoreType.DMA((nsem,)), pltpu.SemaphoreType.DMA((nsem,)),
                   *_hbm_like(srcs), *_hbm_like(lands), jax.ShapeDtypeStruct((8, 128), F32)),
        in_specs=(_HBM,) * (2 * k),
        out_specs=(_SEM, _SEM) + (_HBM,) * (2 * k) + (pl.BlockSpec(memory_space=pltpu.VMEM),),
        input_output_aliases={i: 2 + i for i in range(2 * k)},
        compiler_params=pltpu.CompilerParams(has_side_effects=_EFFECT),
    )(*hbm_in)


def _gather_direct_wait(send_sems, recv_sems, srcs, lands, after, *, name):
    k = len(srcs)

    def body(*refs):
        send_sems, recv_sems = refs[2 * k:2 * k + 2]
        me = _my_coords()
        for a in range(k):
            for j in range(len(_DIRECT_MASKS)):
                _direct_copy(refs[a], refs[k + a], send_sems, recv_sems, me, a, j, False).wait_send()
                _direct_copy(refs[a], refs[k + a], send_sems, recv_sems, me, a, j, True).wait_recv()

    return pl.pallas_call(
        body,
        name=name,
        out_shape=(*_hbm_like(srcs), *_hbm_like(lands)),
        in_specs=(_HBM,) * (2 * k) + (_SEM, _SEM, pl.BlockSpec(memory_space=pl.ANY)),
        out_specs=(_HBM,) * (2 * k),
        input_output_aliases={i: i for i in range(2 * k)},
        compiler_params=pltpu.CompilerParams(has_side_effects=_EFFECT),
    )(*srcs, *lands, send_sems, recv_sems, after)[k:]


def _gather_relay_start(lands, *, name):
    k = len(lands)

    def body(*refs):
        send_sems, recv_sems = refs[k:k + 2]
        me = _my_coords()
        for a in range(k):
            for j in range(len(_CHIP_MASKS)):
                _relay_copy(refs[a], send_sems, recv_sems, me, a, j, False).start()
        refs[-1][...] = jnp.zeros_like(refs[-1])

    nsem = k * len(_CHIP_MASKS)
    return pl.pallas_call(
        body,
        name=name,
        out_shape=(pltpu.SemaphoreType.DMA((nsem,)), pltpu.SemaphoreType.DMA((nsem,)),
                   *_hbm_like(lands), jax.ShapeDtypeStruct((8, 128), F32)),
        in_specs=(_HBM,) * k,
        out_specs=(_SEM, _SEM) + (_HBM,) * k + (pl.BlockSpec(memory_space=pltpu.VMEM),),
        input_output_aliases={i: 2 + i for i in range(k)},
        compiler_params=pltpu.CompilerParams(has_side_effects=_EFFECT),
    )(*lands)


def _gather_relay_wait(send_sems, recv_sems, lands, after, *, name):
    k = len(lands)

    def body(*refs):
        send_sems, recv_sems = refs[k:k + 2]
        me = _my_coords()
        for a in range(k):
            for j in range(len(_CHIP_MASKS)):
                _relay_copy(refs[a], send_sems, recv_sems, me, a, j, False).wait_send()
                _relay_copy(refs[a], send_sems, recv_sems, me, a, j, True).wait_recv()

    return pl.pallas_call(
        body,
        name=name,
        out_shape=_hbm_like(lands),
        in_specs=(_HBM,) * k + (_SEM, _SEM, pl.BlockSpec(memory_space=pl.ANY)),
        out_specs=(_HBM,) * k,
        input_output_aliases={i: i for i in range(k)},
        compiler_params=pltpu.CompilerParams(has_side_effects=_EFFECT),
    )(*lands, send_sems, recv_sems, after)


def _gather_group_begin(srcs, *, name):
    lands = [_place_own(lax.empty((N_DEV,) + s.shape, s.dtype), s, scatter=False, name=f"{name}_own{i}")
             for i, s in enumerate(srcs)]
    k = len(srcs)
    out = _gather_direct_start(srcs, lands, name=name + "_start")
    return (out[0], out[1], out[2:2 + k], out[2 + k:2 + 2 * k]), out[-1][0, 0]


def _gather_group_relay(handle, after, *, name):
    lands = _gather_direct_wait(*handle, after, name=name + "_landed")
    out = _gather_relay_start(lands, name=name + "_relay")
    return (out[0], out[1], out[2:-1]), out[-1][0, 0]


def _gather_group_end(handle, after, *, name):
    return _gather_relay_wait(*handle, after, name=name + "_done")


def _allreduce_small(flat, *, name):
    r, c = flat.shape
    rc = r // N_DEV
    assert rc * N_DEV == r and rc % F32_SUBLANES == 0, r

    def body(in_ref, out_ref, buf, send1, recv1, send2, recv2):
        me = _my_coords()
        my_id = _flat_id(me)

        def rows(dev_id):
            return pl.ds(pl.multiple_of(dev_id * rc, F32_SUBLANES), rc)

        def exchange(copies):
            for send, _ in copies:
                send.start()
            for _, recv in copies:
                recv.wait_recv()
            for send, _ in copies:
                send.wait_send()

        def scatter_copy(p):
            peer = _peer_coords(me, p)
            common = dict(src_ref=in_ref.at[rows(_flat_id(peer))], send_sem=send1.at[p - 1],
                          recv_sem=recv1.at[p - 1], device_id=peer, device_id_type=MESH_ID)
            return (pltpu.make_async_remote_copy(dst_ref=buf.at[my_id], **common),
                    pltpu.make_async_remote_copy(dst_ref=buf.at[_flat_id(peer)], **common))

        def gather_copy(p):
            peer = _peer_coords(me, p)
            common = dict(src_ref=out_ref.at[rows(my_id)], send_sem=send2.at[p - 1],
                          recv_sem=recv2.at[p - 1], device_id=peer, device_id_type=MESH_ID)
            return (pltpu.make_async_remote_copy(dst_ref=out_ref.at[rows(my_id)], **common),
                    pltpu.make_async_remote_copy(dst_ref=out_ref.at[rows(_flat_id(peer))], **common))

        buf[my_id] = in_ref[rows(my_id), :]
        exchange([scatter_copy(p) for p in range(1, N_DEV)])
        acc = buf[0]
        for q in range(1, N_DEV):
            acc = acc + buf[q]
        out_ref[rows(my_id), :] = acc
        exchange([gather_copy(p) for p in range(1, N_DEV)])

    vmem = pl.BlockSpec(memory_space=pltpu.VMEM)
    return pl.pallas_call(
        body,
        in_specs=[vmem],
        out_specs=vmem,
        out_shape=jax.ShapeDtypeStruct((r, c), F32),
        scratch_shapes=[pltpu.VMEM((N_DEV, rc, c), F32)] + [pltpu.SemaphoreType.DMA((N_DEV - 1,))] * 4,
        compiler_params=pltpu.CompilerParams(vmem_limit_bytes=VMEM_LIMIT_BYTES),
        name=name,
    )(flat)


WEIGHT_NAMES = ("mix_norm_g", "w_in", "sink", "conv_dw_w", "conv_dw_b", "conv_ln_g", "conv_ln_b",
                "sgu_ln_g", "sgu_ln_b", "sgu_w", "sgu_b", "w_out", "ffn_norm_g", "w_gate", "w_up",
                "w_down", "final_norm_g")
SHARDED = ("w_in", "conv_dw_w", "w_out", "w_gate", "w_up", "w_down")
SMALL = tuple(n for n in WEIGHT_NAMES if n not in ("w_in", "w_out", "w_gate", "w_up", "w_down"))


def _pack_small(parts):
    flat = jnp.concatenate([parts[n].reshape(-1) for n in SMALL])
    pad = (-flat.shape[0]) % (N_DEV * F32_SUBLANES * LANES)
    return jnp.pad(flat, (0, pad)).reshape(-1, 128)


def _unpack_small(packed, shapes):
    flat = packed.reshape(-1)
    out, pos = {}, 0
    for n in SMALL:
        size = math.prod(shapes[n])
        out[n] = flat[pos:pos + size].reshape(shapes[n])
        pos += size
    return out


def kernel(x, mix_norm_g, w_in, sink, conv_dw_w, conv_dw_b, conv_ln_g, conv_ln_b, sgu_ln_g, sgu_ln_b, sgu_w, sgu_b, w_out, ffn_norm_g, w_gate, w_up, w_down, final_norm_g, loss_target, m_mix_norm_g, m_w_in, m_sink, m_conv_dw_w, m_conv_dw_b, m_conv_ln_g, m_conv_ln_b, m_sgu_ln_g, m_sgu_ln_b, m_sgu_w, m_sgu_b, m_w_out, m_ffn_norm_g, m_w_gate, m_w_up, m_w_down, m_final_norm_g, v_mix_norm_g, v_w_in, v_sink, v_conv_dw_w, v_conv_dw_b, v_conv_ln_g, v_conv_ln_b, v_sgu_ln_g, v_sgu_ln_b, v_sgu_w, v_sgu_b, v_w_out, v_ffn_norm_g, v_w_gate, v_w_up, v_w_down, v_final_norm_g):
    w = dict(mix_norm_g=mix_norm_g, w_in=w_in, sink=sink, conv_dw_w=conv_dw_w, conv_dw_b=conv_dw_b,
             conv_ln_g=conv_ln_g, conv_ln_b=conv_ln_b, sgu_ln_g=sgu_ln_g, sgu_ln_b=sgu_ln_b, sgu_w=sgu_w,
             sgu_b=sgu_b, w_out=w_out, ffn_norm_g=ffn_norm_g, w_gate=w_gate, w_up=w_up, w_down=w_down,
             final_norm_g=final_norm_g)
    mom_m = dict(zip(WEIGHT_NAMES, (m_mix_norm_g, m_w_in, m_sink, m_conv_dw_w, m_conv_dw_b, m_conv_ln_g,
                                    m_conv_ln_b, m_sgu_ln_g, m_sgu_ln_b, m_sgu_w, m_sgu_b, m_w_out,
                                    m_ffn_norm_g, m_w_gate, m_w_up, m_w_down, m_final_norm_g)))
    mom_v = dict(zip(WEIGHT_NAMES, (v_mix_norm_g, v_w_in, v_sink, v_conv_dw_w, v_conv_dw_b, v_conv_ln_g,
                                    v_conv_ln_b, v_sgu_ln_g, v_sgu_ln_b, v_sgu_w, v_sgu_b, v_w_out,
                                    v_ffn_norm_g, v_w_gate, v_w_up, v_w_down, v_final_norm_g)))

    _, t, d = x.shape
    depth = w_in.shape[0]
    cfg = Cfg(d, t)
    ff = w_gate.shape[2] * N_DEV
    my_id = _flat_id(_my_coords())
    xs = x[0]
    target = loss_target[0]

    tm = _pick(t, (1024, 512))
    tr = _pick(t, (256, 128))
    tb = _pick(t, (256, 128))
    tn_in = _pick(cfg.inw, (896, 512, 448))
    tn_ff = _pick(ff, (512, 1408, 704))
    tk_ff = ff
    tk_in = cfg.inw
    tn_d = _pick(d, (512,))
    tk_t = _pick(t, (2048, 1024, 512))
    tm_in = _pick(cfg.inw, (896, 448))
    tm_ff = _pick(ff, (1408, 704))
    tn_dw = _pick(d, (1024,))

    tabs = _rope_tables(t)

    cflat = conv_dw_w.reshape(-1)
    cshard = jnp.pad(cflat, (0, (-cflat.shape[0]) % (8 * 128))).reshape(-1, 128)
    dw_all = _exchange([cshard], scatter=False, name="gather_conv_w")[0]
    dw_all = dw_all.reshape(N_DEV, -1)[:, :cflat.shape[0]].reshape(N_DEV, depth, CONV_KERNEL, -1)
    dw_all = dw_all.transpose(1, 2, 0, 3).reshape(depth, CONV_KERNEL, cfg.conv)
    dw_pad = jnp.pad(dw_all, ((0, 0), (0, 1), (0, 0)))

    def row(v):
        return v.reshape(1, -1)

    first, rest = ("win",), ("wo", "wg", "wu", "wd")

    def gather_begin(l, names, zero):
        shards = dict(win=w_in[l].T, wo=w_out[l], wg=w_gate[l].T, wu=w_up[l].T, wd=w_down[l])
        handle, started = _gather_group_begin([(shards[k] + zero).astype(BF16) for k in names],
                                              name=f"gather_{names[0]}_{l}")
        return handle, zero + started

    def gather_end(handle, names, after, l):
        full = _gather_group_end(handle, after, name=f"gather_{names[0]}_{l}")
        return {k: f.reshape(-1, d) for k, f in zip(names, full)}

    saved = []
    scatters = [None] * depth
    dw_pad, started = lax.optimization_barrier((dw_pad, jnp.zeros((), F32)))
    landing_first, started = gather_begin(0, first, started)
    landing_rest, started = gather_begin(0, rest, started)
    relayed = None
    for l in range(depth):
        h = _rms_fwd(xs, row(mix_norm_g[l]) + started, tr=tr, name="mix_norm")
        if l == 0:
            relayed, _ = _gather_group_relay(landing_first, h, name="gather_win_0")
            wts = gather_end(relayed, first, h, 0)
        else:
            wts = gather_end(relayed, first + rest, h, l)
        win_t = wts["win"]
        z = _in_proj(h, win_t, tabs, cfg, tm=tm, tn=tn_in, name="in_proj")
        started = jnp.zeros((), F32)
        if l + 1 < depth:
            z, started = lax.optimization_barrier((z, started))
            landing_next, started = gather_begin(l + 1, first + rest, started)
        qk = z
        mix = _attn_fwd(qk, z, sink[l], cfg, name="attn_fwd")
        mix, conv_y = _conv_fwd(z, mix, dw_pad[l], row(conv_dw_b[l]), row(conv_ln_g[l]) + started,
                                row(conv_ln_b[l]), cfg, tb=tb, name="conv_fwd")
        ws_b = sgu_w[l].astype(BF16)
        bs_c = sgu_b[l][:, :, None]
        mix = _sgu_fwd(z, mix, row(sgu_ln_g[l]), row(sgu_ln_b[l]), ws_b, bs_c, cfg, name="sgu_fwd")
        if l == 0:
            relayed, _ = _gather_group_relay(landing_rest, mix, name="gather_wo_0")
            wts.update(gather_end(relayed, rest, mix, 0))
        wo, wg_t, wu_t, wd = wts["wo"], wts["wg"], wts["wu"], wts["wd"]
        x1 = _matmul(mix, wo, mode="nn", tm=tm, tn=tn_d, tk=d, epilogue=_ep_add, extras=(xs,),
                     out_dtypes=[F32], name="out_proj")[0]
        h2 = _rms_fwd(x1, row(ffn_norm_g[l]), tr=tr, name="ffn_norm")
        gate, up, act = _ffn_up(h2, wg_t, wu_t, tm=tm, tn=tn_ff, name="ffn_up")
        if l + 1 < depth:
            relayed, zero = _gather_group_relay(landing_next, act, name=f"gather_win_{l + 1}")
            act, zero = lax.optimization_barrier((act, zero))
            started = started + zero
        x2 = _matmul(act, wd, mode="nn", tm=tm, tn=tn_d, tk=tk_ff, epilogue=_ep_add, extras=(x1,),
                     out_dtypes=[F32], name="ffn_down")[0]
        saved.append(dict(x0=xs, h=h, z=z, qk=qk, mix=mix, conv_y=conv_y, x1=x1, h2=h2, gate=gate, up=up,
                          act=act, win_t=win_t, wg_t=wg_t, wu_t=wu_t, wo=wo, wd=wd, ws_b=ws_b, bs_c=bs_c))
        xs = x2

    dx, dxb, head = _loss_head(xs, row(final_norm_g), target, tr=tr, name="loss_head")
    loss = lax.psum(head[1, 0], MESH_AXES)

    def scatter_begin(grad, n, l):
        handle, zero = _exchange_begin(grad.reshape(N_DEV, -1, d), scatter=True, name=f"scatter_{n}_{l}")
        scatters[l][n] = handle
        return zero

    small = {n: [None] * depth for n in SMALL if n != "final_norm_g"}
    big = {n: [None] * depth for n in ("w_in", "w_out", "w_gate", "w_up", "w_down")}
    for l in reversed(range(depth)):
        s = saved[l]
        dgate, dup = _matmul(dxb, s["wd"], mode="nt", tm=tm, tn=tn_ff, tk=d, epilogue=_ep_swiglu_bwd,
                             extras=(s["gate"], s["up"]), out_dtypes=[BF16, BF16], name="ffn_down_bwd")
        dwd = _matmul(s["act"], dxb, mode="tn", tm=tm_ff, tn=tn_dw, tk=tk_t, epilogue=_ep_plain,
                      out_dtypes=[BF16], name="ffn_down_wgrad")[0]
        scatters[l] = {}
        started = scatter_begin(dwd, "w_down", l)
        dh2 = _matmul(dgate, s["wg_t"], mode="nn", tm=tm, tn=tn_d, tk=tk_ff, epilogue=_ep_plain,
                      out_dtypes=[F32], name="ffn_gate_bwd")[0]
        dh2 = _matmul(dup, s["wu_t"], mode="nn", tm=tm, tn=tn_d, tk=tk_ff, epilogue=_ep_add, extras=(dh2,),
                      out_dtypes=[BF16], name="ffn_up_bwd")[0]
        dwg_t = _matmul(dgate, s["h2"], mode="tn", tm=tm_ff, tn=tn_dw, tk=tk_t, epilogue=_ep_plain,
                        out_dtypes=[BF16], name="ffn_gate_wgrad")[0]
        dwu_t = _matmul(dup, s["h2"], mode="tn", tm=tm_ff, tn=tn_dw, tk=tk_t, epilogue=_ep_plain,
                        out_dtypes=[BF16], name="ffn_up_wgrad")[0]
        started = started + scatter_begin(dwg_t, "w_gate", l) + scatter_begin(dwu_t, "w_up", l)
        dx1, dx1b, dg2 = _rms_bwd(dh2, s["x1"], row(ffn_norm_g[l]) + started, dx, tr=tr, name="ffn_norm_bwd")

        dmix = _matmul(dx1b, s["wo"], mode="nt", tm=tm, tn=tn_d, tk=d, epilogue=_ep_plain,
                       out_dtypes=[BF16], name="out_proj_bwd")[0]
        dwo = _matmul(s["mix"], dx1b, mode="tn", tm=_pick(d, (1024,)), tn=tn_dw, tk=tk_t, epilogue=_ep_plain,
                      out_dtypes=[BF16], name="out_proj_wgrad")[0]
        started = scatter_begin(dwo, "w_out", l)
        dq, dk_acc, dv_acc, dsink = _attn_bwd(s["qk"], s["z"], dmix, sink[l], tabs, cfg, name="attn_bwd")
        dk, dv = _kv_finish(dk_acc, dv_acc, tabs, cfg, name="attn_bwd_kv")
        da, dcg, dcw, cst = _conv_bwd(s["z"], s["conv_y"], dmix, dw_pad[l], row(conv_ln_g[l]) + started,
                                      row(conv_ln_b[l]), cfg, tb=tb, name="conv_bwd")
        duv, dws, dbs, sst = _sgu_bwd(s["z"], dmix, row(sgu_ln_g[l]), row(sgu_ln_b[l]), s["ws_b"], s["bs_c"],
                                      cfg, name="sgu_bwd")
        dz = jnp.concatenate([dq, dk, dv, da, dcg, duv], axis=1)
        dh = _matmul(dz, s["win_t"], mode="nn", tm=tm, tn=tn_d, tk=tk_in, epilogue=_ep_plain,
                     out_dtypes=[BF16], name="in_proj_bwd")[0]
        dwin_t = _matmul(dz, s["h"], mode="tn", tm=tm_in, tn=tn_dw, tk=tk_t, epilogue=_ep_plain,
                         out_dtypes=[BF16], name="in_proj_wgrad")[0]
        started = scatter_begin(dwin_t, "w_in", l)
        dx, dxb, dg1 = _rms_bwd(dh, s["x0"], row(mix_norm_g[l]) + started, dx1, tr=tr, name="mix_norm_bwd")

        small["mix_norm_g"][l] = dg1[0]
        small["ffn_norm_g"][l] = dg2[0]
        small["sink"][l] = dsink[:, :Q_PER_KV, 0].reshape(-1)
        small["conv_dw_w"][l] = dcw[:CONV_KERNEL]
        small["conv_dw_b"][l] = cst[0]
        small["conv_ln_g"][l] = cst[1]
        small["conv_ln_b"][l] = cst[2]
        small["sgu_ln_g"][l] = sst[0]
        small["sgu_ln_b"][l] = sst[1]
        small["sgu_w"][l] = dws
        small["sgu_b"][l] = dbs[:, :, 0]

    grads, deltas, new_m, new_v = {}, {}, {}, {}

    def adamw(n):
        shape = w[n].shape
        view = lambda a: a.reshape(-1, shape[-1])
        dl, nm, nv = _adamw(view(w[n]), view(grads[n]), view(mom_m[n]), view(mom_v[n]), name="adamw")
        deltas[n], new_m[n], new_v[n] = dl.reshape(shape), nm.reshape(shape), nv.reshape(shape)

    after = dx
    for n in ("w_down", "w_gate", "w_up", "w_out", "w_in"):
        for l in reversed(range(depth)):
            own, landed = _exchange_wait(scatters[l][n], after, scatter=True, name=f"scatter_{n}_{l}_wait")
            total = _sum_shards(own, landed, tr=_pick(own.shape[1], (64, 32, 16)), name="sum_grads")
            big[n][l] = total.T if n in ("w_in", "w_gate", "w_up") else total
        grads[n] = jnp.stack(big[n])
        adamw(n)
        after = new_v[n]

    parts = {n: jnp.stack(v) for n, v in small.items()}
    parts["final_norm_g"] = head[0]
    shapes = {n: parts[n].shape for n in SMALL}
    summed = _unpack_small(_allreduce_small(_pack_small(parts), name="allreduce_small"), shapes)
    for n in SMALL:
        grads[n] = summed[n]
    cshard_w = conv_dw_w.shape[2]
    grads["conv_dw_w"] = lax.dynamic_slice_in_dim(summed["conv_dw_w"], my_id * cshard_w, cshard_w, axis=2)
    for n in SMALL:
        adamw(n)

    return (loss, dx[None], *[grads[n] for n in WEIGHT_NAMES], *[deltas[n] for n in WEIGHT_NAMES],
            *[new_m[n] for n in WEIGHT_NAMES], *[new_v[n] for n in WEIGHT_NAMES])
```

```python
import functools
import math

import jax
import jax.numpy as jnp
from jax import lax
from jax.experimental import pallas as pl
from jax.experimental.pallas import tpu as pltpu

F32 = jnp.float32
BF16 = jnp.bfloat16

HEAD_DIM = 128
Q_PER_KV = 4
WINDOW = 128
BLOCK = 128
ROT_DIM = 32
ROPE_THETA = 500000.0
CONV_KERNEL = 31
CONV_PAD = (CONV_KERNEL - 1) // 2
CHUNK = 128
EPS = 1e-6

ADAM_LR = 0.001
ADAM_B1 = 0.9
ADAM_B2 = 0.999
ADAM_EPS = 1e-08
ADAM_WD = 0.01
ADAM_STEP = 10

N_DEV = 8
MESH_AXES = ("x", "y", "c")
VMEM_LIMIT_BYTES = 56 * 1024 * 1024
HALO_ROWS = 16
KV_PAD = 512
ADAMW_TILE_ELEMS = 256 * 1024
MESH_ID = pl.DeviceIdType.MESH


class Cfg:
    def __init__(self, d_model, seq):
        self.d = d_model
        self.t = seq
        self.attn = d_model // 2
        self.hq = self.attn // HEAD_DIM
        self.g = self.hq // Q_PER_KV
        self.kv = self.g * HEAD_DIM
        self.conv = d_model // 4
        self.sgu = d_model // 4
        self.sh = self.sgu // HEAD_DIM
        self.off_k = self.attn
        self.off_v = self.attn + self.kv
        self.off_ca = self.attn + 2 * self.kv
        self.off_cg = self.off_ca + self.conv
        self.off_u = self.off_cg + self.conv
        self.off_sv = self.off_u + self.sgu
        self.inw = self.off_sv + self.sgu


def _pick(dim, prefs):
    for p in prefs:
        if dim % p == 0:
            return p
    return dim


def _cparams(*sem):
    return pltpu.CompilerParams(dimension_semantics=sem, vmem_limit_bytes=VMEM_LIMIT_BYTES)


def _sigmoid(v):
    return 0.5 * jnp.tanh(0.5 * v) + 0.5


_DN = {
    "nn": (((1,), (0,)), ((), ())),
    "nt": (((1,), (1,)), ((), ())),
    "tn": (((0,), (0,)), ((), ())),
}


def _dot(a, b, mode):
    return lax.dot_general(a, b, _DN[mode], preferred_element_type=F32)


def _matmul(a, b, *, mode, tm, tn, tk, epilogue, out_dtypes, extras=(), name):
    if mode == "tn":
        kdim, m = a.shape
        n = b.shape[1]
    elif mode == "nn":
        m, kdim = a.shape
        n = b.shape[1]
    else:
        m, kdim = a.shape
        n = b.shape[0]
    assert m % tm == 0 and n % tn == 0 and kdim % tk == 0, (name, m, n, kdim, tm, tn, tk)
    gm, gn, gk = m // tm, n // tn, kdim // tk
    if mode == "tn":
        a_spec = pl.BlockSpec((tk, tm), lambda i, j, k: (k, i))
    else:
        a_spec = pl.BlockSpec((tm, tk), lambda i, j, k: (i, k))
    if mode == "nt":
        b_spec = pl.BlockSpec((tn, tk), lambda i, j, k: (j, k))
    else:
        b_spec = pl.BlockSpec((tk, tn), lambda i, j, k: (k, j))
    tile = pl.BlockSpec((tm, tn), lambda i, j, k: (i, j))
    ne, no = len(extras), len(out_dtypes)

    def body(a_ref, b_ref, *rest):
        ex = rest[:ne]
        outs = rest[ne:ne + no]

        def finish(acc):
            vals = epilogue(acc, *[e[...] for e in ex])
            for o_ref, val in zip(outs, vals):
                o_ref[...] = val.astype(o_ref.dtype)

        part = _dot(a_ref[...], b_ref[...], mode)
        if gk == 1:
            finish(part)
        else:
            acc_ref = rest[ne + no]
            k = pl.program_id(2)

            @pl.when(k == 0)
            def _():
                acc_ref[...] = part

            if gk > 2:
                @pl.when((k > 0) & (k < gk - 1))
                def _():
                    acc_ref[...] += part

            @pl.when(k == gk - 1)
            def _():
                finish(acc_ref[...] + part)

    return pl.pallas_call(
        body,
        grid=(gm, gn, gk),
        in_specs=[a_spec, b_spec] + [tile] * ne,
        out_specs=[tile] * no,
        out_shape=[jax.ShapeDtypeStruct((m, n), dt) for dt in out_dtypes],
        scratch_shapes=[pltpu.VMEM((tm, tn), F32)] if gk > 1 else [],
        compiler_params=_cparams("parallel", "parallel", "arbitrary"),
        name=name,
    )(a, b, *extras)


def _ep_plain(acc):
    return (acc,)


def _ep_add(acc, r):
    return (r.astype(F32) + acc,)


def _ep_swiglu_bwd(dact, gate, up):
    gate = gate.astype(F32)
    up = up.astype(F32)
    s = _sigmoid(gate)
    silu = gate * s
    dgate = dact * up * (s * (1.0 + gate * (1.0 - s)))
    dup = dact * silu
    return dgate, dup


def _ffn_up(h, wgt, wut, *, tm, tn, name):
    m, kdim = h.shape
    n = wgt.shape[0]

    def body(h_ref, g_ref, u_ref, gate_ref, up_ref, act_ref):
        hv = h_ref[...]
        gate = _dot(hv, g_ref[...], "nt")
        up = _dot(hv, u_ref[...], "nt")
        gate_ref[...] = gate.astype(BF16)
        up_ref[...] = up.astype(BF16)
        act_ref[...] = (gate * _sigmoid(gate) * up).astype(BF16)

    tile = pl.BlockSpec((tm, tn), lambda i, j: (i, j))
    wspec = pl.BlockSpec((tn, kdim), lambda i, j: (j, 0))
    return pl.pallas_call(
        body,
        grid=(m // tm, n // tn),
        in_specs=[pl.BlockSpec((tm, kdim), lambda i, j: (i, 0)), wspec, wspec],
        out_specs=[tile] * 3,
        out_shape=[jax.ShapeDtypeStruct((m, n), BF16)] * 3,
        compiler_params=_cparams("parallel", "parallel"),
        name=name,
    )(h, wgt, wut)


def _rms_fwd(x, g, *, tr, name):
    t, d = x.shape

    def body(x_ref, g_ref, h_ref):
        xv = x_ref[...]
        r = lax.rsqrt(jnp.mean(xv * xv, axis=-1, keepdims=True) + EPS)
        h_ref[...] = (xv * r * g_ref[...]).astype(BF16)

    row = pl.BlockSpec((tr, d), lambda i: (i, 0))
    return pl.pallas_call(
        body,
        grid=(t // tr,),
        in_specs=[row, pl.BlockSpec((1, d), lambda i: (0, 0))],
        out_specs=row,
        out_shape=jax.ShapeDtypeStruct((t, d), BF16),
        compiler_params=_cparams("parallel"),
        name=name,
    )(x, g)


def _rms_bwd_math(dy, xv, g):
    r = lax.rsqrt(jnp.mean(xv * xv, axis=-1, keepdims=True) + EPS)
    xh = xv * r
    dg = jnp.sum(dy * xh, axis=0, keepdims=True)
    dyg = dy * g
    dx = r * (dyg - xh * jnp.mean(dyg * xh, axis=-1, keepdims=True))
    return dx, dg


def _rms_bwd(dh, x, g, dres, *, tr, name):
    t, d = x.shape

    def body(dh_ref, x_ref, g_ref, dres_ref, dx_ref, dxb_ref, dg_ref):
        dx, dg = _rms_bwd_math(dh_ref[...].astype(F32), x_ref[...], g_ref[...])
        dx = dx + dres_ref[...]
        dx_ref[...] = dx
        dxb_ref[...] = dx.astype(BF16)

        @pl.when(pl.program_id(0) == 0)
        def _():
            dg_ref[...] = jnp.zeros_like(dg_ref)

        dg_ref[0:1, :] += dg

    row = pl.BlockSpec((tr, d), lambda i: (i, 0))
    vec = pl.BlockSpec((1, d), lambda i: (0, 0))
    return pl.pallas_call(
        body,
        grid=(t // tr,),
        in_specs=[row, row, vec, row],
        out_specs=[row, row, pl.BlockSpec((8, d), lambda i: (0, 0))],
        out_shape=[jax.ShapeDtypeStruct((t, d), F32), jax.ShapeDtypeStruct((t, d), BF16),
                   jax.ShapeDtypeStruct((8, d), F32)],
        compiler_params=_cparams("arbitrary"),
        name=name,
    )(dh, x, g, dres)


def _loss_head(x, g, target, *, tr, name):
    t, d = x.shape

    def body(x_ref, g_ref, t_ref, dx_ref, dxb_ref, st_ref):
        xv = x_ref[...]
        gv = g_ref[...]
        r = lax.rsqrt(jnp.mean(xv * xv, axis=-1, keepdims=True) + EPS)
        err = xv * r * gv - t_ref[...]
        sq = jnp.sum(jnp.sum(err * err, axis=1, keepdims=True), axis=0, keepdims=True)
        dx, dg = _rms_bwd_math(err * (1.0 / d), xv, gv)
        dx_ref[...] = dx
        dxb_ref[...] = dx.astype(BF16)

        @pl.when(pl.program_id(0) == 0)
        def _():
            st_ref[...] = jnp.zeros_like(st_ref)

        st_ref[0:1, :] += dg
        st_ref[1:2, :] += jnp.broadcast_to(sq * (0.5 / d), (1, d))

    row = pl.BlockSpec((tr, d), lambda i: (i, 0))
    return pl.pallas_call(
        body,
        grid=(t // tr,),
        in_specs=[row, pl.BlockSpec((1, d), lambda i: (0, 0)), row],
        out_specs=[row, row, pl.BlockSpec((8, d), lambda i: (0, 0))],
        out_shape=[jax.ShapeDtypeStruct((t, d), F32), jax.ShapeDtypeStruct((t, d), BF16),
                   jax.ShapeDtypeStruct((8, d), F32)],
        compiler_params=_cparams("arbitrary"),
        name=name,
    )(x, g, target)


def _rope_tables(t):
    half = ROT_DIM // 2
    pos = jnp.arange(t, dtype=F32)
    inv = ROPE_THETA ** (-jnp.arange(0, ROT_DIM, 2, dtype=F32) / ROT_DIM)
    ang = pos[:, None] * inv[None, :]
    cos, sin = jnp.cos(ang), jnp.sin(ang)
    rest = HEAD_DIM - ROT_DIM
    c = jnp.concatenate([cos, cos, jnp.ones((t, rest), F32)], axis=1)
    sa = jnp.concatenate([-sin, jnp.zeros((t, HEAD_DIM - half), F32)], axis=1)
    sb = jnp.concatenate([jnp.zeros((t, half), F32), sin, jnp.zeros((t, rest), F32)], axis=1)
    return c, sa, sb


def _rope_apply(v, c, sa, sb):
    half = ROT_DIM // 2
    return v * c + pltpu.roll(v, HEAD_DIM - half, axis=1) * sa + pltpu.roll(v, half, axis=1) * sb


def _rope_apply_t(dv, c, sa, sb):
    half = ROT_DIM // 2
    return dv * c + pltpu.roll(dv * sa, half, axis=1) + pltpu.roll(dv * sb, HEAD_DIM - half, axis=1)


def _in_proj(h, win_t, tabs, cfg, *, tm, tn, name):
    m, kdim = h.shape
    n = win_t.shape[0]
    per_tile = tn // HEAD_DIM
    n_rot = cfg.hq + cfg.g
    rot_tiles = -(-n_rot // per_tile)

    def body(h_ref, w_ref, c_ref, sa_ref, sb_ref, z_ref):
        j = pl.program_id(1)
        acc = _dot(h_ref[...], w_ref[...], "nt")

        @pl.when(j < rot_tiles)
        def _():
            for b in range(per_tile):
                sl = slice(b * HEAD_DIM, (b + 1) * HEAD_DIM)
                rot = _rope_apply(acc[:, sl], c_ref[...], sa_ref[...], sb_ref[...])
                z_ref[:, sl] = jnp.where(j * per_tile + b < n_rot, rot, acc[:, sl]).astype(BF16)

        @pl.when(j >= rot_tiles)
        def _():
            z_ref[...] = acc.astype(BF16)

    tab = pl.BlockSpec((tm, HEAD_DIM), lambda i, j: (i, 0))
    return pl.pallas_call(
        body,
        grid=(m // tm, n // tn),
        in_specs=[pl.BlockSpec((tm, kdim), lambda i, j: (i, 0)), pl.BlockSpec((tn, kdim), lambda i, j: (j, 0)),
                  tab, tab, tab],
        out_specs=pl.BlockSpec((tm, tn), lambda i, j: (i, j)),
        out_shape=jax.ShapeDtypeStruct((m, n), BF16),
        compiler_params=_cparams("parallel", "parallel"),
        name=name,
    )(h, win_t, *tabs)


def _attn_specs(cfg, nb):
    kcol = cfg.hq
    vcol = cfg.off_v // HEAD_DIM
    qw = Q_PER_KV * HEAD_DIM
    q_spec = pl.BlockSpec((BLOCK, qw), lambda g, n: (n, g))

    def kv(col, shift):
        def idx(g, n):
            return (jnp.clip(n + shift, 0, nb - 1), col + g)
        return pl.BlockSpec((BLOCK, HEAD_DIM), idx)

    k_specs = [kv(kcol, s) for s in (-1, 0, 1)]
    v_specs = [kv(vcol, s) for s in (-1, 0, 1)]
    return q_spec, k_specs, v_specs


def _attn_probs(q, k, valid, sk):
    scale = 1.0 / math.sqrt(HEAD_DIM)
    s = _dot(q, k, "nt") * scale
    s = jnp.where(valid, s, jnp.finfo(F32).min)
    m = jnp.maximum(jnp.max(s, axis=1, keepdims=True), sk)
    e = jnp.exp(s - m)
    es = jnp.exp(sk - m)
    inv = 1.0 / (jnp.sum(e, axis=1, keepdims=True) + es)
    return e * inv, es * inv


def _attn_valid(n, t):
    shape = (Q_PER_KV * BLOCK, 3 * BLOCK)
    qpos = n * BLOCK + (lax.broadcasted_iota(jnp.int32, shape, 0) & (BLOCK - 1))
    kpos = (n - 1) * BLOCK + lax.broadcasted_iota(jnp.int32, shape, 1)
    return (kpos >= 0) & (kpos < t) & (jnp.abs(qpos - kpos) <= WINDOW)


def _stack_heads(ref):
    return jnp.concatenate([ref[:, r * HEAD_DIM:(r + 1) * HEAD_DIM] for r in range(Q_PER_KV)], axis=0)


def _sink_column(sink_ref, g):
    head = lax.broadcasted_iota(jnp.int32, (Q_PER_KV * BLOCK, 1), 0) // BLOCK
    col = jnp.full((Q_PER_KV * BLOCK, 1), sink_ref[g * Q_PER_KV], F32)
    for r in range(1, Q_PER_KV):
        col = jnp.where(head == r, sink_ref[g * Q_PER_KV + r], col)
    return col


def _attn_fwd(qk, z, sink, cfg, *, name):
    t = qk.shape[0]
    nb = t // BLOCK
    q_spec, k_specs, v_specs = _attn_specs(cfg, nb)

    def body(sink_ref, q_ref, kp, kc, kn, vp, vc, vn, o_ref):
        g = pl.program_id(0)
        n = pl.program_id(1)
        k = jnp.concatenate([kp[...], kc[...], kn[...]], axis=0)
        v = jnp.concatenate([vp[...], vc[...], vn[...]], axis=0)
        p, _ = _attn_probs(_stack_heads(q_ref), k, _attn_valid(n, t), _sink_column(sink_ref, g))
        o = _dot(p.astype(BF16), v, "nn")
        for r in range(Q_PER_KV):
            o_ref[:, r * HEAD_DIM:(r + 1) * HEAD_DIM] = o[r * BLOCK:(r + 1) * BLOCK].astype(BF16)

    return pl.pallas_call(
        body,
        grid=(cfg.g, nb),
        in_specs=[pl.BlockSpec(memory_space=pltpu.SMEM), q_spec] + k_specs + v_specs,
        out_specs=q_spec,
        out_shape=jax.ShapeDtypeStruct((t, cfg.d), BF16),
        compiler_params=_cparams("parallel", "parallel"),
        name=name,
    )(sink, qk, qk, qk, qk, z, z, z)


def _attn_bwd(qk, z, dmix, sink, tabs, cfg, *, name):
    t = qk.shape[0]
    nb = t // BLOCK
    q_spec, k_specs, v_specs = _attn_specs(cfg, nb)
    tab = pl.BlockSpec((BLOCK, HEAD_DIM), lambda g, n: (n, 0))
    acc_spec = pl.BlockSpec((None, t + 2 * KV_PAD, HEAD_DIM), lambda g, n: (g, 0, 0))
    scale = 1.0 / math.sqrt(HEAD_DIM)

    def body(sink_ref, q_ref, kp, kc, kn, vp, vc, vn, do_ref, c_ref, sa_ref, sb_ref,
             dq_ref, dk_ref, dv_ref, ds_ref):
        g = pl.program_id(0)
        n = pl.program_id(1)

        @pl.when(n == 0)
        def _():
            dk_ref[...] = jnp.zeros_like(dk_ref)
            dv_ref[...] = jnp.zeros_like(dv_ref)
            ds_ref[...] = jnp.zeros_like(ds_ref)

        k = jnp.concatenate([kp[...], kc[...], kn[...]], axis=0)
        v = jnp.concatenate([vp[...], vc[...], vn[...]], axis=0)
        q = _stack_heads(q_ref)
        do = _stack_heads(do_ref)
        p, ps = _attn_probs(q, k, _attn_valid(n, t), _sink_column(sink_ref, g))
        dp = _dot(do, v, "nt")
        delta = jnp.sum(p * dp, axis=1, keepdims=True)
        dsc = (p * (dp - delta) * scale).astype(BF16)
        dq = _dot(dsc, k, "nn")
        dsink = ps * delta
        for r in range(Q_PER_KV):
            rows = slice(r * BLOCK, (r + 1) * BLOCK)
            dq_ref[:, r * HEAD_DIM:(r + 1) * HEAD_DIM] = _rope_apply_t(
                dq[rows], c_ref[...], sa_ref[...], sb_ref[...]).astype(BF16)
            ds_ref[r:r + 1, :] -= jnp.broadcast_to(jnp.sum(dsink[rows], axis=0, keepdims=True), (1, HEAD_DIM))
        rows = pl.ds(pl.multiple_of(n * BLOCK + (KV_PAD - BLOCK), BLOCK), 3 * BLOCK)
        dk_ref[rows, :] += _dot(dsc, q, "tn")
        dv_ref[rows, :] += _dot(p.astype(BF16), do, "tn")

    acc_shape = jax.ShapeDtypeStruct((cfg.g, t + 2 * KV_PAD, HEAD_DIM), F32)
    return pl.pallas_call(
        body,
        grid=(cfg.g, nb),
        in_specs=[pl.BlockSpec(memory_space=pltpu.SMEM), q_spec] + k_specs + v_specs + [q_spec, tab, tab, tab],
        out_specs=[q_spec, acc_spec, acc_spec, pl.BlockSpec((None, 8, HEAD_DIM), lambda g, n: (g, 0, 0))],
        out_shape=[jax.ShapeDtypeStruct((t, cfg.attn), BF16), acc_shape, acc_shape,
                   jax.ShapeDtypeStruct((cfg.g, 8, HEAD_DIM), F32)],
        compiler_params=_cparams("arbitrary", "arbitrary"),
        name=name,
    )(sink, qk, qk, qk, qk, z, z, z, dmix, *tabs)


def _kv_finish(dk_acc, dv_acc, tabs, cfg, *, name):
    t = dk_acc.shape[1] - 2 * KV_PAD

    def body(dk_ref, dv_ref, c_ref, sa_ref, sb_ref, ok_ref, ov_ref):
        ok_ref[...] = _rope_apply_t(dk_ref[...], c_ref[...], sa_ref[...], sb_ref[...]).astype(BF16)
        ov_ref[...] = dv_ref[...].astype(BF16)

    acc = pl.BlockSpec((None, KV_PAD, HEAD_DIM), lambda g, i: (g, i + 1, 0))
    tab = pl.BlockSpec((KV_PAD, HEAD_DIM), lambda g, i: (i, 0))
    out = pl.BlockSpec((KV_PAD, HEAD_DIM), lambda g, i: (i, g))
    return pl.pallas_call(
        body,
        grid=(cfg.g, t // KV_PAD),
        in_specs=[acc, acc, tab, tab, tab],
        out_specs=[out, out],
        out_shape=[jax.ShapeDtypeStruct((t, cfg.kv), BF16)] * 2,
        compiler_params=_cparams("parallel", "parallel"),
        name=name,
    )(dk_acc, dv_acc, *tabs)


def _halo_specs(width, col, tb, t):
    per = tb // HALO_ROWS
    last = t // HALO_ROWS - 1
    prev = pl.BlockSpec((HALO_ROWS, width), lambda i: (jnp.maximum(i * per - 1, 0), col))
    cur = pl.BlockSpec((tb, width), lambda i: (i, col))
    nxt = pl.BlockSpec((HALO_ROWS, width), lambda i: (jnp.minimum((i + 1) * per, last), col))
    return [prev, cur, nxt]


def _halo_load(refs):
    return jnp.concatenate([r[...].astype(F32) for r in refs], axis=0)


F32_SUBLANES = 8


CONV_ROWS = 64
LANES = 128


def _store_phases(ph_ref, v):
    total = v.shape[0]
    ph_ref[0] = v
    for p in range(1, F32_SUBLANES):
        ph_ref[p] = pltpu.roll(v, total - p, axis=0)


def _shifted(ph_ref, start, r0, lanes):
    base = start - start % F32_SUBLANES + r0
    return ph_ref[start % F32_SUBLANES, base:base + CONV_ROWS, lanes]


def _conv_blocks(tb, cw):
    return [(r0, slice(l0, l0 + LANES)) for l0 in range(0, cw, LANES) for r0 in range(0, tb, CONV_ROWS)]


def _conv_glu(a_refs, g_refs, i, tb, t):
    a = _halo_load(a_refs)
    g = _halo_load(g_refs)
    rows = i * tb - HALO_ROWS + lax.broadcasted_iota(jnp.int32, (tb + 2 * HALO_ROWS, 1), 0)
    valid = (rows >= 0) & (rows < t)
    sg = _sigmoid(g)
    return a, sg, jnp.where(valid, a * sg, 0.0), valid


def _conv_fwd(z, mix, w, b, lg, lb, cfg, *, tb, name):
    t = z.shape[0]
    cw = cfg.conv
    vec = pl.BlockSpec((1, cw), lambda i: (0, 0))

    def body(ap, ac, an, gp, gc, gn, w_ref, b_ref, lg_ref, lb_ref, mix_ref, o_ref, y_ref, c_ph):
        _, _, c, _ = _conv_glu((ap, ac, an), (gp, gc, gn), pl.program_id(0), tb, t)
        _store_phases(c_ph, c)
        for r0, lanes in _conv_blocks(tb, cw):
            acc = jnp.zeros((CONV_ROWS, LANES), F32)
            for j in range(CONV_KERNEL):
                acc = acc + w_ref[j:j + 1, lanes] * _shifted(c_ph, j + HALO_ROWS - CONV_PAD, r0, lanes)
            y_ref[r0:r0 + CONV_ROWS, lanes] = acc + b_ref[:, lanes]
        y = y_ref[...]
        mu = jnp.mean(y, axis=-1, keepdims=True)
        dlt = y - mu
        var = jnp.mean(dlt * dlt, axis=-1, keepdims=True)
        yn = dlt * lax.rsqrt(var + EPS) * lg_ref[...] + lb_ref[...]
        o_ref[...] = (yn * _sigmoid(yn)).astype(BF16)

    return pl.pallas_call(
        body,
        grid=(t // tb,),
        in_specs=(_halo_specs(cw, cfg.off_ca // cw, tb, t) + _halo_specs(cw, cfg.off_cg // cw, tb, t)
                  + [pl.BlockSpec((CONV_KERNEL + 1, cw), lambda i: (0, 0)), vec, vec, vec,
                     pl.BlockSpec(memory_space=pl.ANY)]),
        out_specs=[pl.BlockSpec((tb, cw), lambda i: (i, cfg.attn // cw)), pl.BlockSpec((tb, cw), lambda i: (i, 0))],
        out_shape=[jax.ShapeDtypeStruct(mix.shape, BF16), jax.ShapeDtypeStruct((t, cw), F32)],
        input_output_aliases={10: 0},
        scratch_shapes=[pltpu.VMEM((F32_SUBLANES, tb + 2 * HALO_ROWS, cw), F32)],
        compiler_params=_cparams("parallel"),
        name=name,
    )(z, z, z, z, z, z, w, b, lg, lb, mix)


def _conv_bwd(z, y, dmix, w, lg, lb, cfg, *, tb, name):
    t = z.shape[0]
    cw = cfg.conv
    vec = pl.BlockSpec((1, cw), lambda i: (0, 0))
    cen = slice(HALO_ROWS, HALO_ROWS + tb)

    def body(ap, ac, an, gp, gc, gn, yp, yc, yn_, dp, dc_, dn, w_ref, lg_ref, lb_ref,
             da_ref, dg_ref, dw_ref, st_ref, dy_ph, c_ph):
        @pl.when(pl.program_id(0) == 0)
        def _():
            dw_ref[...] = jnp.zeros_like(dw_ref)
            st_ref[...] = jnp.zeros_like(st_ref)

        a, sg, c, valid = _conv_glu((ap, ac, an), (gp, gc, gn), pl.program_id(0), tb, t)
        yv = _halo_load((yp, yc, yn_))
        do = _halo_load((dp, dc_, dn))
        mu = jnp.mean(yv, axis=-1, keepdims=True)
        dlt = yv - mu
        rstd = lax.rsqrt(jnp.mean(dlt * dlt, axis=-1, keepdims=True) + EPS)
        xh = dlt * rstd
        lgv = lg_ref[...]
        yn = xh * lgv + lb_ref[...]
        s = _sigmoid(yn)
        dyn = do * (s * (1.0 + yn * (1.0 - s)))
        dxh = dyn * lgv
        dy = rstd * (dxh - jnp.mean(dxh, axis=-1, keepdims=True)
                     - xh * jnp.mean(dxh * xh, axis=-1, keepdims=True))
        dy = jnp.where(valid, dy, 0.0)
        dyc = dy[cen]
        st_ref[0:1, :] += jnp.sum(dyc, axis=0, keepdims=True)
        st_ref[1:2, :] += jnp.sum((dyn * xh)[cen], axis=0, keepdims=True)
        st_ref[2:3, :] += jnp.sum(dyn[cen], axis=0, keepdims=True)
        _store_phases(dy_ph, dy)
        _store_phases(c_ph, c)
        for r0, lanes in _conv_blocks(tb, cw):
            dc = jnp.zeros((CONV_ROWS, LANES), F32)
            for j in range(CONV_KERNEL):
                dc = dc + w_ref[j:j + 1, lanes] * _shifted(dy_ph, HALO_ROWS + CONV_PAD - j, r0, lanes)
            rows = slice(HALO_ROWS + r0, HALO_ROWS + r0 + CONV_ROWS)
            sgc = sg[rows, lanes]
            da_ref[r0:r0 + CONV_ROWS, lanes] = (dc * sgc).astype(BF16)
            dg_ref[r0:r0 + CONV_ROWS, lanes] = (dc * a[rows, lanes] * sgc * (1.0 - sgc)).astype(BF16)
        for l0 in range(0, cw, LANES):
            lanes = slice(l0, l0 + LANES)
            for j in range(CONV_KERNEL):
                part = jnp.zeros((CONV_ROWS, LANES), F32)
                for r0 in range(0, tb, CONV_ROWS):
                    part = part + (_shifted(dy_ph, HALO_ROWS, r0, lanes)
                                   * _shifted(c_ph, j + HALO_ROWS - CONV_PAD, r0, lanes))
                dw_ref[j:j + 1, lanes] += jnp.sum(part, axis=0, keepdims=True)

    out = pl.BlockSpec((tb, cw), lambda i: (i, 0))
    wspec = pl.BlockSpec((CONV_KERNEL + 1, cw), lambda i: (0, 0))
    return pl.pallas_call(
        body,
        grid=(t // tb,),
        in_specs=(_halo_specs(cw, cfg.off_ca // cw, tb, t) + _halo_specs(cw, cfg.off_cg // cw, tb, t)
                  + _halo_specs(cw, 0, tb, t) + _halo_specs(cw, cfg.attn // cw, tb, t) + [wspec, vec, vec]),
        out_specs=[out, out, wspec, pl.BlockSpec((8, cw), lambda i: (0, 0))],
        out_shape=[jax.ShapeDtypeStruct((t, cw), BF16), jax.ShapeDtypeStruct((t, cw), BF16),
                   jax.ShapeDtypeStruct((CONV_KERNEL + 1, cw), F32), jax.ShapeDtypeStruct((8, cw), F32)],
        scratch_shapes=[pltpu.VMEM((F32_SUBLANES, tb + 2 * HALO_ROWS, cw), F32)] * 2,
        compiler_params=_cparams("arbitrary"),
        name=name,
    )(z, z, z, z, z, z, y, y, y, dmix, dmix, dmix, w, lg, lb)


_SQRT_HALF = 1.0 / math.sqrt(2.0)
_INV_SQRT_2PI = 1.0 / math.sqrt(2.0 * math.pi)


def _gelu(v):
    return 0.5 * v * (1.0 + lax.erf(v * _SQRT_HALF))


def _gelu_grad(v):
    return 0.5 * (1.0 + lax.erf(v * _SQRT_HALF)) + v * jnp.exp(-0.5 * v * v) * _INV_SQRT_2PI


def _sgu_norm(zv_ref, lg_ref, lb_ref):
    xv = zv_ref[...].astype(F32)
    v = _gelu(xv)
    mu = jnp.mean(v, axis=-1, keepdims=True)
    dlt = v - mu
    rstd = lax.rsqrt(jnp.mean(dlt * dlt, axis=-1, keepdims=True) + EPS)
    xh = dlt * rstd
    return xv, xh, rstd, xh * lg_ref[...] + lb_ref[...]


def _sgu_specs(cfg):
    sw = cfg.sgu
    zu = pl.BlockSpec((CHUNK, sw), lambda i: (i, cfg.off_u // sw))
    zv = pl.BlockSpec((CHUNK, sw), lambda i: (i, cfg.off_sv // sw))
    vec = pl.BlockSpec((1, sw), lambda i: (0, 0))
    ws = pl.BlockSpec((cfg.sh, CHUNK, CHUNK), lambda i: (0, 0, 0))
    bs = pl.BlockSpec((cfg.sh, CHUNK, 1), lambda i: (0, 0, 0))
    return zu, zv, vec, ws, bs


def _sgu_fwd(z, mix, lg, lb, ws, bs, cfg, *, name):
    t = z.shape[0]
    sw = cfg.sgu
    zu, zv, vec, wspec, bspec = _sgu_specs(cfg)

    def body(zu_ref, zv_ref, lg_ref, lb_ref, ws_ref, bs_ref, mix_ref, o_ref):
        u = _gelu(zu_ref[...].astype(F32))
        _, _, _, vn = _sgu_norm(zv_ref, lg_ref, lb_ref)
        vnb = vn.astype(BF16)
        for h in range(cfg.sh):
            sl = slice(h * HEAD_DIM, (h + 1) * HEAD_DIM)
            sp = _dot(ws_ref[h], vnb[:, sl], "nn") + bs_ref[h]
            o_ref[:, sl] = (u[:, sl] * sp).astype(BF16)

    return pl.pallas_call(
        body,
        grid=(t // CHUNK,),
        in_specs=[zu, zv, vec, vec, wspec, bspec, pl.BlockSpec(memory_space=pl.ANY)],
        out_specs=pl.BlockSpec((CHUNK, sw), lambda i: (i, (cfg.attn + cfg.conv) // sw)),
        out_shape=jax.ShapeDtypeStruct(mix.shape, BF16),
        input_output_aliases={6: 0},
        compiler_params=_cparams("parallel"),
        name=name,
    )(z, z, lg, lb, ws, bs, mix)


def _sgu_bwd(z, dmix, lg, lb, ws, bs, cfg, *, name):
    t = z.shape[0]
    sw = cfg.sgu
    zu, zv, vec, wspec, bspec = _sgu_specs(cfg)
    do_spec = pl.BlockSpec((CHUNK, sw), lambda i: (i, (cfg.attn + cfg.conv) // sw))

    def body(zu_ref, zv_ref, do_ref, lg_ref, lb_ref, ws_ref, bs_ref, duv_ref, dws_ref, dbs_ref, st_ref):
        @pl.when(pl.program_id(0) == 0)
        def _():
            dws_ref[...] = jnp.zeros_like(dws_ref)
            dbs_ref[...] = jnp.zeros_like(dbs_ref)
            st_ref[...] = jnp.zeros_like(st_ref)

        xu = zu_ref[...].astype(F32)
        u = _gelu(xu)
        xv, xh, rstd, vn = _sgu_norm(zv_ref, lg_ref, lb_ref)
        vnb = vn.astype(BF16)
        do = do_ref[...].astype(F32)
        dvn_parts = []
        for h in range(cfg.sh):
            sl = slice(h * HEAD_DIM, (h + 1) * HEAD_DIM)
            wh = ws_ref[h]
            sp = _dot(wh, vnb[:, sl], "nn") + bs_ref[h]
            dsp = do[:, sl] * u[:, sl]
            dspb = dsp.astype(BF16)
            dvn_parts.append(_dot(wh, dspb, "tn"))
            dws_ref[h] += _dot(dspb, vnb[:, sl], "nt")
            dbs_ref[h] += jnp.sum(dsp, axis=1, keepdims=True)
            duv_ref[:, sl] = (do[:, sl] * sp * _gelu_grad(xu[:, sl])).astype(BF16)
        dvn = jnp.concatenate(dvn_parts, axis=1)
        st_ref[0:1, :] += jnp.sum(dvn * xh, axis=0, keepdims=True)
        st_ref[1:2, :] += jnp.sum(dvn, axis=0, keepdims=True)
        dxh = dvn * lg_ref[...]
        dv = rstd * (dxh - jnp.mean(dxh, axis=-1, keepdims=True)
                     - xh * jnp.mean(dxh * xh, axis=-1, keepdims=True))
        duv_ref[:, sw:] = (dv * _gelu_grad(xv)).astype(BF16)

    return pl.pallas_call(
        body,
        grid=(t // CHUNK,),
        in_specs=[zu, zv, do_spec, vec, vec, wspec, bspec],
        out_specs=[pl.BlockSpec((CHUNK, 2 * sw), lambda i: (i, 0)), wspec, bspec,
                   pl.BlockSpec((8, sw), lambda i: (0, 0))],
        out_shape=[jax.ShapeDtypeStruct((t, 2 * sw), BF16), jax.ShapeDtypeStruct((cfg.sh, CHUNK, CHUNK), F32),
                   jax.ShapeDtypeStruct((cfg.sh, CHUNK, 1), F32), jax.ShapeDtypeStruct((8, sw), F32)],
        compiler_params=_cparams("arbitrary"),
        name=name,
    )(z, z, dmix, lg, lb, ws, bs)


def _sum_shards(own, landed, *, name):
    _, r, c = own.shape
    my_id = _flat_id(_my_coords()).astype(jnp.int32).reshape(1)
    tr = _pick(r, (64, 32, 16))

    def body(me_ref, *refs):
        o_ref = refs[N_DEV]
        acc = refs[0][...].astype(F32)
        for p_ref in refs[1:N_DEV]:
            acc = acc + p_ref[...].astype(F32)
        o_ref[...] = acc

    def slab(p):
        return pl.BlockSpec((None, tr, c), lambda i, me: (me[0] ^ p, i, 0))

    return pl.pallas_call(
        body,
        grid_spec=pltpu.PrefetchScalarGridSpec(
            num_scalar_prefetch=1,
            grid=(r // tr,),
            in_specs=[slab(p) for p in range(N_DEV)],
            out_specs=pl.BlockSpec((tr, c), lambda i, me: (i, 0)),
        ),
        out_shape=jax.ShapeDtypeStruct((r, c), F32),
        compiler_params=_cparams("parallel"),
        name=name,
    )(my_id, own, *([landed] * (N_DEV - 1)))


def _adamw(w, g, m, v, *, name):
    r, c = w.shape
    tr = _pick(r, [p for p in (1024, 512, 256, 128, 64, 32, 16, 8) if p * c <= ADAMW_TILE_ELEMS])

    def body(w_ref, g_ref, m_ref, v_ref, d_ref, nm_ref, nv_ref):
        gv = g_ref[...]
        nm = ADAM_B1 * m_ref[...] + (1.0 - ADAM_B1) * gv
        nv = ADAM_B2 * v_ref[...] + (1.0 - ADAM_B2) * (gv * gv)
        m_hat = nm / (1.0 - ADAM_B1 ** ADAM_STEP)
        v_hat = nv / (1.0 - ADAM_B2 ** ADAM_STEP)
        d_ref[...] = -ADAM_LR * (m_hat / (jnp.sqrt(v_hat) + ADAM_EPS) + ADAM_WD * w_ref[...])
        nm_ref[...] = nm
        nv_ref[...] = nv

    blk = pl.BlockSpec((tr, c), lambda i: (i, 0))
    return pl.pallas_call(
        body,
        grid=(r // tr,),
        in_specs=[blk] * 4,
        out_specs=[blk] * 3,
        out_shape=[jax.ShapeDtypeStruct((r, c), F32)] * 3,
        compiler_params=_cparams("parallel"),
        name=name,
    )(w, g, m, v)


def _my_coords():
    return tuple(lax.axis_index(a) for a in MESH_AXES)


def _peer_coords(me, p):
    return tuple(1 - v if (p >> (2 - a)) & 1 else v for a, v in enumerate(me))


def _flat_id(coords):
    return 4 * coords[0] + 2 * coords[1] + coords[2]


def _exchange(arrs, *, scatter, name):
    na = len(arrs)

    def body(*refs):
        ins = refs[:na]
        outs = refs[na:2 * na]
        send_sems, recv_sems, local_sems = refs[2 * na:]
        me = _my_coords()
        my_id = _flat_id(me)

        local = []
        for k in range(na):
            src = ins[k].at[my_id] if scatter else ins[k]
            cp = pltpu.make_async_copy(src, outs[k].at[my_id], local_sems.at[k])
            cp.start()
            local.append(cp)

        def remote(p, k):
            peer = _peer_coords(me, p)
            peer_id = _flat_id(peer)
            sem = (p - 1) * na + k
            src = ins[k].at[peer_id] if scatter else ins[k]
            send = pltpu.make_async_remote_copy(
                src_ref=src, dst_ref=outs[k].at[my_id], send_sem=send_sems.at[sem],
                recv_sem=recv_sems.at[sem], device_id=peer, device_id_type=MESH_ID)
            recv = pltpu.make_async_remote_copy(
                src_ref=src, dst_ref=outs[k].at[peer_id], send_sem=send_sems.at[sem],
                recv_sem=recv_sems.at[sem], device_id=peer, device_id_type=MESH_ID)
            return send, recv

        pairs = [remote(p, k) for p in range(1, N_DEV) for k in range(na)]
        for send, _ in pairs:
            send.start()
        for _, recv in pairs:
            recv.wait_recv()
        for send, _ in pairs:
            send.wait_send()
        for cp in local:
            cp.wait()

    def out_of(a):
        return jax.ShapeDtypeStruct(a.shape if scatter else (N_DEV,) + a.shape, a.dtype)

    hbm = pl.BlockSpec(memory_space=pl.ANY)
    nsem = (N_DEV - 1) * na
    return pl.pallas_call(
        body,
        in_specs=[hbm] * na,
        out_specs=[hbm] * na,
        out_shape=[out_of(a) for a in arrs],
        scratch_shapes=[pltpu.SemaphoreType.DMA((nsem,)), pltpu.SemaphoreType.DMA((nsem,)),
                        pltpu.SemaphoreType.DMA((na,))],
        name=name,
    )(*arrs)


_HBM = pl.BlockSpec(memory_space=pltpu.HBM)
_SEM = pl.BlockSpec(memory_space=pltpu.SEMAPHORE)
_EFFECT = pltpu.SideEffectType.DATAFLOW_SIDE_EFFECTING


def _place_own(land, src, *, scatter, name):
    _, r, c = land.shape
    tr = _pick(r, (256, 128, 64, 32, 16))
    my_id = _flat_id(_my_coords()).astype(jnp.int32).reshape(1)

    def body(me_ref, land_ref, src_ref, out_ref):
        out_ref[...] = src_ref[...]

    if scatter:
        src_spec = pl.BlockSpec((None, tr, c), lambda i, me: (me[0], i, 0))
    else:
        src_spec = pl.BlockSpec((tr, c), lambda i, me: (i, 0))
    return pl.pallas_call(
        body,
        grid_spec=pltpu.PrefetchScalarGridSpec(
            num_scalar_prefetch=1,
            grid=(r // tr,),
            in_specs=[pl.BlockSpec(memory_space=pl.ANY), src_spec],
            out_specs=pl.BlockSpec((None, tr, c), lambda i, me: (me[0], i, 0)),
        ),
        out_shape=jax.ShapeDtypeStruct(land.shape, land.dtype),
        input_output_aliases={1: 0},
        compiler_params=_cparams("parallel"),
        name=name,
    )(my_id, land, src)


def _peer_copy(src_ref, land_ref, send_sems, recv_sems, me, p, scatter, arrival):
    peer = _peer_coords(me, p)
    peer_id = _flat_id(peer)
    return pltpu.make_async_remote_copy(
        src_ref=src_ref.at[peer_id] if scatter else src_ref,
        dst_ref=land_ref.at[peer_id if arrival else _flat_id(me)],
        send_sem=send_sems.at[p - 1], recv_sem=recv_sems.at[p - 1], device_id=peer, device_id_type=MESH_ID)


def _exchange_start(src, land, *, scatter, name):
    def body(src_ref, land_ref, send_sems, recv_sems, src_thru, land_thru, token):
        me = _my_coords()
        for p in range(1, N_DEV):
            _peer_copy(src_ref, land_ref, send_sems, recv_sems, me, p, scatter, False).start()
        token[...] = jnp.zeros_like(token)

    nsem = N_DEV - 1
    return pl.pallas_call(
        body,
        name=name,
        out_shape=(pltpu.SemaphoreType.DMA((nsem,)), pltpu.SemaphoreType.DMA((nsem,)),
                   pltpu.HBM(src.shape, src.dtype), pltpu.HBM(land.shape, land.dtype),
                   jax.ShapeDtypeStruct((8, 128), F32)),
        in_specs=(_HBM, _HBM),
        out_specs=(_SEM, _SEM, _HBM, _HBM, pl.BlockSpec(memory_space=pltpu.VMEM)),
        input_output_aliases={0: 2, 1: 3},
        compiler_params=pltpu.CompilerParams(has_side_effects=_EFFECT),
    )(pltpu.with_memory_space_constraint(src, pltpu.HBM), pltpu.with_memory_space_constraint(land, pltpu.HBM))


def _exchange_wait(handle, after, *, scatter, name):
    send_sems, recv_sems, src_thru, land_thru = handle

    def body(src_ref, land_ref, send_sems, recv_sems, after_ref, src_dead, got_ref):
        me = _my_coords()
        for p in range(1, N_DEV):
            _peer_copy(src_ref, land_ref, send_sems, recv_sems, me, p, scatter, False).wait_send()
            _peer_copy(src_ref, land_ref, send_sems, recv_sems, me, p, scatter, True).wait_recv()

    return pl.pallas_call(
        body,
        name=name,
        out_shape=(pltpu.HBM(src_thru.shape, src_thru.dtype), pltpu.HBM(land_thru.shape, land_thru.dtype)),
        in_specs=(_HBM, _HBM, _SEM, _SEM, pl.BlockSpec(memory_space=pl.ANY)),
        out_specs=(_HBM, _HBM),
        input_output_aliases={0: 0, 1: 1},
        compiler_params=pltpu.CompilerParams(has_side_effects=_EFFECT),
    )(src_thru, land_thru, send_sems, recv_sems, after)


def _exchange_begin(src, *, scatter, name):
    if scatter:
        land = lax.empty(src.shape, src.dtype)
    else:
        land = _place_own(lax.empty((N_DEV,) + src.shape, src.dtype), src, scatter=False, name=name + "_own")
    *handle, token = _exchange_start(src, land, scatter=scatter, name=name + "_start")
    return tuple(handle), token[0, 0]


_SIBLING_MASK = 1
_CHIP_MASKS = (2, 4, 6)
_DIRECT_MASKS = (_SIBLING_MASK,) + _CHIP_MASKS


def _direct_copy(src_ref, land_ref, send_sems, recv_sems, me, a, j, arrival):
    p = _DIRECT_MASKS[j]
    peer = _peer_coords(me, p)
    sem = a * len(_DIRECT_MASKS) + j
    return pltpu.make_async_remote_copy(
        src_ref=src_ref, dst_ref=land_ref.at[_flat_id(peer) if arrival else _flat_id(me)],
        send_sem=send_sems.at[sem], recv_sem=recv_sems.at[sem], device_id=peer, device_id_type=MESH_ID)


def _relay_copy(land_ref, send_sems, recv_sems, me, a, j, arrival):
    sibling = _peer_coords(me, _SIBLING_MASK)
    holder = sibling if arrival else me
    slab = land_ref.at[_flat_id(_peer_coords(holder, _CHIP_MASKS[j]))]
    sem = a * len(_CHIP_MASKS) + j
    return pltpu.make_async_remote_copy(
        src_ref=slab, dst_ref=slab, send_sem=send_sems.at[sem], recv_sem=recv_sems.at[sem],
        device_id=sibling, device_id_type=MESH_ID)


def _hbm_like(arrs):
    return tuple(pltpu.HBM(a.shape, a.dtype) for a in arrs)


def _gather_direct_start(srcs, lands, *, name):
    k = len(srcs)

    def body(*refs):
        send_sems, recv_sems = refs[2 * k:2 * k + 2]
        me = _my_coords()
        for a in range(k):
            for j in range(len(_DIRECT_MASKS)):
                _direct_copy(refs[a], refs[k + a], send_sems, recv_sems, me, a, j, False).start()
        refs[-1][...] = jnp.zeros_like(refs[-1])

    nsem = k * len(_DIRECT_MASKS)
    hbm_in = [pltpu.with_memory_space_constraint(a, pltpu.HBM) for a in (*srcs, *lands)]
    return pl.pallas_call(
        body,
        name=name,
        out_shape=(pltpu.SemaphoreType.DMA((nsem,)), pltpu.SemaphoreType.DMA((nsem,)),
                   *_hbm_like(srcs), *_hbm_like(lands), jax.ShapeDtypeStruct((8, 128), F32)),
        in_specs=(_HBM,) * (2 * k),
        out_specs=(_SEM, _SEM) + (_HBM,) * (2 * k) + (pl.BlockSpec(memory_space=pltpu.VMEM),),
        input_output_aliases={i: 2 + i for i in range(2 * k)},
        compiler_params=pltpu.CompilerParams(has_side_effects=_EFFECT),
    )(*hbm_in)


def _gather_direct_wait(send_sems, recv_sems, srcs, lands, after, *, name):
    k = len(srcs)

    def body(*refs):
        send_sems, recv_sems = refs[2 * k:2 * k + 2]
        me = _my_coords()
        for a in range(k):
            for j in range(len(_DIRECT_MASKS)):
                _direct_copy(refs[a], refs[k + a], send_sems, recv_sems, me, a, j, False).wait_send()
                _direct_copy(refs[a], refs[k + a], send_sems, recv_sems, me, a, j, True).wait_recv()

    return pl.pallas_call(
        body,
        name=name,
        out_shape=(*_hbm_like(srcs), *_hbm_like(lands)),
        in_specs=(_HBM,) * (2 * k) + (_SEM, _SEM, pl.BlockSpec(memory_space=pl.ANY)),
        out_specs=(_HBM,) * (2 * k),
        input_output_aliases={i: i for i in range(2 * k)},
        compiler_params=pltpu.CompilerParams(has_side_effects=_EFFECT),
    )(*srcs, *lands, send_sems, recv_sems, after)[k:]


def _gather_relay_start(lands, *, name):
    k = len(lands)

    def body(*refs):
        send_sems, recv_sems = refs[k:k + 2]
        me = _my_coords()
        for a in range(k):
            for j in range(len(_CHIP_MASKS)):
                _relay_copy(refs[a], send_sems, recv_sems, me, a, j, False).start()
        refs[-1][...] = jnp.zeros_like(refs[-1])

    nsem = k * len(_CHIP_MASKS)
    return pl.pallas_call(
        body,
        name=name,
        out_shape=(pltpu.SemaphoreType.DMA((nsem,)), pltpu.SemaphoreType.DMA((nsem,)),
                   *_hbm_like(lands), jax.ShapeDtypeStruct((8, 128), F32)),
        in_specs=(_HBM,) * k,
        out_specs=(_SEM, _SEM) + (_HBM,) * k + (pl.BlockSpec(memory_space=pltpu.VMEM),),
        input_output_aliases={i: 2 + i for i in range(k)},
        compiler_params=pltpu.CompilerParams(has_side_effects=_EFFECT),
    )(*lands)


def _gather_relay_wait(send_sems, recv_sems, lands, after, *, name):
    k = len(lands)

    def body(*refs):
        send_sems, recv_sems = refs[k:k + 2]
        me = _my_coords()
        for a in range(k):
            for j in range(len(_CHIP_MASKS)):
                _relay_copy(refs[a], send_sems, recv_sems, me, a, j, False).wait_send()
                _relay_copy(refs[a], send_sems, recv_sems, me, a, j, True).wait_recv()

    return pl.pallas_call(
        body,
        name=name,
        out_shape=_hbm_like(lands),
        in_specs=(_HBM,) * k + (_SEM, _SEM, pl.BlockSpec(memory_space=pl.ANY)),
        out_specs=(_HBM,) * k,
        input_output_aliases={i: i for i in range(k)},
        compiler_params=pltpu.CompilerParams(has_side_effects=_EFFECT),
    )(*lands, send_sems, recv_sems, after)


def _gather_group_begin(srcs, *, name):
    lands = [_place_own(lax.empty((N_DEV,) + s.shape, s.dtype), s, scatter=False, name=f"{name}_own{i}")
             for i, s in enumerate(srcs)]
    k = len(srcs)
    out = _gather_direct_start(srcs, lands, name=name + "_start")
    return (out[0], out[1], out[2:2 + k], out[2 + k:2 + 2 * k]), out[-1][0, 0]


def _gather_group_relay(handle, after, *, name):
    lands = _gather_direct_wait(*handle, after, name=name + "_landed")
    out = _gather_relay_start(lands, name=name + "_relay")
    return (out[0], out[1], out[2:-1]), out[-1][0, 0]


def _gather_group_end(handle, after, *, name):
    return _gather_relay_wait(*handle, after, name=name + "_done")


def _allreduce_small(flat, *, name):
    r, c = flat.shape
    rc = r // N_DEV
    assert rc * N_DEV == r and rc % F32_SUBLANES == 0, r

    def body(in_ref, out_ref, buf, send1, recv1, send2, recv2):
        me = _my_coords()
        my_id = _flat_id(me)

        def rows(dev_id):
            return pl.ds(pl.multiple_of(dev_id * rc, F32_SUBLANES), rc)

        def exchange(copies):
            for send, _ in copies:
                send.start()
            for _, recv in copies:
                recv.wait_recv()
            for send, _ in copies:
                send.wait_send()

        def scatter_copy(p):
            peer = _peer_coords(me, p)
            common = dict(src_ref=in_ref.at[rows(_flat_id(peer))], send_sem=send1.at[p - 1],
                          recv_sem=recv1.at[p - 1], device_id=peer, device_id_type=MESH_ID)
            return (pltpu.make_async_remote_copy(dst_ref=buf.at[my_id], **common),
                    pltpu.make_async_remote_copy(dst_ref=buf.at[_flat_id(peer)], **common))

        def gather_copy(p):
            peer = _peer_coords(me, p)
            common = dict(src_ref=out_ref.at[rows(my_id)], send_sem=send2.at[p - 1],
                          recv_sem=recv2.at[p - 1], device_id=peer, device_id_type=MESH_ID)
            return (pltpu.make_async_remote_copy(dst_ref=out_ref.at[rows(my_id)], **common),
                    pltpu.make_async_remote_copy(dst_ref=out_ref.at[rows(_flat_id(peer))], **common))

        buf[my_id] = in_ref[rows(my_id), :]
        exchange([scatter_copy(p) for p in range(1, N_DEV)])
        acc = buf[0]
        for q in range(1, N_DEV):
            acc = acc + buf[q]
        out_ref[rows(my_id), :] = acc
        exchange([gather_copy(p) for p in range(1, N_DEV)])

    vmem = pl.BlockSpec(memory_space=pltpu.VMEM)
    return pl.pallas_call(
        body,
        in_specs=[vmem],
        out_specs=vmem,
        out_shape=jax.ShapeDtypeStruct((r, c), F32),
        scratch_shapes=[pltpu.VMEM((N_DEV, rc, c), F32)] + [pltpu.SemaphoreType.DMA((N_DEV - 1,))] * 4,
        compiler_params=pltpu.CompilerParams(vmem_limit_bytes=VMEM_LIMIT_BYTES),
        name=name,
    )(flat)


WEIGHT_NAMES = ("mix_norm_g", "w_in", "sink", "conv_dw_w", "conv_dw_b", "conv_ln_g", "conv_ln_b",
                "sgu_ln_g", "sgu_ln_b", "sgu_w", "sgu_b", "w_out", "ffn_norm_g", "w_gate", "w_up",
                "w_down", "final_norm_g")
SHARDED = ("w_in", "conv_dw_w", "w_out", "w_gate", "w_up", "w_down")
SMALL = tuple(n for n in WEIGHT_NAMES if n not in ("w_in", "w_out", "w_gate", "w_up", "w_down"))


def _pack_small(parts):
    flat = jnp.concatenate([parts[n].reshape(-1) for n in SMALL])
    pad = (-flat.shape[0]) % (N_DEV * F32_SUBLANES * LANES)
    return jnp.pad(flat, (0, pad)).reshape(-1, 128)


def _unpack_small(packed, shapes):
    flat = packed.reshape(-1)
    out, pos = {}, 0
    for n in SMALL:
        size = math.prod(shapes[n])
        out[n] = flat[pos:pos + size].reshape(shapes[n])
        pos += size
    return out


def kernel(x, mix_norm_g, w_in, sink, conv_dw_w, conv_dw_b, conv_ln_g, conv_ln_b, sgu_ln_g, sgu_ln_b, sgu_w, sgu_b, w_out, ffn_norm_g, w_gate, w_up, w_down, final_norm_g, loss_target, m_mix_norm_g, m_w_in, m_sink, m_conv_dw_w, m_conv_dw_b, m_conv_ln_g, m_conv_ln_b, m_sgu_ln_g, m_sgu_ln_b, m_sgu_w, m_sgu_b, m_w_out, m_ffn_norm_g, m_w_gate, m_w_up, m_w_down, m_final_norm_g, v_mix_norm_g, v_w_in, v_sink, v_conv_dw_w, v_conv_dw_b, v_conv_ln_g, v_conv_ln_b, v_sgu_ln_g, v_sgu_ln_b, v_sgu_w, v_sgu_b, v_w_out, v_ffn_norm_g, v_w_gate, v_w_up, v_w_down, v_final_norm_g):
    w = dict(mix_norm_g=mix_norm_g, w_in=w_in, sink=sink, conv_dw_w=conv_dw_w, conv_dw_b=conv_dw_b,
             conv_ln_g=conv_ln_g, conv_ln_b=conv_ln_b, sgu_ln_g=sgu_ln_g, sgu_ln_b=sgu_ln_b, sgu_w=sgu_w,
             sgu_b=sgu_b, w_out=w_out, ffn_norm_g=ffn_norm_g, w_gate=w_gate, w_up=w_up, w_down=w_down,
             final_norm_g=final_norm_g)
    mom_m = dict(zip(WEIGHT_NAMES, (m_mix_norm_g, m_w_in, m_sink, m_conv_dw_w, m_conv_dw_b, m_conv_ln_g,
                                    m_conv_ln_b, m_sgu_ln_g, m_sgu_ln_b, m_sgu_w, m_sgu_b, m_w_out,
                                    m_ffn_norm_g, m_w_gate, m_w_up, m_w_down, m_final_norm_g)))
    mom_v = dict(zip(WEIGHT_NAMES, (v_mix_norm_g, v_w_in, v_sink, v_conv_dw_w, v_conv_dw_b, v_conv_ln_g,
                                    v_conv_ln_b, v_sgu_ln_g, v_sgu_ln_b, v_sgu_w, v_sgu_b, v_w_out,
                                    v_ffn_norm_g, v_w_gate, v_w_up, v_w_down, v_final_norm_g)))

    _, t, d = x.shape
    depth = w_in.shape[0]
    cfg = Cfg(d, t)
    ff = w_gate.shape[2] * N_DEV
    my_id = _flat_id(_my_coords())
    xs = x[0]
    target = loss_target[0]

    tm = _pick(t, (1024, 512))
    tr = _pick(t, (256, 128))
    tb = _pick(t, (256, 128))
    tn_in = _pick(cfg.inw, (896, 512, 448))
    tn_ff = _pick(ff, (512, 1408, 704))
    tk_ff = ff
    tk_in = cfg.inw
    tn_d = _pick(d, (512,))
    tk_t = _pick(t, (2048, 1024, 512))
    tm_in = _pick(cfg.inw, (896, 448))
    tm_ff = _pick(ff, (1408, 704))
    tn_dw = _pick(d, (1024,))

    tabs = _rope_tables(t)

    cflat = conv_dw_w.reshape(-1)
    cshard = jnp.pad(cflat, (0, (-cflat.shape[0]) % (8 * 128))).reshape(-1, 128)
    dw_all = _exchange([cshard], scatter=False, name="gather_conv_w")[0]
    dw_all = dw_all.reshape(N_DEV, -1)[:, :cflat.shape[0]].reshape(N_DEV, depth, CONV_KERNEL, -1)
    dw_all = dw_all.transpose(1, 2, 0, 3).reshape(depth, CONV_KERNEL, cfg.conv)
    dw_pad = jnp.pad(dw_all, ((0, 0), (0, 1), (0, 0)))

    def row(v):
        return v.reshape(1, -1)

    first, rest = ("win",), ("wo", "wg", "wu", "wd")

    def gather_begin(l, names, zero):
        shards = dict(win=w_in[l].T, wo=w_out[l], wg=w_gate[l].T, wu=w_up[l].T, wd=w_down[l])
        handle, started = _gather_group_begin([(shards[k] + zero).astype(BF16) for k in names],
                                              name=f"gather_{names[0]}_{l}")
        return handle, zero + started

    def gather_end(handle, names, after, l):
        full = _gather_group_end(handle, after, name=f"gather_{names[0]}_{l}")
        return {k: f.reshape(-1, d) for k, f in zip(names, full)}

    saved = []
    scatters = [None] * depth
    dw_pad, started = lax.optimization_barrier((dw_pad, jnp.zeros((), F32)))
    landing_first, started = gather_begin(0, first, started)
    landing_rest, started = gather_begin(0, rest, started)
    relayed = None
    for l in range(depth):
        h = _rms_fwd(xs, row(mix_norm_g[l]) + started, tr=tr, name="mix_norm")
        if l == 0:
            relayed, _ = _gather_group_relay(landing_first, h, name="gather_win_0")
            wts = gather_end(relayed, first, h, 0)
        else:
            wts = gather_end(relayed, first + rest, h, l)
        win_t = wts["win"]
        started = jnp.zeros((), F32)
        if l + 1 < depth:
            win_t, started = lax.optimization_barrier((win_t, started))
            landing_next, started = gather_begin(l + 1, first + rest, started)
            h, started = lax.optimization_barrier((h, started))
        z = _in_proj(h, win_t, tabs, cfg, tm=tm, tn=tn_in, name="in_proj")
        qk = z
        mix = _attn_fwd(qk, z, sink[l], cfg, name="attn_fwd")
        mix, conv_y = _conv_fwd(z, mix, dw_pad[l], row(conv_dw_b[l]), row(conv_ln_g[l]) + started,
                                row(conv_ln_b[l]), cfg, tb=tb, name="conv_fwd")
        ws_b = sgu_w[l].astype(BF16)
        bs_c = sgu_b[l][:, :, None]
        mix = _sgu_fwd(z, mix, row(sgu_ln_g[l]), row(sgu_ln_b[l]), ws_b, bs_c, cfg, name="sgu_fwd")
        if l == 0:
            relayed, _ = _gather_group_relay(landing_rest, mix, name="gather_wo_0")
            wts.update(gather_end(relayed, rest, mix, 0))
        wo, wg_t, wu_t, wd = wts["wo"], wts["wg"], wts["wu"], wts["wd"]
        x1 = _matmul(mix, wo, mode="nn", tm=tm, tn=tn_d, tk=d, epilogue=_ep_add, extras=(xs,),
                     out_dtypes=[F32], name="out_proj")[0]
        h2 = _rms_fwd(x1, row(ffn_norm_g[l]), tr=tr, name="ffn_norm")
        gate, up, act = _ffn_up(h2, wg_t, wu_t, tm=tm, tn=tn_ff, name="ffn_up")
        if l + 1 < depth:
            relayed, zero = _gather_group_relay(landing_next, act, name=f"gather_win_{l + 1}")
            act, zero = lax.optimization_barrier((act, zero))
            started = started + zero
        x2 = _matmul(act, wd, mode="nn", tm=tm, tn=tn_d, tk=tk_ff, epilogue=_ep_add, extras=(x1,),
                     out_dtypes=[F32], name="ffn_down")[0]
        saved.append(dict(x0=xs, h=h, z=z, qk=qk, mix=mix, conv_y=conv_y, x1=x1, h2=h2, gate=gate, up=up,
                          act=act, win_t=win_t, wg_t=wg_t, wu_t=wu_t, wo=wo, wd=wd, ws_b=ws_b, bs_c=bs_c))
        xs = x2

    dx, dxb, head = _loss_head(xs, row(final_norm_g), target, tr=tr, name="loss_head")
    loss = lax.psum(head[1, 0], MESH_AXES)

    def scatter_begin(grad, n, l):
        handle, zero = _exchange_begin(grad.reshape(N_DEV, -1, d), scatter=True, name=f"scatter_{n}_{l}")
        scatters[l][n] = handle
        return zero

    small = {n: [None] * depth for n in SMALL if n != "final_norm_g"}
    big = {n: [None] * depth for n in ("w_in", "w_out", "w_gate", "w_up", "w_down")}
    for l in reversed(range(depth)):
        s = saved[l]
        dgate, dup = _matmul(dxb, s["wd"], mode="nt", tm=tm, tn=tn_ff, tk=d, epilogue=_ep_swiglu_bwd,
                             extras=(s["gate"], s["up"]), out_dtypes=[BF16, BF16], name="ffn_down_bwd")
        dwd = _matmul(s["act"], dxb, mode="tn", tm=tm_ff, tn=tn_dw, tk=tk_t, epilogue=_ep_plain,
                      out_dtypes=[BF16], name="ffn_down_wgrad")[0]
        scatters[l] = {}
        started = scatter_begin(dwd, "w_down", l)
        dh2 = _matmul(dgate, s["wg_t"], mode="nn", tm=tm, tn=tn_d, tk=tk_ff, epilogue=_ep_plain,
                      out_dtypes=[F32], name="ffn_gate_bwd")[0]
        dh2 = _matmul(dup, s["wu_t"], mode="nn", tm=tm, tn=tn_d, tk=tk_ff, epilogue=_ep_add, extras=(dh2,),
                      out_dtypes=[BF16], name="ffn_up_bwd")[0]
        dwg_t = _matmul(dgate, s["h2"], mode="tn", tm=tm_ff, tn=tn_dw, tk=tk_t, epilogue=_ep_plain,
                        out_dtypes=[BF16], name="ffn_gate_wgrad")[0]
        dwu_t = _matmul(dup, s["h2"], mode="tn", tm=tm_ff, tn=tn_dw, tk=tk_t, epilogue=_ep_plain,
                        out_dtypes=[BF16], name="ffn_up_wgrad")[0]
        started = started + scatter_begin(dwg_t, "w_gate", l) + scatter_begin(dwu_t, "w_up", l)
        dx1, dx1b, dg2 = _rms_bwd(dh2, s["x1"], row(ffn_norm_g[l]) + started, dx, tr=tr, name="ffn_norm_bwd")

        dmix = _matmul(dx1b, s["wo"], mode="nt", tm=tm, tn=tn_d, tk=d, epilogue=_ep_plain,
                       out_dtypes=[BF16], name="out_proj_bwd")[0]
        dwo = _matmul(s["mix"], dx1b, mode="tn", tm=_pick(d, (1024,)), tn=tn_dw, tk=tk_t, epilogue=_ep_plain,
                      out_dtypes=[BF16], name="out_proj_wgrad")[0]
        started = scatter_begin(dwo, "w_out", l)
        dq, dk_acc, dv_acc, dsink = _attn_bwd(s["qk"], s["z"], dmix, sink[l], tabs, cfg, name="attn_bwd")
        dk, dv = _kv_finish(dk_acc, dv_acc, tabs, cfg, name="attn_bwd_kv")
        da, dcg, dcw, cst = _conv_bwd(s["z"], s["conv_y"], dmix, dw_pad[l], row(conv_ln_g[l]) + started,
                                      row(conv_ln_b[l]), cfg, tb=tb, name="conv_bwd")
        duv, dws, dbs, sst = _sgu_bwd(s["z"], dmix, row(sgu_ln_g[l]), row(sgu_ln_b[l]), s["ws_b"], s["bs_c"],
                                      cfg, name="sgu_bwd")
        dz = jnp.concatenate([dq, dk, dv, da, dcg, duv], axis=1)
        dh = _matmul(dz, s["win_t"], mode="nn", tm=tm, tn=tn_d, tk=tk_in, epilogue=_ep_plain,
                     out_dtypes=[BF16], name="in_proj_bwd")[0]
        dwin_t = _matmul(dz, s["h"], mode="tn", tm=tm_in, tn=tn_dw, tk=tk_t, epilogue=_ep_plain,
                         out_dtypes=[BF16], name="in_proj_wgrad")[0]
        started = scatter_begin(dwin_t, "w_in", l)
        dx, dxb, dg1 = _rms_bwd(dh, s["x0"], row(mix_norm_g[l]) + started, dx1, tr=tr, name="mix_norm_bwd")

        small["mix_norm_g"][l] = dg1[0]
        small["ffn_norm_g"][l] = dg2[0]
        small["sink"][l] = dsink[:, :Q_PER_KV, 0].reshape(-1)
        small["conv_dw_w"][l] = dcw[:CONV_KERNEL]
        small["conv_dw_b"][l] = cst[0]
        small["conv_ln_g"][l] = cst[1]
        small["conv_ln_b"][l] = cst[2]
        small["sgu_ln_g"][l] = sst[0]
        small["sgu_ln_b"][l] = sst[1]
        small["sgu_w"][l] = dws
        small["sgu_b"][l] = dbs[:, :, 0]

    grads, deltas, new_m, new_v = {}, {}, {}, {}

    transposed = ("w_in", "w_gate", "w_up")

    def adamw(n, grad):
        flip = (lambda a: jnp.swapaxes(a, 1, 2)) if n in transposed else (lambda a: a)
        shape = flip(w[n]).shape
        view = lambda a: flip(a).reshape(-1, shape[-1])
        out = _adamw(view(w[n]), grad.reshape(-1, shape[-1]), view(mom_m[n]), view(mom_v[n]), name="adamw")
        grads[n] = flip(grad.reshape(shape))
        deltas[n], new_m[n], new_v[n] = [flip(o.reshape(shape)) for o in out]

    after = dx
    for n in ("w_down", "w_gate", "w_up", "w_out", "w_in"):
        for l in reversed(range(depth)):
            own, landed = _exchange_wait(scatters[l][n], after, scatter=True, name=f"scatter_{n}_{l}_wait")
            big[n][l] = _sum_shards(own, landed, name="sum_grads")
        adamw(n, jnp.stack(big[n]))
        after = new_v[n]

    parts = {n: jnp.stack(v) for n, v in small.items()}
    parts["final_norm_g"] = head[0]
    shapes = {n: parts[n].shape for n in SMALL}
    summed = _unpack_small(_allreduce_small(_pack_small(parts), name="allreduce_small"), shapes)
    cshard_w = conv_dw_w.shape[2]
    summed["conv_dw_w"] = lax.dynamic_slice_in_dim(summed["conv_dw_w"], my_id * cshard_w, cshard_w, axis=2)
    for n in SMALL:
        adamw(n, summed[n])

    return (loss, dx[None], *[grads[n] for n in WEIGHT_NAMES], *[deltas[n] for n in WEIGHT_NAMES],
            *[new_m[n] for n in WEIGHT_NAMES], *[new_v[n] for n in WEIGHT_NAMES])
```

```python
import functools
import math

import jax
import jax.numpy as jnp
from jax import lax
from jax.experimental import pallas as pl
from jax.experimental.pallas import tpu as pltpu

F32 = jnp.float32
BF16 = jnp.bfloat16

HEAD_DIM = 128
Q_PER_KV = 4
WINDOW = 128
BLOCK = 128
ROT_DIM = 32
ROPE_THETA = 500000.0
CONV_KERNEL = 31
CONV_PAD = (CONV_KERNEL - 1) // 2
CHUNK = 128
EPS = 1e-6

ADAM_LR = 0.001
ADAM_B1 = 0.9
ADAM_B2 = 0.999
ADAM_EPS = 1e-08
ADAM_WD = 0.01
ADAM_STEP = 10

N_DEV = 8
MESH_AXES = ("x", "y", "c")
VMEM_LIMIT_BYTES = 56 * 1024 * 1024
HALO_ROWS = 16
KV_PAD = 512
ADAMW_TILE_ELEMS = 256 * 1024
MESH_ID = pl.DeviceIdType.MESH


class Cfg:
    def __init__(self, d_model, seq):
        self.d = d_model
        self.t = seq
        self.attn = d_model // 2
        self.hq = self.attn // HEAD_DIM
        self.g = self.hq // Q_PER_KV
        self.kv = self.g * HEAD_DIM
        self.conv = d_model // 4
        self.sgu = d_model // 4
        self.sh = self.sgu // HEAD_DIM
        self.off_k = self.attn
        self.off_v = self.attn + self.kv
        self.off_ca = self.attn + 2 * self.kv
        self.off_cg = self.off_ca + self.conv
        self.off_u = self.off_cg + self.conv
        self.off_sv = self.off_u + self.sgu
        self.inw = self.off_sv + self.sgu


def _pick(dim, prefs):
    for p in prefs:
        if dim % p == 0:
            return p
    return dim


def _cparams(*sem):
    return pltpu.CompilerParams(dimension_semantics=sem, vmem_limit_bytes=VMEM_LIMIT_BYTES)


def _sigmoid(v):
    return 0.5 * jnp.tanh(0.5 * v) + 0.5


_DN = {
    "nn": (((1,), (0,)), ((), ())),
    "nt": (((1,), (1,)), ((), ())),
    "tn": (((0,), (0,)), ((), ())),
}


def _dot(a, b, mode):
    return lax.dot_general(a, b, _DN[mode], preferred_element_type=F32)


def _matmul(a, b, *, mode, tm, tn, tk, epilogue, out_dtypes, extras=(), name):
    if mode == "tn":
        kdim, m = a.shape
        n = b.shape[1]
    elif mode == "nn":
        m, kdim = a.shape
        n = b.shape[1]
    else:
        m, kdim = a.shape
        n = b.shape[0]
    assert m % tm == 0 and n % tn == 0 and kdim % tk == 0, (name, m, n, kdim, tm, tn, tk)
    gm, gn, gk = m // tm, n // tn, kdim // tk
    if mode == "tn":
        a_spec = pl.BlockSpec((tk, tm), lambda i, j, k: (k, i))
    else:
        a_spec = pl.BlockSpec((tm, tk), lambda i, j, k: (i, k))
    if mode == "nt":
        b_spec = pl.BlockSpec((tn, tk), lambda i, j, k: (j, k))
    else:
        b_spec = pl.BlockSpec((tk, tn), lambda i, j, k: (k, j))
    tile = pl.BlockSpec((tm, tn), lambda i, j, k: (i, j))
    ne, no = len(extras), len(out_dtypes)

    def body(a_ref, b_ref, *rest):
        ex = rest[:ne]
        outs = rest[ne:ne + no]

        def finish(acc):
            vals = epilogue(acc, *[e[...] for e in ex])
            for o_ref, val in zip(outs, vals):
                o_ref[...] = val.astype(o_ref.dtype)

        part = _dot(a_ref[...], b_ref[...], mode)
        if gk == 1:
            finish(part)
        else:
            acc_ref = rest[ne + no]
            k = pl.program_id(2)

            @pl.when(k == 0)
            def _():
                acc_ref[...] = part

            if gk > 2:
                @pl.when((k > 0) & (k < gk - 1))
                def _():
                    acc_ref[...] += part

            @pl.when(k == gk - 1)
            def _():
                finish(acc_ref[...] + part)

    return pl.pallas_call(
        body,
        grid=(gm, gn, gk),
        in_specs=[a_spec, b_spec] + [tile] * ne,
        out_specs=[tile] * no,
        out_shape=[jax.ShapeDtypeStruct((m, n), dt) for dt in out_dtypes],
        scratch_shapes=[pltpu.VMEM((tm, tn), F32)] if gk > 1 else [],
        compiler_params=_cparams("parallel", "parallel", "arbitrary"),
        name=name,
    )(a, b, *extras)


def _ep_plain(acc):
    return (acc,)


def _ep_add(acc, r):
    return (r.astype(F32) + acc,)


def _ep_swiglu_bwd(dact, gate, up):
    gate = gate.astype(F32)
    up = up.astype(F32)
    s = _sigmoid(gate)
    silu = gate * s
    dgate = dact * up * (s * (1.0 + gate * (1.0 - s)))
    dup = dact * silu
    return dgate, dup


def _ffn_up(h, wgt, wut, *, tm, tn, name):
    m, kdim = h.shape
    n = wgt.shape[0]

    def body(h_ref, g_ref, u_ref, gate_ref, up_ref, act_ref):
        hv = h_ref[...]
        gate = _dot(hv, g_ref[...], "nt")
        up = _dot(hv, u_ref[...], "nt")
        gate_ref[...] = gate.astype(BF16)
        up_ref[...] = up.astype(BF16)
        act_ref[...] = (gate * _sigmoid(gate) * up).astype(BF16)

    tile = pl.BlockSpec((tm, tn), lambda i, j: (i, j))
    wspec = pl.BlockSpec((tn, kdim), lambda i, j: (j, 0))
    return pl.pallas_call(
        body,
        grid=(m // tm, n // tn),
        in_specs=[pl.BlockSpec((tm, kdim), lambda i, j: (i, 0)), wspec, wspec],
        out_specs=[tile] * 3,
        out_shape=[jax.ShapeDtypeStruct((m, n), BF16)] * 3,
        compiler_params=_cparams("parallel", "parallel"),
        name=name,
    )(h, wgt, wut)


def _rms_fwd(x, g, *, tr, name):
    t, d = x.shape

    def body(x_ref, g_ref, h_ref):
        xv = x_ref[...]
        r = lax.rsqrt(jnp.mean(xv * xv, axis=-1, keepdims=True) + EPS)
        h_ref[...] = (xv * r * g_ref[...]).astype(BF16)

    row = pl.BlockSpec((tr, d), lambda i: (i, 0))
    return pl.pallas_call(
        body,
        grid=(t // tr,),
        in_specs=[row, pl.BlockSpec((1, d), lambda i: (0, 0))],
        out_specs=row,
        out_shape=jax.ShapeDtypeStruct((t, d), BF16),
        compiler_params=_cparams("parallel"),
        name=name,
    )(x, g)


def _rms_bwd_math(dy, xv, g):
    r = lax.rsqrt(jnp.mean(xv * xv, axis=-1, keepdims=True) + EPS)
    xh = xv * r
    dg = jnp.sum(dy * xh, axis=0, keepdims=True)
    dyg = dy * g
    dx = r * (dyg - xh * jnp.mean(dyg * xh, axis=-1, keepdims=True))
    return dx, dg


def _rms_bwd(dh, x, g, dres, *, tr, name):
    t, d = x.shape

    def body(dh_ref, x_ref, g_ref, dres_ref, dx_ref, dxb_ref, dg_ref):
        dx, dg = _rms_bwd_math(dh_ref[...].astype(F32), x_ref[...], g_ref[...])
        dx = dx + dres_ref[...]
        dx_ref[...] = dx
        dxb_ref[...] = dx.astype(BF16)

        @pl.when(pl.program_id(0) == 0)
        def _():
            dg_ref[...] = jnp.zeros_like(dg_ref)

        dg_ref[0:1, :] += dg

    row = pl.BlockSpec((tr, d), lambda i: (i, 0))
    vec = pl.BlockSpec((1, d), lambda i: (0, 0))
    return pl.pallas_call(
        body,
        grid=(t // tr,),
        in_specs=[row, row, vec, row],
        out_specs=[row, row, pl.BlockSpec((8, d), lambda i: (0, 0))],
        out_shape=[jax.ShapeDtypeStruct((t, d), F32), jax.ShapeDtypeStruct((t, d), BF16),
                   jax.ShapeDtypeStruct((8, d), F32)],
        compiler_params=_cparams("arbitrary"),
        name=name,
    )(dh, x, g, dres)


def _loss_head(x, g, target, *, tr, name):
    t, d = x.shape

    def body(x_ref, g_ref, t_ref, dx_ref, dxb_ref, st_ref):
        xv = x_ref[...]
        gv = g_ref[...]
        r = lax.rsqrt(jnp.mean(xv * xv, axis=-1, keepdims=True) + EPS)
        err = xv * r * gv - t_ref[...]
        sq = jnp.sum(jnp.sum(err * err, axis=1, keepdims=True), axis=0, keepdims=True)
        dx, dg = _rms_bwd_math(err * (1.0 / d), xv, gv)
        dx_ref[...] = dx
        dxb_ref[...] = dx.astype(BF16)

        @pl.when(pl.program_id(0) == 0)
        def _():
            st_ref[...] = jnp.zeros_like(st_ref)

        st_ref[0:1, :] += dg
        st_ref[1:2, :] += jnp.broadcast_to(sq * (0.5 / d), (1, d))

    row = pl.BlockSpec((tr, d), lambda i: (i, 0))
    return pl.pallas_call(
        body,
        grid=(t // tr,),
        in_specs=[row, pl.BlockSpec((1, d), lambda i: (0, 0)), row],
        out_specs=[row, row, pl.BlockSpec((8, d), lambda i: (0, 0))],
        out_shape=[jax.ShapeDtypeStruct((t, d), F32), jax.ShapeDtypeStruct((t, d), BF16),
                   jax.ShapeDtypeStruct((8, d), F32)],
        compiler_params=_cparams("arbitrary"),
        name=name,
    )(x, g, target)


def _rope_tables(t):
    half = ROT_DIM // 2
    pos = jnp.arange(t, dtype=F32)
    inv = ROPE_THETA ** (-jnp.arange(0, ROT_DIM, 2, dtype=F32) / ROT_DIM)
    ang = pos[:, None] * inv[None, :]
    cos, sin = jnp.cos(ang), jnp.sin(ang)
    rest = HEAD_DIM - ROT_DIM
    c = jnp.concatenate([cos, cos, jnp.ones((t, rest), F32)], axis=1)
    sa = jnp.concatenate([-sin, jnp.zeros((t, HEAD_DIM - half), F32)], axis=1)
    sb = jnp.concatenate([jnp.zeros((t, half), F32), sin, jnp.zeros((t, rest), F32)], axis=1)
    return c, sa, sb


def _rope_apply(v, c, sa, sb):
    half = ROT_DIM // 2
    return v * c + pltpu.roll(v, HEAD_DIM - half, axis=1) * sa + pltpu.roll(v, half, axis=1) * sb


def _rope_apply_t(dv, c, sa, sb):
    half = ROT_DIM // 2
    return dv * c + pltpu.roll(dv * sa, half, axis=1) + pltpu.roll(dv * sb, HEAD_DIM - half, axis=1)


def _in_proj(h, win_t, tabs, cfg, *, tm, tn, name):
    m, kdim = h.shape
    n = win_t.shape[0]
    per_tile = tn // HEAD_DIM
    n_rot = cfg.hq + cfg.g
    rot_tiles = -(-n_rot // per_tile)

    def body(h_ref, w_ref, c_ref, sa_ref, sb_ref, z_ref):
        j = pl.program_id(1)
        acc = _dot(h_ref[...], w_ref[...], "nt")

        @pl.when(j < rot_tiles)
        def _():
            for b in range(per_tile):
                sl = slice(b * HEAD_DIM, (b + 1) * HEAD_DIM)
                rot = _rope_apply(acc[:, sl], c_ref[...], sa_ref[...], sb_ref[...])
                z_ref[:, sl] = jnp.where(j * per_tile + b < n_rot, rot, acc[:, sl]).astype(BF16)

        @pl.when(j >= rot_tiles)
        def _():
            z_ref[...] = acc.astype(BF16)

    tab = pl.BlockSpec((tm, HEAD_DIM), lambda i, j: (i, 0))
    return pl.pallas_call(
        body,
        grid=(m // tm, n // tn),
        in_specs=[pl.BlockSpec((tm, kdim), lambda i, j: (i, 0)), pl.BlockSpec((tn, kdim), lambda i, j: (j, 0)),
                  tab, tab, tab],
        out_specs=pl.BlockSpec((tm, tn), lambda i, j: (i, j)),
        out_shape=jax.ShapeDtypeStruct((m, n), BF16),
        compiler_params=_cparams("parallel", "parallel"),
        name=name,
    )(h, win_t, *tabs)


def _attn_specs(cfg, nb):
    kcol = cfg.hq
    vcol = cfg.off_v // HEAD_DIM
    qw = Q_PER_KV * HEAD_DIM
    q_spec = pl.BlockSpec((BLOCK, qw), lambda g, n: (n, g))

    def kv(col, shift):
        def idx(g, n):
            return (jnp.clip(n + shift, 0, nb - 1), col + g)
        return pl.BlockSpec((BLOCK, HEAD_DIM), idx)

    k_specs = [kv(kcol, s) for s in (-1, 0, 1)]
    v_specs = [kv(vcol, s) for s in (-1, 0, 1)]
    return q_spec, k_specs, v_specs


def _attn_probs(q, k, valid, sk):
    scale = 1.0 / math.sqrt(HEAD_DIM)
    s = _dot(q, k, "nt") * scale
    s = jnp.where(valid, s, jnp.finfo(F32).min)
    m = jnp.maximum(jnp.max(s, axis=1, keepdims=True), sk)
    e = jnp.exp(s - m)
    es = jnp.exp(sk - m)
    inv = 1.0 / (jnp.sum(e, axis=1, keepdims=True) + es)
    return e * inv, es * inv


def _attn_valid(n, t):
    shape = (Q_PER_KV * BLOCK, 3 * BLOCK)
    qpos = n * BLOCK + (lax.broadcasted_iota(jnp.int32, shape, 0) & (BLOCK - 1))
    kpos = (n - 1) * BLOCK + lax.broadcasted_iota(jnp.int32, shape, 1)
    return (kpos >= 0) & (kpos < t) & (jnp.abs(qpos - kpos) <= WINDOW)


def _stack_heads(ref):
    return jnp.concatenate([ref[:, r * HEAD_DIM:(r + 1) * HEAD_DIM] for r in range(Q_PER_KV)], axis=0)


def _sink_column(sink_ref, g):
    head = lax.broadcasted_iota(jnp.int32, (Q_PER_KV * BLOCK, 1), 0) // BLOCK
    col = jnp.full((Q_PER_KV * BLOCK, 1), sink_ref[g * Q_PER_KV], F32)
    for r in range(1, Q_PER_KV):
        col = jnp.where(head == r, sink_ref[g * Q_PER_KV + r], col)
    return col


def _attn_fwd(qk, z, sink, cfg, *, name):
    t = qk.shape[0]
    nb = t // BLOCK
    q_spec, k_specs, v_specs = _attn_specs(cfg, nb)

    def body(sink_ref, q_ref, kp, kc, kn, vp, vc, vn, o_ref):
        g = pl.program_id(0)
        n = pl.program_id(1)
        k = jnp.concatenate([kp[...], kc[...], kn[...]], axis=0)
        v = jnp.concatenate([vp[...], vc[...], vn[...]], axis=0)
        p, _ = _attn_probs(_stack_heads(q_ref), k, _attn_valid(n, t), _sink_column(sink_ref, g))
        o = _dot(p.astype(BF16), v, "nn")
        for r in range(Q_PER_KV):
            o_ref[:, r * HEAD_DIM:(r + 1) * HEAD_DIM] = o[r * BLOCK:(r + 1) * BLOCK].astype(BF16)

    return pl.pallas_call(
        body,
        grid=(cfg.g, nb),
        in_specs=[pl.BlockSpec(memory_space=pltpu.SMEM), q_spec] + k_specs + v_specs,
        out_specs=q_spec,
        out_shape=jax.ShapeDtypeStruct((t, cfg.d), BF16),
        compiler_params=_cparams("parallel", "parallel"),
        name=name,
    )(sink, qk, qk, qk, qk, z, z, z)


def _attn_bwd(qk, z, dmix, sink, tabs, cfg, *, name):
    t = qk.shape[0]
    nb = t // BLOCK
    q_spec, k_specs, v_specs = _attn_specs(cfg, nb)
    tab = pl.BlockSpec((BLOCK, HEAD_DIM), lambda g, n: (n, 0))
    acc_spec = pl.BlockSpec((None, t + 2 * KV_PAD, HEAD_DIM), lambda g, n: (g, 0, 0))
    scale = 1.0 / math.sqrt(HEAD_DIM)

    def body(sink_ref, q_ref, kp, kc, kn, vp, vc, vn, do_ref, c_ref, sa_ref, sb_ref,
             dq_ref, dk_ref, dv_ref, ds_ref):
        g = pl.program_id(0)
        n = pl.program_id(1)

        @pl.when(n == 0)
        def _():
            dk_ref[...] = jnp.zeros_like(dk_ref)
            dv_ref[...] = jnp.zeros_like(dv_ref)
            ds_ref[...] = jnp.zeros_like(ds_ref)

        k = jnp.concatenate([kp[...], kc[...], kn[...]], axis=0)
        v = jnp.concatenate([vp[...], vc[...], vn[...]], axis=0)
        q = _stack_heads(q_ref)
        do = _stack_heads(do_ref)
        p, ps = _attn_probs(q, k, _attn_valid(n, t), _sink_column(sink_ref, g))
        dp = _dot(do, v, "nt")
        delta = jnp.sum(p * dp, axis=1, keepdims=True)
        dsc = (p * (dp - delta) * scale).astype(BF16)
        dq = _dot(dsc, k, "nn")
        dsink = ps * delta
        for r in range(Q_PER_KV):
            rows = slice(r * BLOCK, (r + 1) * BLOCK)
            dq_ref[:, r * HEAD_DIM:(r + 1) * HEAD_DIM] = _rope_apply_t(
                dq[rows], c_ref[...], sa_ref[...], sb_ref[...]).astype(BF16)
            ds_ref[r:r + 1, :] -= jnp.broadcast_to(jnp.sum(dsink[rows], axis=0, keepdims=True), (1, HEAD_DIM))
        rows = pl.ds(pl.multiple_of(n * BLOCK + (KV_PAD - BLOCK), BLOCK), 3 * BLOCK)
        dk_ref[rows, :] += _dot(dsc, q, "tn")
        dv_ref[rows, :] += _dot(p.astype(BF16), do, "tn")

    acc_shape = jax.ShapeDtypeStruct((cfg.g, t + 2 * KV_PAD, HEAD_DIM), F32)
    return pl.pallas_call(
        body,
        grid=(cfg.g, nb),
        in_specs=[pl.BlockSpec(memory_space=pltpu.SMEM), q_spec] + k_specs + v_specs + [q_spec, tab, tab, tab],
        out_specs=[q_spec, acc_spec, acc_spec, pl.BlockSpec((None, 8, HEAD_DIM), lambda g, n: (g, 0, 0))],
        out_shape=[jax.ShapeDtypeStruct((t, cfg.attn), BF16), acc_shape, acc_shape,
                   jax.ShapeDtypeStruct((cfg.g, 8, HEAD_DIM), F32)],
        compiler_params=_cparams("arbitrary", "arbitrary"),
        name=name,
    )(sink, qk, qk, qk, qk, z, z, z, dmix, *tabs)


def _kv_finish(dk_acc, dv_acc, tabs, cfg, *, name):
    t = dk_acc.shape[1] - 2 * KV_PAD

    def body(dk_ref, dv_ref, c_ref, sa_ref, sb_ref, ok_ref, ov_ref):
        ok_ref[...] = _rope_apply_t(dk_ref[...], c_ref[...], sa_ref[...], sb_ref[...]).astype(BF16)
        ov_ref[...] = dv_ref[...].astype(BF16)

    acc = pl.BlockSpec((None, KV_PAD, HEAD_DIM), lambda g, i: (g, i + 1, 0))
    tab = pl.BlockSpec((KV_PAD, HEAD_DIM), lambda g, i: (i, 0))
    out = pl.BlockSpec((KV_PAD, HEAD_DIM), lambda g, i: (i, g))
    return pl.pallas_call(
        body,
        grid=(cfg.g, t // KV_PAD),
        in_specs=[acc, acc, tab, tab, tab],
        out_specs=[out, out],
        out_shape=[jax.ShapeDtypeStruct((t, cfg.kv), BF16)] * 2,
        compiler_params=_cparams("parallel", "parallel"),
        name=name,
    )(dk_acc, dv_acc, *tabs)


def _halo_specs(width, col, tb, t):
    per = tb // HALO_ROWS
    last = t // HALO_ROWS - 1
    prev = pl.BlockSpec((HALO_ROWS, width), lambda i: (jnp.maximum(i * per - 1, 0), col))
    cur = pl.BlockSpec((tb, width), lambda i: (i, col))
    nxt = pl.BlockSpec((HALO_ROWS, width), lambda i: (jnp.minimum((i + 1) * per, last), col))
    return [prev, cur, nxt]


def _halo_load(refs):
    return jnp.concatenate([r[...].astype(F32) for r in refs], axis=0)


F32_SUBLANES = 8


CONV_ROWS = 64
LANES = 128


def _store_phases(ph_ref, v):
    total = v.shape[0]
    ph_ref[0] = v
    for p in range(1, F32_SUBLANES):
        ph_ref[p] = pltpu.roll(v, total - p, axis=0)


def _shifted(ph_ref, start, r0, lanes):
    base = start - start % F32_SUBLANES + r0
    return ph_ref[start % F32_SUBLANES, base:base + CONV_ROWS, lanes]


def _conv_blocks(tb, cw):
    return [(r0, slice(l0, l0 + LANES)) for l0 in range(0, cw, LANES) for r0 in range(0, tb, CONV_ROWS)]


def _conv_glu(a_refs, g_refs, i, tb, t):
    a = _halo_load(a_refs)
    g = _halo_load(g_refs)
    rows = i * tb - HALO_ROWS + lax.broadcasted_iota(jnp.int32, (tb + 2 * HALO_ROWS, 1), 0)
    valid = (rows >= 0) & (rows < t)
    sg = _sigmoid(g)
    return a, sg, jnp.where(valid, a * sg, 0.0), valid


def _conv_fwd(z, mix, w, b, lg, lb, cfg, *, tb, name):
    t = z.shape[0]
    cw = cfg.conv
    vec = pl.BlockSpec((1, cw), lambda i: (0, 0))

    def body(ap, ac, an, gp, gc, gn, w_ref, b_ref, lg_ref, lb_ref, mix_ref, o_ref, y_ref, c_ph):
        _, _, c, _ = _conv_glu((ap, ac, an), (gp, gc, gn), pl.program_id(0), tb, t)
        _store_phases(c_ph, c)
        for r0, lanes in _conv_blocks(tb, cw):
            acc = jnp.zeros((CONV_ROWS, LANES), F32)
            for j in range(CONV_KERNEL):
                acc = acc + w_ref[j:j + 1, lanes] * _shifted(c_ph, j + HALO_ROWS - CONV_PAD, r0, lanes)
            y_ref[r0:r0 + CONV_ROWS, lanes] = acc + b_ref[:, lanes]
        y = y_ref[...]
        mu = jnp.mean(y, axis=-1, keepdims=True)
        dlt = y - mu
        var = jnp.mean(dlt * dlt, axis=-1, keepdims=True)
        yn = dlt * lax.rsqrt(var + EPS) * lg_ref[...] + lb_ref[...]
        o_ref[...] = (yn * _sigmoid(yn)).astype(BF16)

    return pl.pallas_call(
        body,
        grid=(t // tb,),
        in_specs=(_halo_specs(cw, cfg.off_ca // cw, tb, t) + _halo_specs(cw, cfg.off_cg // cw, tb, t)
                  + [pl.BlockSpec((CONV_KERNEL + 1, cw), lambda i: (0, 0)), vec, vec, vec,
                     pl.BlockSpec(memory_space=pl.ANY)]),
        out_specs=[pl.BlockSpec((tb, cw), lambda i: (i, cfg.attn // cw)), pl.BlockSpec((tb, cw), lambda i: (i, 0))],
        out_shape=[jax.ShapeDtypeStruct(mix.shape, BF16), jax.ShapeDtypeStruct((t, cw), F32)],
        input_output_aliases={10: 0},
        scratch_shapes=[pltpu.VMEM((F32_SUBLANES, tb + 2 * HALO_ROWS, cw), F32)],
        compiler_params=_cparams("parallel"),
        name=name,
    )(z, z, z, z, z, z, w, b, lg, lb, mix)


def _conv_bwd(z, y, dmix, w, lg, lb, cfg, *, tb, name):
    t = z.shape[0]
    cw = cfg.conv
    vec = pl.BlockSpec((1, cw), lambda i: (0, 0))
    cen = slice(HALO_ROWS, HALO_ROWS + tb)

    def body(ap, ac, an, gp, gc, gn, yp, yc, yn_, dp, dc_, dn, w_ref, lg_ref, lb_ref,
             da_ref, dg_ref, dw_ref, st_ref, dy_ph, c_ph):
        @pl.when(pl.program_id(0) == 0)
        def _():
            dw_ref[...] = jnp.zeros_like(dw_ref)
            st_ref[...] = jnp.zeros_like(st_ref)

        a, sg, c, valid = _conv_glu((ap, ac, an), (gp, gc, gn), pl.program_id(0), tb, t)
        yv = _halo_load((yp, yc, yn_))
        do = _halo_load((dp, dc_, dn))
        mu = jnp.mean(yv, axis=-1, keepdims=True)
        dlt = yv - mu
        rstd = lax.rsqrt(jnp.mean(dlt * dlt, axis=-1, keepdims=True) + EPS)
        xh = dlt * rstd
        lgv = lg_ref[...]
        yn = xh * lgv + lb_ref[...]
        s = _sigmoid(yn)
        dyn = do * (s * (1.0 + yn * (1.0 - s)))
        dxh = dyn * lgv
        dy = rstd * (dxh - jnp.mean(dxh, axis=-1, keepdims=True)
                     - xh * jnp.mean(dxh * xh, axis=-1, keepdims=True))
        dy = jnp.where(valid, dy, 0.0)
        dyc = dy[cen]
        st_ref[0:1, :] += jnp.sum(dyc, axis=0, keepdims=True)
        st_ref[1:2, :] += jnp.sum((dyn * xh)[cen], axis=0, keepdims=True)
        st_ref[2:3, :] += jnp.sum(dyn[cen], axis=0, keepdims=True)
        _store_phases(dy_ph, dy)
        _store_phases(c_ph, c)
        for r0, lanes in _conv_blocks(tb, cw):
            dc = jnp.zeros((CONV_ROWS, LANES), F32)
            for j in range(CONV_KERNEL):
                dc = dc + w_ref[j:j + 1, lanes] * _shifted(dy_ph, HALO_ROWS + CONV_PAD - j, r0, lanes)
            rows = slice(HALO_ROWS + r0, HALO_ROWS + r0 + CONV_ROWS)
            sgc = sg[rows, lanes]
            da_ref[r0:r0 + CONV_ROWS, lanes] = (dc * sgc).astype(BF16)
            dg_ref[r0:r0 + CONV_ROWS, lanes] = (dc * a[rows, lanes] * sgc * (1.0 - sgc)).astype(BF16)
        for l0 in range(0, cw, LANES):
            lanes = slice(l0, l0 + LANES)
            for j in range(CONV_KERNEL):
                part = jnp.zeros((CONV_ROWS, LANES), F32)
                for r0 in range(0, tb, CONV_ROWS):
                    part = part + (_shifted(dy_ph, HALO_ROWS, r0, lanes)
                                   * _shifted(c_ph, j + HALO_ROWS - CONV_PAD, r0, lanes))
                dw_ref[j:j + 1, lanes] += jnp.sum(part, axis=0, keepdims=True)

    out = pl.BlockSpec((tb, cw), lambda i: (i, 0))
    wspec = pl.BlockSpec((CONV_KERNEL + 1, cw), lambda i: (0, 0))
    return pl.pallas_call(
        body,
        grid=(t // tb,),
        in_specs=(_halo_specs(cw, cfg.off_ca // cw, tb, t) + _halo_specs(cw, cfg.off_cg // cw, tb, t)
                  + _halo_specs(cw, 0, tb, t) + _halo_specs(cw, cfg.attn // cw, tb, t) + [wspec, vec, vec]),
        out_specs=[out, out, wspec, pl.BlockSpec((8, cw), lambda i: (0, 0))],
        out_shape=[jax.ShapeDtypeStruct((t, cw), BF16), jax.ShapeDtypeStruct((t, cw), BF16),
                   jax.ShapeDtypeStruct((CONV_KERNEL + 1, cw), F32), jax.ShapeDtypeStruct((8, cw), F32)],
        scratch_shapes=[pltpu.VMEM((F32_SUBLANES, tb + 2 * HALO_ROWS, cw), F32)] * 2,
        compiler_params=_cparams("arbitrary"),
        name=name,
    )(z, z, z, z, z, z, y, y, y, dmix, dmix, dmix, w, lg, lb)


_SQRT_HALF = 1.0 / math.sqrt(2.0)
_INV_SQRT_2PI = 1.0 / math.sqrt(2.0 * math.pi)


def _gelu(v):
    return 0.5 * v * (1.0 + lax.erf(v * _SQRT_HALF))


def _gelu_grad(v):
    return 0.5 * (1.0 + lax.erf(v * _SQRT_HALF)) + v * jnp.exp(-0.5 * v * v) * _INV_SQRT_2PI


def _sgu_norm(zv_ref, lg_ref, lb_ref):
    xv = zv_ref[...].astype(F32)
    v = _gelu(xv)
    mu = jnp.mean(v, axis=-1, keepdims=True)
    dlt = v - mu
    rstd = lax.rsqrt(jnp.mean(dlt * dlt, axis=-1, keepdims=True) + EPS)
    xh = dlt * rstd
    return xv, xh, rstd, xh * lg_ref[...] + lb_ref[...]


def _sgu_specs(cfg):
    sw = cfg.sgu
    zu = pl.BlockSpec((CHUNK, sw), lambda i: (i, cfg.off_u // sw))
    zv = pl.BlockSpec((CHUNK, sw), lambda i: (i, cfg.off_sv // sw))
    vec = pl.BlockSpec((1, sw), lambda i: (0, 0))
    ws = pl.BlockSpec((cfg.sh, CHUNK, CHUNK), lambda i: (0, 0, 0))
    bs = pl.BlockSpec((cfg.sh, CHUNK, 1), lambda i: (0, 0, 0))
    return zu, zv, vec, ws, bs


def _sgu_fwd(z, mix, lg, lb, ws, bs, cfg, *, name):
    t = z.shape[0]
    sw = cfg.sgu
    zu, zv, vec, wspec, bspec = _sgu_specs(cfg)

    def body(zu_ref, zv_ref, lg_ref, lb_ref, ws_ref, bs_ref, mix_ref, o_ref):
        u = _gelu(zu_ref[...].astype(F32))
        _, _, _, vn = _sgu_norm(zv_ref, lg_ref, lb_ref)
        vnb = vn.astype(BF16)
        for h in range(cfg.sh):
            sl = slice(h * HEAD_DIM, (h + 1) * HEAD_DIM)
            sp = _dot(ws_ref[h], vnb[:, sl], "nn") + bs_ref[h]
            o_ref[:, sl] = (u[:, sl] * sp).astype(BF16)

    return pl.pallas_call(
        body,
        grid=(t // CHUNK,),
        in_specs=[zu, zv, vec, vec, wspec, bspec, pl.BlockSpec(memory_space=pl.ANY)],
        out_specs=pl.BlockSpec((CHUNK, sw), lambda i: (i, (cfg.attn + cfg.conv) // sw)),
        out_shape=jax.ShapeDtypeStruct(mix.shape, BF16),
        input_output_aliases={6: 0},
        compiler_params=_cparams("parallel"),
        name=name,
    )(z, z, lg, lb, ws, bs, mix)


def _sgu_bwd(z, dmix, lg, lb, ws, bs, cfg, *, name):
    t = z.shape[0]
    sw = cfg.sgu
    zu, zv, vec, wspec, bspec = _sgu_specs(cfg)
    do_spec = pl.BlockSpec((CHUNK, sw), lambda i: (i, (cfg.attn + cfg.conv) // sw))

    def body(zu_ref, zv_ref, do_ref, lg_ref, lb_ref, ws_ref, bs_ref, duv_ref, dws_ref, dbs_ref, st_ref):
        @pl.when(pl.program_id(0) == 0)
        def _():
            dws_ref[...] = jnp.zeros_like(dws_ref)
            dbs_ref[...] = jnp.zeros_like(dbs_ref)
            st_ref[...] = jnp.zeros_like(st_ref)

        xu = zu_ref[...].astype(F32)
        u = _gelu(xu)
        xv, xh, rstd, vn = _sgu_norm(zv_ref, lg_ref, lb_ref)
        vnb = vn.astype(BF16)
        do = do_ref[...].astype(F32)
        dvn_parts = []
        for h in range(cfg.sh):
            sl = slice(h * HEAD_DIM, (h + 1) * HEAD_DIM)
            wh = ws_ref[h]
            sp = _dot(wh, vnb[:, sl], "nn") + bs_ref[h]
            dsp = do[:, sl] * u[:, sl]
            dspb = dsp.astype(BF16)
            dvn_parts.append(_dot(wh, dspb, "tn"))
            dws_ref[h] += _dot(dspb, vnb[:, sl], "nt")
            dbs_ref[h] += jnp.sum(dsp, axis=1, keepdims=True)
            duv_ref[:, sl] = (do[:, sl] * sp * _gelu_grad(xu[:, sl])).astype(BF16)
        dvn = jnp.concatenate(dvn_parts, axis=1)
        st_ref[0:1, :] += jnp.sum(dvn * xh, axis=0, keepdims=True)
        st_ref[1:2, :] += jnp.sum(dvn, axis=0, keepdims=True)
        dxh = dvn * lg_ref[...]
        dv = rstd * (dxh - jnp.mean(dxh, axis=-1, keepdims=True)
                     - xh * jnp.mean(dxh * xh, axis=-1, keepdims=True))
        duv_ref[:, sw:] = (dv * _gelu_grad(xv)).astype(BF16)

    return pl.pallas_call(
        body,
        grid=(t // CHUNK,),
        in_specs=[zu, zv, do_spec, vec, vec, wspec, bspec],
        out_specs=[pl.BlockSpec((CHUNK, 2 * sw), lambda i: (i, 0)), wspec, bspec,
                   pl.BlockSpec((8, sw), lambda i: (0, 0))],
        out_shape=[jax.ShapeDtypeStruct((t, 2 * sw), BF16), jax.ShapeDtypeStruct((cfg.sh, CHUNK, CHUNK), F32),
                   jax.ShapeDtypeStruct((cfg.sh, CHUNK, 1), F32), jax.ShapeDtypeStruct((8, sw), F32)],
        compiler_params=_cparams("arbitrary"),
        name=name,
    )(z, z, dmix, lg, lb, ws, bs)


def _sum_shards(own, landed, *, name):
    _, r, c = own.shape
    my_id = _flat_id(_my_coords()).astype(jnp.int32).reshape(1)
    tr = _pick(r, (64, 32, 16))

    def body(me_ref, *refs):
        o_ref = refs[N_DEV]
        acc = refs[0][...].astype(F32)
        for p_ref in refs[1:N_DEV]:
            acc = acc + p_ref[...].astype(F32)
        o_ref[...] = acc

    def slab(p):
        return pl.BlockSpec((None, tr, c), lambda i, me: (me[0] ^ p, i, 0))

    return pl.pallas_call(
        body,
        grid_spec=pltpu.PrefetchScalarGridSpec(
            num_scalar_prefetch=1,
            grid=(r // tr,),
            in_specs=[slab(p) for p in range(N_DEV)],
            out_specs=pl.BlockSpec((tr, c), lambda i, me: (i, 0)),
        ),
        out_shape=jax.ShapeDtypeStruct((r, c), F32),
        compiler_params=_cparams("parallel"),
        name=name,
    )(my_id, own, *([landed] * (N_DEV - 1)))


def _adamw(w, g, m, v, *, name):
    r, c = w.shape
    tr = _pick(r, [p for p in (1024, 512, 256, 128, 64, 32, 16, 8) if p * c <= ADAMW_TILE_ELEMS])

    def body(w_ref, g_ref, m_ref, v_ref, d_ref, nm_ref, nv_ref):
        gv = g_ref[...]
        nm = ADAM_B1 * m_ref[...] + (1.0 - ADAM_B1) * gv
        nv = ADAM_B2 * v_ref[...] + (1.0 - ADAM_B2) * (gv * gv)
        m_hat = nm / (1.0 - ADAM_B1 ** ADAM_STEP)
        v_hat = nv / (1.0 - ADAM_B2 ** ADAM_STEP)
        d_ref[...] = -ADAM_LR * (m_hat / (jnp.sqrt(v_hat) + ADAM_EPS) + ADAM_WD * w_ref[...])
        nm_ref[...] = nm
        nv_ref[...] = nv

    blk = pl.BlockSpec((tr, c), lambda i: (i, 0))
    return pl.pallas_call(
        body,
        grid=(r // tr,),
        in_specs=[blk] * 4,
        out_specs=[blk] * 3,
        out_shape=[jax.ShapeDtypeStruct((r, c), F32)] * 3,
        compiler_params=_cparams("parallel"),
        name=name,
    )(w, g, m, v)


def _my_coords():
    return tuple(lax.axis_index(a) for a in MESH_AXES)


def _peer_coords(me, p):
    return tuple(1 - v if (p >> (2 - a)) & 1 else v for a, v in enumerate(me))


def _flat_id(coords):
    return 4 * coords[0] + 2 * coords[1] + coords[2]


def _exchange(arrs, *, scatter, name):
    na = len(arrs)

    def body(*refs):
        ins = refs[:na]
        outs = refs[na:2 * na]
        send_sems, recv_sems, local_sems = refs[2 * na:]
        me = _my_coords()
        my_id = _flat_id(me)

        local = []
        for k in range(na):
            src = ins[k].at[my_id] if scatter else ins[k]
            cp = pltpu.make_async_copy(src, outs[k].at[my_id], local_sems.at[k])
            cp.start()
            local.append(cp)

        def remote(p, k):
            peer = _peer_coords(me, p)
            peer_id = _flat_id(peer)
            sem = (p - 1) * na + k
            src = ins[k].at[peer_id] if scatter else ins[k]
            send = pltpu.make_async_remote_copy(
                src_ref=src, dst_ref=outs[k].at[my_id], send_sem=send_sems.at[sem],
                recv_sem=recv_sems.at[sem], device_id=peer, device_id_type=MESH_ID)
            recv = pltpu.make_async_remote_copy(
                src_ref=src, dst_ref=outs[k].at[peer_id], send_sem=send_sems.at[sem],
                recv_sem=recv_sems.at[sem], device_id=peer, device_id_type=MESH_ID)
            return send, recv

        pairs = [remote(p, k) for p in range(1, N_DEV) for k in range(na)]
        for send, _ in pairs:
            send.start()
        for _, recv in pairs:
            recv.wait_recv()
        for send, _ in pairs:
            send.wait_send()
        for cp in local:
            cp.wait()

    def out_of(a):
        return jax.ShapeDtypeStruct(a.shape if scatter else (N_DEV,) + a.shape, a.dtype)

    hbm = pl.BlockSpec(memory_space=pl.ANY)
    nsem = (N_DEV - 1) * na
    return pl.pallas_call(
        body,
        in_specs=[hbm] * na,
        out_specs=[hbm] * na,
        out_shape=[out_of(a) for a in arrs],
        scratch_shapes=[pltpu.SemaphoreType.DMA((nsem,)), pltpu.SemaphoreType.DMA((nsem,)),
                        pltpu.SemaphoreType.DMA((na,))],
        name=name,
    )(*arrs)


_HBM = pl.BlockSpec(memory_space=pltpu.HBM)
_SEM = pl.BlockSpec(memory_space=pltpu.SEMAPHORE)
_EFFECT = pltpu.SideEffectType.DATAFLOW_SIDE_EFFECTING


def _place_own(land, src, *, scatter, name):
    _, r, c = land.shape
    tr = _pick(r, (256, 128, 64, 32, 16))
    my_id = _flat_id(_my_coords()).astype(jnp.int32).reshape(1)

    def body(me_ref, land_ref, src_ref, out_ref):
        out_ref[...] = src_ref[...]

    if scatter:
        src_spec = pl.BlockSpec((None, tr, c), lambda i, me: (me[0], i, 0))
    else:
        src_spec = pl.BlockSpec((tr, c), lambda i, me: (i, 0))
    return pl.pallas_call(
        body,
        grid_spec=pltpu.PrefetchScalarGridSpec(
            num_scalar_prefetch=1,
            grid=(r // tr,),
            in_specs=[pl.BlockSpec(memory_space=pl.ANY), src_spec],
            out_specs=pl.BlockSpec((None, tr, c), lambda i, me: (me[0], i, 0)),
        ),
        out_shape=jax.ShapeDtypeStruct(land.shape, land.dtype),
        input_output_aliases={1: 0},
        compiler_params=_cparams("parallel"),
        name=name,
    )(my_id, land, src)


def _peer_copy(src_ref, land_ref, send_sems, recv_sems, me, p, scatter, arrival):
    peer = _peer_coords(me, p)
    peer_id = _flat_id(peer)
    return pltpu.make_async_remote_copy(
        src_ref=src_ref.at[peer_id] if scatter else src_ref,
        dst_ref=land_ref.at[peer_id if arrival else _flat_id(me)],
        send_sem=send_sems.at[p - 1], recv_sem=recv_sems.at[p - 1], device_id=peer, device_id_type=MESH_ID)


def _exchange_start(src, land, *, scatter, name):
    def body(src_ref, land_ref, send_sems, recv_sems, src_thru, land_thru, token):
        me = _my_coords()
        for p in range(1, N_DEV):
            _peer_copy(src_ref, land_ref, send_sems, recv_sems, me, p, scatter, False).start()
        token[...] = jnp.zeros_like(token)

    nsem = N_DEV - 1
    return pl.pallas_call(
        body,
        name=name,
        out_shape=(pltpu.SemaphoreType.DMA((nsem,)), pltpu.SemaphoreType.DMA((nsem,)),
                   pltpu.HBM(src.shape, src.dtype), pltpu.HBM(land.shape, land.dtype),
                   jax.ShapeDtypeStruct((8, 128), F32)),
        in_specs=(_HBM, _HBM),
        out_specs=(_SEM, _SEM, _HBM, _HBM, pl.BlockSpec(memory_space=pltpu.VMEM)),
        input_output_aliases={0: 2, 1: 3},
        compiler_params=pltpu.CompilerParams(has_side_effects=_EFFECT),
    )(pltpu.with_memory_space_constraint(src, pltpu.HBM), pltpu.with_memory_space_constraint(land, pltpu.HBM))


def _exchange_wait(handle, after, *, scatter, name):
    send_sems, recv_sems, src_thru, land_thru = handle

    def body(src_ref, land_ref, send_sems, recv_sems, after_ref, src_dead, got_ref):
        me = _my_coords()
        for p in range(1, N_DEV):
            _peer_copy(src_ref, land_ref, send_sems, recv_sems, me, p, scatter, False).wait_send()
            _peer_copy(src_ref, land_ref, send_sems, recv_sems, me, p, scatter, True).wait_recv()

    return pl.pallas_call(
        body,
        name=name,
        out_shape=(pltpu.HBM(src_thru.shape, src_thru.dtype), pltpu.HBM(land_thru.shape, land_thru.dtype)),
        in_specs=(_HBM, _HBM, _SEM, _SEM, pl.BlockSpec(memory_space=pl.ANY)),
        out_specs=(_HBM, _HBM),
        input_output_aliases={0: 0, 1: 1},
        compiler_params=pltpu.CompilerParams(has_side_effects=_EFFECT),
    )(src_thru, land_thru, send_sems, recv_sems, after)


def _exchange_begin(src, *, scatter, name):
    if scatter:
        land = lax.empty(src.shape, src.dtype)
    else:
        land = _place_own(lax.empty((N_DEV,) + src.shape, src.dtype), src, scatter=False, name=name + "_own")
    *handle, token = _exchange_start(src, land, scatter=scatter, name=name + "_start")
    return tuple(handle), token[0, 0]


_SIBLING_MASK = 1
_CHIP_MASKS = (2, 4, 6)
_DIRECT_MASKS = (_SIBLING_MASK,) + _CHIP_MASKS


def _direct_copy(src_ref, land_ref, send_sems, recv_sems, me, a, j, arrival):
    p = _DIRECT_MASKS[j]
    peer = _peer_coords(me, p)
    sem = a * len(_DIRECT_MASKS) + j
    return pltpu.make_async_remote_copy(
        src_ref=src_ref, dst_ref=land_ref.at[_flat_id(peer) if arrival else _flat_id(me)],
        send_sem=send_sems.at[sem], recv_sem=recv_sems.at[sem], device_id=peer, device_id_type=MESH_ID)


def _relay_copy(land_ref, send_sems, recv_sems, me, a, j, arrival):
    sibling = _peer_coords(me, _SIBLING_MASK)
    holder = sibling if arrival else me
    slab = land_ref.at[_flat_id(_peer_coords(holder, _CHIP_MASKS[j]))]
    sem = a * len(_CHIP_MASKS) + j
    return pltpu.make_async_remote_copy(
        src_ref=slab, dst_ref=slab, send_sem=send_sems.at[sem], recv_sem=recv_sems.at[sem],
        device_id=sibling, device_id_type=MESH_ID)


def _hbm_like(arrs):
    return tuple(pltpu.HBM(a.shape, a.dtype) for a in arrs)


def _gather_direct_start(srcs, lands, *, name):
    k = len(srcs)

    def body(*refs):
        send_sems, recv_sems = refs[2 * k:2 * k + 2]
        me = _my_coords()
        for a in range(k):
            for j in range(len(_DIRECT_MASKS)):
                _direct_copy(refs[a], refs[k + a], send_sems, recv_sems, me, a, j, False).start()
        refs[-1][...] = jnp.zeros_like(refs[-1])

    nsem = k * len(_DIRECT_MASKS)
    hbm_in = [pltpu.with_memory_space_constraint(a, pltpu.HBM) for a in (*srcs, *lands)]
    return pl.pallas_call(
        body,
        name=name,
        out_shape=(pltpu.SemaphoreType.DMA((nsem,)), pltpu.SemaphoreType.DMA((nsem,)),
                   *_hbm_like(srcs), *_hbm_like(lands), jax.ShapeDtypeStruct((8, 128), F32)),
        in_specs=(_HBM,) * (2 * k),
        out_specs=(_SEM, _SEM) + (_HBM,) * (2 * k) + (pl.BlockSpec(memory_space=pltpu.VMEM),),
        input_output_aliases={i: 2 + i for i in range(2 * k)},
        compiler_params=pltpu.CompilerParams(has_side_effects=_EFFECT),
    )(*hbm_in)


def _gather_direct_wait(send_sems, recv_sems, srcs, lands, after, *, name):
    k = len(srcs)

    def body(*refs):
        send_sems, recv_sems = refs[2 * k:2 * k + 2]
        me = _my_coords()
        for a in range(k):
            for j in range(len(_DIRECT_MASKS)):
                _direct_copy(refs[a], refs[k + a], send_sems, recv_sems, me, a, j, False).wait_send()
                _direct_copy(refs[a], refs[k + a], send_sems, recv_sems, me, a, j, True).wait_recv()

    return pl.pallas_call(
        body,
        name=name,
        out_shape=(*_hbm_like(srcs), *_hbm_like(lands)),
        in_specs=(_HBM,) * (2 * k) + (_SEM, _SEM, pl.BlockSpec(memory_space=pl.ANY)),
        out_specs=(_HBM,) * (2 * k),
        input_output_aliases={i: i for i in range(2 * k)},
        compiler_params=pltpu.CompilerParams(has_side_effects=_EFFECT),
    )(*srcs, *lands, send_sems, recv_sems, after)[k:]


def _gather_relay_start(lands, *, name):
    k = len(lands)

    def body(*refs):
        send_sems, recv_sems = refs[k:k + 2]
        me = _my_coords()
        for a in range(k):
            for j in range(len(_CHIP_MASKS)):
                _relay_copy(refs[a], send_sems, recv_sems, me, a, j, False).start()
        refs[-1][...] = jnp.zeros_like(refs[-1])

    nsem = k * len(_CHIP_MASKS)
    return pl.pallas_call(
        body,
        name=name,
        out_shape=(pltpu.SemaphoreType.DMA((nsem,)), pltpu.SemaphoreType.DMA((nsem,)),
                   *_hbm_like(lands), jax.ShapeDtypeStruct((8, 128), F32)),
        in_specs=(_HBM,) * k,
        out_specs=(_SEM, _SEM) + (_HBM,) * k + (pl.BlockSpec(memory_space=pltpu.VMEM),),
        input_output_aliases={i: 2 + i for i in range(k)},
        compiler_params=pltpu.CompilerParams(has_side_effects=_EFFECT),
    )(*lands)


def _gather_relay_wait(send_sems, recv_sems, lands, after, *, name):
    k = len(lands)

    def body(*refs):
        send_sems, recv_sems = refs[k:k + 2]
        me = _my_coords()
        for a in range(k):
            for j in range(len(_CHIP_MASKS)):
                _relay_copy(refs[a], send_sems, recv_sems, me, a, j, False).wait_send()
                _relay_copy(refs[a], send_sems, recv_sems, me, a, j, True).wait_recv()

    return pl.pallas_call(
        body,
        name=name,
        out_shape=_hbm_like(lands),
        in_specs=(_HBM,) * k + (_SEM, _SEM, pl.BlockSpec(memory_space=pl.ANY)),
        out_specs=(_HBM,) * k,
        input_output_aliases={i: i for i in range(k)},
        compiler_params=pltpu.CompilerParams(has_side_effects=_EFFECT),
    )(*lands, send_sems, recv_sems, after)


def _gather_group_begin(srcs, *, name):
    lands = [_place_own(lax.empty((N_DEV,) + s.shape, s.dtype), s, scatter=False, name=f"{name}_own{i}")
             for i, s in enumerate(srcs)]
    k = len(srcs)
    out = _gather_direct_start(srcs, lands, name=name + "_start")
    return (out[0], out[1], out[2:2 + k], out[2 + k:2 + 2 * k]), out[-1][0, 0]


def _gather_group_relay(handle, after, *, name):
    lands = _gather_direct_wait(*handle, after, name=name + "_landed")
    out = _gather_relay_start(lands, name=name + "_relay")
    return (out[0], out[1], out[2:-1]), out[-1][0, 0]


def _gather_group_end(handle, after, *, name):
    return _gather_relay_wait(*handle, after, name=name + "_done")


def _allreduce_small(flat, *, name):
    r, c = flat.shape
    rc = r // N_DEV
    assert rc * N_DEV == r and rc % F32_SUBLANES == 0, r

    def body(in_ref, out_ref, buf, send1, recv1, send2, recv2):
        me = _my_coords()
        my_id = _flat_id(me)

        def rows(dev_id):
            return pl.ds(pl.multiple_of(dev_id * rc, F32_SUBLANES), rc)

        def exchange(copies):
            for send, _ in copies:
                send.start()
            for _, recv in copies:
                recv.wait_recv()
            for send, _ in copies:
                send.wait_send()

        def scatter_copy(p):
            peer = _peer_coords(me, p)
            common = dict(src_ref=in_ref.at[rows(_flat_id(peer))], send_sem=send1.at[p - 1],
                          recv_sem=recv1.at[p - 1], device_id=peer, device_id_type=MESH_ID)
            return (pltpu.make_async_remote_copy(dst_ref=buf.at[my_id], **common),
                    pltpu.make_async_remote_copy(dst_ref=buf.at[_flat_id(peer)], **common))

        def gather_copy(p):
            peer = _peer_coords(me, p)
            common = dict(src_ref=out_ref.at[rows(my_id)], send_sem=send2.at[p - 1],
                          recv_sem=recv2.at[p - 1], device_id=peer, device_id_type=MESH_ID)
            return (pltpu.make_async_remote_copy(dst_ref=out_ref.at[rows(my_id)], **common),
                    pltpu.make_async_remote_copy(dst_ref=out_ref.at[rows(_flat_id(peer))], **common))

        buf[my_id] = in_ref[rows(my_id), :]
        exchange([scatter_copy(p) for p in range(1, N_DEV)])
        acc = buf[0]
        for q in range(1, N_DEV):
            acc = acc + buf[q]
        out_ref[rows(my_id), :] = acc
        exchange([gather_copy(p) for p in range(1, N_DEV)])

    vmem = pl.BlockSpec(memory_space=pltpu.VMEM)
    return pl.pallas_call(
        body,
        in_specs=[vmem],
        out_specs=vmem,
        out_shape=jax.ShapeDtypeStruct((r, c), F32),
        scratch_shapes=[pltpu.VMEM((N_DEV, rc, c), F32)] + [pltpu.SemaphoreType.DMA((N_DEV - 1,))] * 4,
        compiler_params=pltpu.CompilerParams(vmem_limit_bytes=VMEM_LIMIT_BYTES),
        name=name,
    )(flat)


WEIGHT_NAMES = ("mix_norm_g", "w_in", "sink", "conv_dw_w", "conv_dw_b", "conv_ln_g", "conv_ln_b",
                "sgu_ln_g", "sgu_ln_b", "sgu_w", "sgu_b", "w_out", "ffn_norm_g", "w_gate", "w_up",
                "w_down", "final_norm_g")
SHARDED = ("w_in", "conv_dw_w", "w_out", "w_gate", "w_up", "w_down")
SMALL = tuple(n for n in WEIGHT_NAMES if n not in ("w_in", "w_out", "w_gate", "w_up", "w_down"))


PACKED = SMALL + ("loss",)


def _pack_small(parts):
    flat = jnp.concatenate([parts[n].reshape(-1) for n in PACKED])
    pad = (-flat.shape[0]) % (N_DEV * F32_SUBLANES * LANES)
    return jnp.pad(flat, (0, pad)).reshape(-1, 128)


def _unpack_small(packed, shapes):
    flat = packed.reshape(-1)
    out, pos = {}, 0
    for n in PACKED:
        size = math.prod(shapes[n])
        out[n] = flat[pos:pos + size].reshape(shapes[n])
        pos += size
    return out


def kernel(x, mix_norm_g, w_in, sink, conv_dw_w, conv_dw_b, conv_ln_g, conv_ln_b, sgu_ln_g, sgu_ln_b, sgu_w, sgu_b, w_out, ffn_norm_g, w_gate, w_up, w_down, final_norm_g, loss_target, m_mix_norm_g, m_w_in, m_sink, m_conv_dw_w, m_conv_dw_b, m_conv_ln_g, m_conv_ln_b, m_sgu_ln_g, m_sgu_ln_b, m_sgu_w, m_sgu_b, m_w_out, m_ffn_norm_g, m_w_gate, m_w_up, m_w_down, m_final_norm_g, v_mix_norm_g, v_w_in, v_sink, v_conv_dw_w, v_conv_dw_b, v_conv_ln_g, v_conv_ln_b, v_sgu_ln_g, v_sgu_ln_b, v_sgu_w, v_sgu_b, v_w_out, v_ffn_norm_g, v_w_gate, v_w_up, v_w_down, v_final_norm_g):
    w = dict(mix_norm_g=mix_norm_g, w_in=w_in, sink=sink, conv_dw_w=conv_dw_w, conv_dw_b=conv_dw_b,
             conv_ln_g=conv_ln_g, conv_ln_b=conv_ln_b, sgu_ln_g=sgu_ln_g, sgu_ln_b=sgu_ln_b, sgu_w=sgu_w,
             sgu_b=sgu_b, w_out=w_out, ffn_norm_g=ffn_norm_g, w_gate=w_gate, w_up=w_up, w_down=w_down,
             final_norm_g=final_norm_g)
    mom_m = dict(zip(WEIGHT_NAMES, (m_mix_norm_g, m_w_in, m_sink, m_conv_dw_w, m_conv_dw_b, m_conv_ln_g,
                                    m_conv_ln_b, m_sgu_ln_g, m_sgu_ln_b, m_sgu_w, m_sgu_b, m_w_out,
                                    m_ffn_norm_g, m_w_gate, m_w_up, m_w_down, m_final_norm_g)))
    mom_v = dict(zip(WEIGHT_NAMES, (v_mix_norm_g, v_w_in, v_sink, v_conv_dw_w, v_conv_dw_b, v_conv_ln_g,
                                    v_conv_ln_b, v_sgu_ln_g, v_sgu_ln_b, v_sgu_w, v_sgu_b, v_w_out,
                                    v_ffn_norm_g, v_w_gate, v_w_up, v_w_down, v_final_norm_g)))

    _, t, d = x.shape
    depth = w_in.shape[0]
    cfg = Cfg(d, t)
    ff = w_gate.shape[2] * N_DEV
    my_id = _flat_id(_my_coords())
    xs = x[0]
    target = loss_target[0]

    tm = _pick(t, (1024, 512))
    tr = _pick(t, (256, 128))
    tb = _pick(t, (256, 128))
    tn_in = _pick(cfg.inw, (896, 512, 448))
    tn_ff = _pick(ff, (512, 1408, 704))
    tk_ff = ff
    tk_in = cfg.inw
    tn_d = _pick(d, (512,))
    tk_t = _pick(t, (2048, 1024, 512))
    tm_in = _pick(cfg.inw, (896, 448))
    tm_ff = _pick(ff, (1408, 704))
    tn_dw = _pick(d, (1024,))

    tabs = _rope_tables(t)

    cflat = conv_dw_w.reshape(-1)
    cshard = jnp.pad(cflat, (0, (-cflat.shape[0]) % (8 * 128))).reshape(-1, 128)
    dw_all = _exchange([cshard], scatter=False, name="gather_conv_w")[0]
    dw_all = dw_all.reshape(N_DEV, -1)[:, :cflat.shape[0]].reshape(N_DEV, depth, CONV_KERNEL, -1)
    dw_all = dw_all.transpose(1, 2, 0, 3).reshape(depth, CONV_KERNEL, cfg.conv)
    dw_pad = jnp.pad(dw_all, ((0, 0), (0, 1), (0, 0)))

    def row(v):
        return v.reshape(1, -1)

    first, rest = ("win",), ("wo", "wg", "wu", "wd")

    def gather_begin(l, names, zero):
        shards = dict(win=w_in[l].T, wo=w_out[l], wg=w_gate[l].T, wu=w_up[l].T, wd=w_down[l])
        handle, started = _gather_group_begin([(shards[k] + zero).astype(BF16) for k in names],
                                              name=f"gather_{names[0]}_{l}")
        return handle, zero + started

    def gather_end(handle, names, after, l):
        full = _gather_group_end(handle, after, name=f"gather_{names[0]}_{l}")
        return {k: f.reshape(-1, d) for k, f in zip(names, full)}

    saved = []
    scatters = [None] * depth
    dw_pad, started = lax.optimization_barrier((dw_pad, jnp.zeros((), F32)))
    landing_first, started = gather_begin(0, first, started)
    landing_wo, started = gather_begin(0, rest[:1], started)
    landing_ffn, started = gather_begin(0, rest[1:], started)
    relayed = None
    for l in range(depth):
        h = _rms_fwd(xs, row(mix_norm_g[l]) + started, tr=tr, name="mix_norm")
        if l == 0:
            relayed, _ = _gather_group_relay(landing_first, h, name="gather_win_0")
            wts = gather_end(relayed, first, h, 0)
        else:
            wts = gather_end(relayed, first + rest, h, l)
        win_t = wts["win"]
        started = jnp.zeros((), F32)
        if l + 1 < depth:
            win_t, started = lax.optimization_barrier((win_t, started))
            landing_next, started = gather_begin(l + 1, first + rest, started)
            h, started = lax.optimization_barrier((h, started))
        z = _in_proj(h, win_t, tabs, cfg, tm=tm, tn=tn_in, name="in_proj")
        qk = z
        mix = _attn_fwd(qk, z, sink[l], cfg, name="attn_fwd")
        mix, conv_y = _conv_fwd(z, mix, dw_pad[l], row(conv_dw_b[l]), row(conv_ln_g[l]) + started,
                                row(conv_ln_b[l]), cfg, tb=tb, name="conv_fwd")
        ws_b = sgu_w[l].astype(BF16)
        bs_c = sgu_b[l][:, :, None]
        mix = _sgu_fwd(z, mix, row(sgu_ln_g[l]), row(sgu_ln_b[l]), ws_b, bs_c, cfg, name="sgu_fwd")
        if l == 0:
            relayed, _ = _gather_group_relay(landing_wo, mix, name="gather_wo_0")
            wts.update(gather_end(relayed, rest[:1], mix, 0))
        x1 = _matmul(mix, wts["wo"], mode="nn", tm=tm, tn=tn_d, tk=d, epilogue=_ep_add, extras=(xs,),
                     out_dtypes=[F32], name="out_proj")[0]
        h2 = _rms_fwd(x1, row(ffn_norm_g[l]), tr=tr, name="ffn_norm")
        if l == 0:
            relayed, _ = _gather_group_relay(landing_ffn, h2, name="gather_wg_0")
            wts.update(gather_end(relayed, rest[1:], h2, 0))
        wo, wg_t, wu_t, wd = wts["wo"], wts["wg"], wts["wu"], wts["wd"]
        gate, up, act = _ffn_up(h2, wg_t, wu_t, tm=tm, tn=tn_ff, name="ffn_up")
        if l + 1 < depth:
            relayed, zero = _gather_group_relay(landing_next, act, name=f"gather_win_{l + 1}")
            act, zero = lax.optimization_barrier((act, zero))
            started = started + zero
        x2 = _matmul(act, wd, mode="nn", tm=tm, tn=tn_d, tk=tk_ff, epilogue=_ep_add, extras=(x1,),
                     out_dtypes=[F32], name="ffn_down")[0]
        saved.append(dict(x0=xs, h=h, z=z, qk=qk, mix=mix, conv_y=conv_y, x1=x1, h2=h2, gate=gate, up=up,
                          act=act, win_t=win_t, wg_t=wg_t, wu_t=wu_t, wo=wo, wd=wd, ws_b=ws_b, bs_c=bs_c))
        xs = x2

    dx, dxb, head = _loss_head(xs, row(final_norm_g), target, tr=tr, name="loss_head")

    def scatter_begin(grad, n, l):
        handle, zero = _exchange_begin(grad.reshape(N_DEV, -1, d), scatter=True, name=f"scatter_{n}_{l}")
        scatters[l][n] = handle
        return zero

    small = {n: [None] * depth for n in SMALL if n != "final_norm_g"}
    big = {n: [None] * depth for n in ("w_in", "w_out", "w_gate", "w_up", "w_down")}
    for l in reversed(range(depth)):
        s = saved[l]
        dgate, dup = _matmul(dxb, s["wd"], mode="nt", tm=tm, tn=tn_ff, tk=d, epilogue=_ep_swiglu_bwd,
                             extras=(s["gate"], s["up"]), out_dtypes=[BF16, BF16], name="ffn_down_bwd")
        dwd = _matmul(s["act"], dxb, mode="tn", tm=tm_ff, tn=tn_dw, tk=tk_t, epilogue=_ep_plain,
                      out_dtypes=[BF16], name="ffn_down_wgrad")[0]
        scatters[l] = {}
        started = scatter_begin(dwd, "w_down", l)
        dh2 = _matmul(dgate, s["wg_t"], mode="nn", tm=tm, tn=tn_d, tk=tk_ff, epilogue=_ep_plain,
                      out_dtypes=[F32], name="ffn_gate_bwd")[0]
        dh2 = _matmul(dup, s["wu_t"], mode="nn", tm=tm, tn=tn_d, tk=tk_ff, epilogue=_ep_add, extras=(dh2,),
                      out_dtypes=[BF16], name="ffn_up_bwd")[0]
        dwg_t = _matmul(dgate, s["h2"], mode="tn", tm=tm_ff, tn=tn_dw, tk=tk_t, epilogue=_ep_plain,
                        out_dtypes=[BF16], name="ffn_gate_wgrad")[0]
        dwu_t = _matmul(dup, s["h2"], mode="tn", tm=tm_ff, tn=tn_dw, tk=tk_t, epilogue=_ep_plain,
                        out_dtypes=[BF16], name="ffn_up_wgrad")[0]
        started = started + scatter_begin(dwg_t, "w_gate", l) + scatter_begin(dwu_t, "w_up", l)
        dx1, dx1b, dg2 = _rms_bwd(dh2, s["x1"], row(ffn_norm_g[l]) + started, dx, tr=tr, name="ffn_norm_bwd")

        dmix = _matmul(dx1b, s["wo"], mode="nt", tm=tm, tn=tn_d, tk=d, epilogue=_ep_plain,
                       out_dtypes=[BF16], name="out_proj_bwd")[0]
        dwo = _matmul(s["mix"], dx1b, mode="tn", tm=_pick(d, (1024,)), tn=tn_dw, tk=tk_t, epilogue=_ep_plain,
                      out_dtypes=[BF16], name="out_proj_wgrad")[0]
        started = scatter_begin(dwo, "w_out", l)
        dq, dk_acc, dv_acc, dsink = _attn_bwd(s["qk"], s["z"], dmix, sink[l], tabs, cfg, name="attn_bwd")
        dk, dv = _kv_finish(dk_acc, dv_acc, tabs, cfg, name="attn_bwd_kv")
        da, dcg, dcw, cst = _conv_bwd(s["z"], s["conv_y"], dmix, dw_pad[l], row(conv_ln_g[l]) + started,
                                      row(conv_ln_b[l]), cfg, tb=tb, name="conv_bwd")
        duv, dws, dbs, sst = _sgu_bwd(s["z"], dmix, row(sgu_ln_g[l]), row(sgu_ln_b[l]), s["ws_b"], s["bs_c"],
                                      cfg, name="sgu_bwd")
        dz = jnp.concatenate([dq, dk, dv, da, dcg, duv], axis=1)
        dh = _matmul(dz, s["win_t"], mode="nn", tm=tm, tn=tn_d, tk=tk_in, epilogue=_ep_plain,
                     out_dtypes=[BF16], name="in_proj_bwd")[0]
        dwin_t = _matmul(dz, s["h"], mode="tn", tm=tm_in, tn=tn_dw, tk=tk_t, epilogue=_ep_plain,
                         out_dtypes=[BF16], name="in_proj_wgrad")[0]
        started = scatter_begin(dwin_t, "w_in", l)
        dx, dxb, dg1 = _rms_bwd(dh, s["x0"], row(mix_norm_g[l]) + started, dx1, tr=tr, name="mix_norm_bwd")

        small["mix_norm_g"][l] = dg1[0]
        small["ffn_norm_g"][l] = dg2[0]
        small["sink"][l] = dsink[:, :Q_PER_KV, 0].reshape(-1)
        small["conv_dw_w"][l] = dcw[:CONV_KERNEL]
        small["conv_dw_b"][l] = cst[0]
        small["conv_ln_g"][l] = cst[1]
        small["conv_ln_b"][l] = cst[2]
        small["sgu_ln_g"][l] = sst[0]
        small["sgu_ln_b"][l] = sst[1]
        small["sgu_w"][l] = dws
        small["sgu_b"][l] = dbs[:, :, 0]

    grads, deltas, new_m, new_v = {}, {}, {}, {}

    transposed = ("w_in", "w_gate", "w_up")

    def adamw(n, grad):
        flip = (lambda a: jnp.swapaxes(a, 1, 2)) if n in transposed else (lambda a: a)
        shape = flip(w[n]).shape
        view = lambda a: flip(a).reshape(-1, shape[-1])
        out = _adamw(view(w[n]), grad.reshape(-1, shape[-1]), view(mom_m[n]), view(mom_v[n]), name="adamw")
        grads[n] = flip(grad.reshape(shape))
        deltas[n], new_m[n], new_v[n] = [flip(o.reshape(shape)) for o in out]

    after = dx
    for n in ("w_down", "w_gate", "w_up", "w_out", "w_in"):
        for l in reversed(range(depth)):
            own, landed = _exchange_wait(scatters[l][n], after, scatter=True, name=f"scatter_{n}_{l}_wait")
            big[n][l] = _sum_shards(own, landed, name="sum_grads")
        adamw(n, jnp.stack(big[n]))
        after = new_v[n]

    parts = {n: jnp.stack(v) for n, v in small.items()}
    parts["final_norm_g"] = head[0]
    parts["loss"] = head[1, :1]
    shapes = {n: parts[n].shape for n in PACKED}
    summed = _unpack_small(_allreduce_small(_pack_small(parts), name="allreduce_small"), shapes)
    loss = summed["loss"][0]
    cshard_w = conv_dw_w.shape[2]
    summed["conv_dw_w"] = lax.dynamic_slice_in_dim(summed["conv_dw_w"], my_id * cshard_w, cshard_w, axis=2)
    for n in SMALL:
        adamw(n, summed[n])

    return (loss, dx[None], *[grads[n] for n in WEIGHT_NAMES], *[deltas[n] for n in WEIGHT_NAMES],
            *[new_m[n] for n in WEIGHT_NAMES], *[new_v[n] for n in WEIGHT_NAMES])
```

```python
import functools
import math

import jax
import jax.numpy as jnp
from jax import lax
from jax.experimental import pallas as pl
from jax.experimental.pallas import tpu as pltpu

F32 = jnp.float32
BF16 = jnp.bfloat16

HEAD_DIM = 128
Q_PER_KV = 4
WINDOW = 128
BLOCK = 128
ROT_DIM = 32
ROPE_THETA = 500000.0
CONV_KERNEL = 31
CONV_PAD = (CONV_KERNEL - 1) // 2
CHUNK = 128
EPS = 1e-6

ADAM_LR = 0.001
ADAM_B1 = 0.9
ADAM_B2 = 0.999
ADAM_EPS = 1e-08
ADAM_WD = 0.01
ADAM_STEP = 10

N_DEV = 8
MESH_AXES = ("x", "y", "c")
VMEM_LIMIT_BYTES = 56 * 1024 * 1024
HALO_ROWS = 16
KV_PAD = 512
ADAMW_TILE_ELEMS = 256 * 1024
MESH_ID = pl.DeviceIdType.MESH


class Cfg:
    def __init__(self, d_model, seq):
        self.d = d_model
        self.t = seq
        self.attn = d_model // 2
        self.hq = self.attn // HEAD_DIM
        self.g = self.hq // Q_PER_KV
        self.kv = self.g * HEAD_DIM
        self.conv = d_model // 4
        self.sgu = d_model // 4
        self.sh = self.sgu // HEAD_DIM
        self.off_k = self.attn
        self.off_v = self.attn + self.kv
        self.off_ca = self.attn + 2 * self.kv
        self.off_cg = self.off_ca + self.conv
        self.off_u = self.off_cg + self.conv
        self.off_sv = self.off_u + self.sgu
        self.inw = self.off_sv + self.sgu


def _pick(dim, prefs):
    for p in prefs:
        if dim % p == 0:
            return p
    return dim


def _cparams(*sem):
    return pltpu.CompilerParams(dimension_semantics=sem, vmem_limit_bytes=VMEM_LIMIT_BYTES)


def _sigmoid(v):
    return 0.5 * jnp.tanh(0.5 * v) + 0.5


_DN = {
    "nn": (((1,), (0,)), ((), ())),
    "nt": (((1,), (1,)), ((), ())),
    "tn": (((0,), (0,)), ((), ())),
}


def _dot(a, b, mode):
    return lax.dot_general(a, b, _DN[mode], preferred_element_type=F32)


def _matmul(a, b, *, mode, tm, tn, tk, epilogue, out_dtypes, extras=(), name):
    if mode == "tn":
        kdim, m = a.shape
        n = b.shape[1]
    elif mode == "nn":
        m, kdim = a.shape
        n = b.shape[1]
    else:
        m, kdim = a.shape
        n = b.shape[0]
    assert m % tm == 0 and n % tn == 0 and kdim % tk == 0, (name, m, n, kdim, tm, tn, tk)
    gm, gn, gk = m // tm, n // tn, kdim // tk
    if mode == "tn":
        a_spec = pl.BlockSpec((tk, tm), lambda i, j, k: (k, i))
    else:
        a_spec = pl.BlockSpec((tm, tk), lambda i, j, k: (i, k))
    if mode == "nt":
        b_spec = pl.BlockSpec((tn, tk), lambda i, j, k: (j, k))
    else:
        b_spec = pl.BlockSpec((tk, tn), lambda i, j, k: (k, j))
    tile = pl.BlockSpec((tm, tn), lambda i, j, k: (i, j))
    ne, no = len(extras), len(out_dtypes)

    def body(a_ref, b_ref, *rest):
        ex = rest[:ne]
        outs = rest[ne:ne + no]

        def finish(acc):
            vals = epilogue(acc, *[e[...] for e in ex])
            for o_ref, val in zip(outs, vals):
                o_ref[...] = val.astype(o_ref.dtype)

        part = _dot(a_ref[...], b_ref[...], mode)
        if gk == 1:
            finish(part)
        else:
            acc_ref = rest[ne + no]
            k = pl.program_id(2)

            @pl.when(k == 0)
            def _():
                acc_ref[...] = part

            if gk > 2:
                @pl.when((k > 0) & (k < gk - 1))
                def _():
                    acc_ref[...] += part

            @pl.when(k == gk - 1)
            def _():
                finish(acc_ref[...] + part)

    return pl.pallas_call(
        body,
        grid=(gm, gn, gk),
        in_specs=[a_spec, b_spec] + [tile] * ne,
        out_specs=[tile] * no,
        out_shape=[jax.ShapeDtypeStruct((m, n), dt) for dt in out_dtypes],
        scratch_shapes=[pltpu.VMEM((tm, tn), F32)] if gk > 1 else [],
        compiler_params=_cparams("parallel", "parallel", "arbitrary"),
        name=name,
    )(a, b, *extras)


def _ep_plain(acc):
    return (acc,)


def _ep_add(acc, r):
    return (r.astype(F32) + acc,)


def _ep_swiglu_bwd(dact, gate, up):
    gate = gate.astype(F32)
    up = up.astype(F32)
    s = _sigmoid(gate)
    silu = gate * s
    dgate = dact * up * (s * (1.0 + gate * (1.0 - s)))
    dup = dact * silu
    return dgate, dup


def _ffn_up(h, wgt, wut, *, tm, tn, name):
    m, kdim = h.shape
    n = wgt.shape[0]

    def body(h_ref, g_ref, u_ref, gate_ref, up_ref, act_ref):
        hv = h_ref[...]
        gate = _dot(hv, g_ref[...], "nt")
        up = _dot(hv, u_ref[...], "nt")
        gate_ref[...] = gate.astype(BF16)
        up_ref[...] = up.astype(BF16)
        act_ref[...] = (gate * _sigmoid(gate) * up).astype(BF16)

    tile = pl.BlockSpec((tm, tn), lambda i, j: (i, j))
    wspec = pl.BlockSpec((tn, kdim), lambda i, j: (j, 0))
    return pl.pallas_call(
        body,
        grid=(m // tm, n // tn),
        in_specs=[pl.BlockSpec((tm, kdim), lambda i, j: (i, 0)), wspec, wspec],
        out_specs=[tile] * 3,
        out_shape=[jax.ShapeDtypeStruct((m, n), BF16)] * 3,
        compiler_params=_cparams("parallel", "parallel"),
        name=name,
    )(h, wgt, wut)


def _rms_fwd(x, g, *, tr, name):
    t, d = x.shape

    def body(x_ref, g_ref, h_ref):
        xv = x_ref[...]
        r = lax.rsqrt(jnp.mean(xv * xv, axis=-1, keepdims=True) + EPS)
        h_ref[...] = (xv * r * g_ref[...]).astype(BF16)

    row = pl.BlockSpec((tr, d), lambda i: (i, 0))
    return pl.pallas_call(
        body,
        grid=(t // tr,),
        in_specs=[row, pl.BlockSpec((1, d), lambda i: (0, 0))],
        out_specs=row,
        out_shape=jax.ShapeDtypeStruct((t, d), BF16),
        compiler_params=_cparams("parallel"),
        name=name,
    )(x, g)


def _rms_bwd_math(dy, xv, g):
    r = lax.rsqrt(jnp.mean(xv * xv, axis=-1, keepdims=True) + EPS)
    xh = xv * r
    dg = jnp.sum(dy * xh, axis=0, keepdims=True)
    dyg = dy * g
    dx = r * (dyg - xh * jnp.mean(dyg * xh, axis=-1, keepdims=True))
    return dx, dg


def _rms_bwd(dh, x, g, dres, *, tr, name):
    t, d = x.shape

    def body(dh_ref, x_ref, g_ref, dres_ref, dx_ref, dxb_ref, dg_ref):
        dx, dg = _rms_bwd_math(dh_ref[...].astype(F32), x_ref[...], g_ref[...])
        dx = dx + dres_ref[...]
        dx_ref[...] = dx
        dxb_ref[...] = dx.astype(BF16)

        @pl.when(pl.program_id(0) == 0)
        def _():
            dg_ref[...] = jnp.zeros_like(dg_ref)

        dg_ref[0:1, :] += dg

    row = pl.BlockSpec((tr, d), lambda i: (i, 0))
    vec = pl.BlockSpec((1, d), lambda i: (0, 0))
    return pl.pallas_call(
        body,
        grid=(t // tr,),
        in_specs=[row, row, vec, row],
        out_specs=[row, row, pl.BlockSpec((8, d), lambda i: (0, 0))],
        out_shape=[jax.ShapeDtypeStruct((t, d), F32), jax.ShapeDtypeStruct((t, d), BF16),
                   jax.ShapeDtypeStruct((8, d), F32)],
        compiler_params=_cparams("arbitrary"),
        name=name,
    )(dh, x, g, dres)


def _loss_head(x, g, target, *, tr, name):
    t, d = x.shape

    def body(x_ref, g_ref, t_ref, dx_ref, dxb_ref, st_ref):
        xv = x_ref[...]
        gv = g_ref[...]
        r = lax.rsqrt(jnp.mean(xv * xv, axis=-1, keepdims=True) + EPS)
        err = xv * r * gv - t_ref[...]
        sq = jnp.sum(jnp.sum(err * err, axis=1, keepdims=True), axis=0, keepdims=True)
        dx, dg = _rms_bwd_math(err * (1.0 / d), xv, gv)
        dx_ref[...] = dx
        dxb_ref[...] = dx.astype(BF16)

        @pl.when(pl.program_id(0) == 0)
        def _():
            st_ref[...] = jnp.zeros_like(st_ref)

        st_ref[0:1, :] += dg
        st_ref[1:2, :] += jnp.broadcast_to(sq * (0.5 / d), (1, d))

    row = pl.BlockSpec((tr, d), lambda i: (i, 0))
    return pl.pallas_call(
        body,
        grid=(t // tr,),
        in_specs=[row, pl.BlockSpec((1, d), lambda i: (0, 0)), row],
        out_specs=[row, row, pl.BlockSpec((8, d), lambda i: (0, 0))],
        out_shape=[jax.ShapeDtypeStruct((t, d), F32), jax.ShapeDtypeStruct((t, d), BF16),
                   jax.ShapeDtypeStruct((8, d), F32)],
        compiler_params=_cparams("arbitrary"),
        name=name,
    )(x, g, target)


def _rope_tables(t):
    half = ROT_DIM // 2
    pos = jnp.arange(t, dtype=F32)
    inv = ROPE_THETA ** (-jnp.arange(0, ROT_DIM, 2, dtype=F32) / ROT_DIM)
    ang = pos[:, None] * inv[None, :]
    cos, sin = jnp.cos(ang), jnp.sin(ang)
    rest = HEAD_DIM - ROT_DIM
    c = jnp.concatenate([cos, cos, jnp.ones((t, rest), F32)], axis=1)
    sa = jnp.concatenate([-sin, jnp.zeros((t, HEAD_DIM - half), F32)], axis=1)
    sb = jnp.concatenate([jnp.zeros((t, half), F32), sin, jnp.zeros((t, rest), F32)], axis=1)
    return c, sa, sb


def _rope_apply(v, c, sa, sb):
    half = ROT_DIM // 2
    return v * c + pltpu.roll(v, HEAD_DIM - half, axis=1) * sa + pltpu.roll(v, half, axis=1) * sb


def _rope_apply_t(dv, c, sa, sb):
    half = ROT_DIM // 2
    return dv * c + pltpu.roll(dv * sa, half, axis=1) + pltpu.roll(dv * sb, HEAD_DIM - half, axis=1)


def _in_proj(h, win_t, tabs, cfg, *, tm, tn, name):
    m, kdim = h.shape
    n = win_t.shape[0]
    per_tile = tn // HEAD_DIM
    n_rot = cfg.hq + cfg.g
    rot_tiles = -(-n_rot // per_tile)

    def body(h_ref, w_ref, c_ref, sa_ref, sb_ref, z_ref):
        j = pl.program_id(1)
        acc = _dot(h_ref[...], w_ref[...], "nt")

        @pl.when(j < rot_tiles)
        def _():
            for b in range(per_tile):
                sl = slice(b * HEAD_DIM, (b + 1) * HEAD_DIM)
                rot = _rope_apply(acc[:, sl], c_ref[...], sa_ref[...], sb_ref[...])
                z_ref[:, sl] = jnp.where(j * per_tile + b < n_rot, rot, acc[:, sl]).astype(BF16)

        @pl.when(j >= rot_tiles)
        def _():
            z_ref[...] = acc.astype(BF16)

    tab = pl.BlockSpec((tm, HEAD_DIM), lambda i, j: (i, 0))
    return pl.pallas_call(
        body,
        grid=(m // tm, n // tn),
        in_specs=[pl.BlockSpec((tm, kdim), lambda i, j: (i, 0)), pl.BlockSpec((tn, kdim), lambda i, j: (j, 0)),
                  tab, tab, tab],
        out_specs=pl.BlockSpec((tm, tn), lambda i, j: (i, j)),
        out_shape=jax.ShapeDtypeStruct((m, n), BF16),
        compiler_params=_cparams("parallel", "parallel"),
        name=name,
    )(h, win_t, *tabs)


def _attn_specs(cfg, nb):
    kcol = cfg.hq
    vcol = cfg.off_v // HEAD_DIM
    qw = Q_PER_KV * HEAD_DIM
    q_spec = pl.BlockSpec((BLOCK, qw), lambda g, n: (n, g))

    def kv(col, shift):
        def idx(g, n):
            return (jnp.clip(n + shift, 0, nb - 1), col + g)
        return pl.BlockSpec((BLOCK, HEAD_DIM), idx)

    k_specs = [kv(kcol, s) for s in (-1, 0, 1)]
    v_specs = [kv(vcol, s) for s in (-1, 0, 1)]
    return q_spec, k_specs, v_specs


def _attn_probs(q, k, valid, sk):
    scale = 1.0 / math.sqrt(HEAD_DIM)
    s = _dot(q, k, "nt") * scale
    s = jnp.where(valid, s, jnp.finfo(F32).min)
    m = jnp.maximum(jnp.max(s, axis=1, keepdims=True), sk)
    e = jnp.exp(s - m)
    es = jnp.exp(sk - m)
    inv = 1.0 / (jnp.sum(e, axis=1, keepdims=True) + es)
    return e * inv, es * inv


def _attn_valid(n, t):
    shape = (Q_PER_KV * BLOCK, 3 * BLOCK)
    qpos = n * BLOCK + (lax.broadcasted_iota(jnp.int32, shape, 0) & (BLOCK - 1))
    kpos = (n - 1) * BLOCK + lax.broadcasted_iota(jnp.int32, shape, 1)
    return (kpos >= 0) & (kpos < t) & (jnp.abs(qpos - kpos) <= WINDOW)


def _stack_heads(ref):
    return jnp.concatenate([ref[:, r * HEAD_DIM:(r + 1) * HEAD_DIM] for r in range(Q_PER_KV)], axis=0)


def _sink_column(sink_ref, g):
    head = lax.broadcasted_iota(jnp.int32, (Q_PER_KV * BLOCK, 1), 0) // BLOCK
    col = jnp.full((Q_PER_KV * BLOCK, 1), sink_ref[g * Q_PER_KV], F32)
    for r in range(1, Q_PER_KV):
        col = jnp.where(head == r, sink_ref[g * Q_PER_KV + r], col)
    return col


def _attn_fwd(qk, z, sink, cfg, *, name):
    t = qk.shape[0]
    nb = t // BLOCK
    q_spec, k_specs, v_specs = _attn_specs(cfg, nb)

    def body(sink_ref, q_ref, kp, kc, kn, vp, vc, vn, o_ref):
        g = pl.program_id(0)
        n = pl.program_id(1)
        k = jnp.concatenate([kp[...], kc[...], kn[...]], axis=0)
        v = jnp.concatenate([vp[...], vc[...], vn[...]], axis=0)
        p, _ = _attn_probs(_stack_heads(q_ref), k, _attn_valid(n, t), _sink_column(sink_ref, g))
        o = _dot(p.astype(BF16), v, "nn")
        for r in range(Q_PER_KV):
            o_ref[:, r * HEAD_DIM:(r + 1) * HEAD_DIM] = o[r * BLOCK:(r + 1) * BLOCK].astype(BF16)

    return pl.pallas_call(
        body,
        grid=(cfg.g, nb),
        in_specs=[pl.BlockSpec(memory_space=pltpu.SMEM), q_spec] + k_specs + v_specs,
        out_specs=q_spec,
        out_shape=jax.ShapeDtypeStruct((t, cfg.d), BF16),
        compiler_params=_cparams("parallel", "parallel"),
        name=name,
    )(sink, qk, qk, qk, qk, z, z, z)


def _attn_bwd(qk, z, dmix, sink, tabs, cfg, *, name):
    t = qk.shape[0]
    nb = t // BLOCK
    q_spec, k_specs, v_specs = _attn_specs(cfg, nb)
    tab = pl.BlockSpec((BLOCK, HEAD_DIM), lambda g, n: (n, 0))
    acc_spec = pl.BlockSpec((None, t + 2 * KV_PAD, HEAD_DIM), lambda g, n: (g, 0, 0))
    scale = 1.0 / math.sqrt(HEAD_DIM)

    def body(sink_ref, q_ref, kp, kc, kn, vp, vc, vn, do_ref, c_ref, sa_ref, sb_ref,
             dq_ref, dk_ref, dv_ref, ds_ref):
        g = pl.program_id(0)
        n = pl.program_id(1)

        @pl.when(n == 0)
        def _():
            dk_ref[...] = jnp.zeros_like(dk_ref)
            dv_ref[...] = jnp.zeros_like(dv_ref)
            ds_ref[...] = jnp.zeros_like(ds_ref)

        k = jnp.concatenate([kp[...], kc[...], kn[...]], axis=0)
        v = jnp.concatenate([vp[...], vc[...], vn[...]], axis=0)
        q = _stack_heads(q_ref)
        do = _stack_heads(do_ref)
        p, ps = _attn_probs(q, k, _attn_valid(n, t), _sink_column(sink_ref, g))
        dp = _dot(do, v, "nt")
        delta = jnp.sum(p * dp, axis=1, keepdims=True)
        dsc = (p * (dp - delta) * scale).astype(BF16)
        dq = _dot(dsc, k, "nn")
        dsink = ps * delta
        for r in range(Q_PER_KV):
            rows = slice(r * BLOCK, (r + 1) * BLOCK)
            dq_ref[:, r * HEAD_DIM:(r + 1) * HEAD_DIM] = _rope_apply_t(
                dq[rows], c_ref[...], sa_ref[...], sb_ref[...]).astype(BF16)
            ds_ref[r:r + 1, :] -= jnp.broadcast_to(jnp.sum(dsink[rows], axis=0, keepdims=True), (1, HEAD_DIM))
        rows = pl.ds(pl.multiple_of(n * BLOCK + (KV_PAD - BLOCK), BLOCK), 3 * BLOCK)
        dk_ref[rows, :] += _dot(dsc, q, "tn")
        dv_ref[rows, :] += _dot(p.astype(BF16), do, "tn")

    acc_shape = jax.ShapeDtypeStruct((cfg.g, t + 2 * KV_PAD, HEAD_DIM), F32)
    return pl.pallas_call(
        body,
        grid=(cfg.g, nb),
        in_specs=[pl.BlockSpec(memory_space=pltpu.SMEM), q_spec] + k_specs + v_specs + [q_spec, tab, tab, tab],
        out_specs=[q_spec, acc_spec, acc_spec, pl.BlockSpec((None, 8, HEAD_DIM), lambda g, n: (g, 0, 0))],
        out_shape=[jax.ShapeDtypeStruct((t, cfg.attn), BF16), acc_shape, acc_shape,
                   jax.ShapeDtypeStruct((cfg.g, 8, HEAD_DIM), F32)],
        compiler_params=_cparams("arbitrary", "arbitrary"),
        name=name,
    )(sink, qk, qk, qk, qk, z, z, z, dmix, *tabs)


def _kv_finish(dk_acc, dv_acc, tabs, cfg, *, name):
    t = dk_acc.shape[1] - 2 * KV_PAD

    def body(dk_ref, dv_ref, c_ref, sa_ref, sb_ref, ok_ref, ov_ref):
        ok_ref[...] = _rope_apply_t(dk_ref[...], c_ref[...], sa_ref[...], sb_ref[...]).astype(BF16)
        ov_ref[...] = dv_ref[...].astype(BF16)

    acc = pl.BlockSpec((None, KV_PAD, HEAD_DIM), lambda g, i: (g, i + 1, 0))
    tab = pl.BlockSpec((KV_PAD, HEAD_DIM), lambda g, i: (i, 0))
    out = pl.BlockSpec((KV_PAD, HEAD_DIM), lambda g, i: (i, g))
    return pl.pallas_call(
        body,
        grid=(cfg.g, t // KV_PAD),
        in_specs=[acc, acc, tab, tab, tab],
        out_specs=[out, out],
        out_shape=[jax.ShapeDtypeStruct((t, cfg.kv), BF16)] * 2,
        compiler_params=_cparams("parallel", "parallel"),
        name=name,
    )(dk_acc, dv_acc, *tabs)


def _halo_specs(width, col, tb, t):
    per = tb // HALO_ROWS
    last = t // HALO_ROWS - 1
    prev = pl.BlockSpec((HALO_ROWS, width), lambda i: (jnp.maximum(i * per - 1, 0), col))
    cur = pl.BlockSpec((tb, width), lambda i: (i, col))
    nxt = pl.BlockSpec((HALO_ROWS, width), lambda i: (jnp.minimum((i + 1) * per, last), col))
    return [prev, cur, nxt]


def _halo_load(refs):
    return jnp.concatenate([r[...].astype(F32) for r in refs], axis=0)


F32_SUBLANES = 8


CONV_ROWS = 64
LANES = 128


def _store_phases(ph_ref, v):
    total = v.shape[0]
    ph_ref[0] = v
    for p in range(1, F32_SUBLANES):
        ph_ref[p] = pltpu.roll(v, total - p, axis=0)


def _shifted(ph_ref, start, r0, lanes):
    base = start - start % F32_SUBLANES + r0
    return ph_ref[start % F32_SUBLANES, base:base + CONV_ROWS, lanes]


def _conv_blocks(tb, cw):
    return [(r0, slice(l0, l0 + LANES)) for l0 in range(0, cw, LANES) for r0 in range(0, tb, CONV_ROWS)]


def _conv_glu(a_refs, g_refs, i, tb, t):
    a = _halo_load(a_refs)
    g = _halo_load(g_refs)
    rows = i * tb - HALO_ROWS + lax.broadcasted_iota(jnp.int32, (tb + 2 * HALO_ROWS, 1), 0)
    valid = (rows >= 0) & (rows < t)
    sg = _sigmoid(g)
    return a, sg, jnp.where(valid, a * sg, 0.0), valid


def _conv_fwd(z, mix, w, b, lg, lb, cfg, *, tb, name):
    t = z.shape[0]
    cw = cfg.conv
    vec = pl.BlockSpec((1, cw), lambda i: (0, 0))

    def body(ap, ac, an, gp, gc, gn, w_ref, b_ref, lg_ref, lb_ref, mix_ref, o_ref, y_ref, c_ph):
        _, _, c, _ = _conv_glu((ap, ac, an), (gp, gc, gn), pl.program_id(0), tb, t)
        _store_phases(c_ph, c)
        for r0, lanes in _conv_blocks(tb, cw):
            acc = jnp.zeros((CONV_ROWS, LANES), F32)
            for j in range(CONV_KERNEL):
                acc = acc + w_ref[j:j + 1, lanes] * _shifted(c_ph, j + HALO_ROWS - CONV_PAD, r0, lanes)
            y_ref[r0:r0 + CONV_ROWS, lanes] = acc + b_ref[:, lanes]
        y = y_ref[...]
        mu = jnp.mean(y, axis=-1, keepdims=True)
        dlt = y - mu
        var = jnp.mean(dlt * dlt, axis=-1, keepdims=True)
        yn = dlt * lax.rsqrt(var + EPS) * lg_ref[...] + lb_ref[...]
        o_ref[...] = (yn * _sigmoid(yn)).astype(BF16)

    return pl.pallas_call(
        body,
        grid=(t // tb,),
        in_specs=(_halo_specs(cw, cfg.off_ca // cw, tb, t) + _halo_specs(cw, cfg.off_cg // cw, tb, t)
                  + [pl.BlockSpec((CONV_KERNEL + 1, cw), lambda i: (0, 0)), vec, vec, vec,
                     pl.BlockSpec(memory_space=pl.ANY)]),
        out_specs=[pl.BlockSpec((tb, cw), lambda i: (i, cfg.attn // cw)), pl.BlockSpec((tb, cw), lambda i: (i, 0))],
        out_shape=[jax.ShapeDtypeStruct(mix.shape, BF16), jax.ShapeDtypeStruct((t, cw), F32)],
        input_output_aliases={10: 0},
        scratch_shapes=[pltpu.VMEM((F32_SUBLANES, tb + 2 * HALO_ROWS, cw), F32)],
        compiler_params=_cparams("parallel"),
        name=name,
    )(z, z, z, z, z, z, w, b, lg, lb, mix)


def _conv_bwd(z, y, dmix, w, lg, lb, cfg, *, tb, name):
    t = z.shape[0]
    cw = cfg.conv
    vec = pl.BlockSpec((1, cw), lambda i: (0, 0))
    cen = slice(HALO_ROWS, HALO_ROWS + tb)

    def body(ap, ac, an, gp, gc, gn, yp, yc, yn_, dp, dc_, dn, w_ref, lg_ref, lb_ref,
             da_ref, dg_ref, dw_ref, st_ref, dy_ph, c_ph):
        @pl.when(pl.program_id(0) == 0)
        def _():
            dw_ref[...] = jnp.zeros_like(dw_ref)
            st_ref[...] = jnp.zeros_like(st_ref)

        a, sg, c, valid = _conv_glu((ap, ac, an), (gp, gc, gn), pl.program_id(0), tb, t)
        yv = _halo_load((yp, yc, yn_))
        do = _halo_load((dp, dc_, dn))
        mu = jnp.mean(yv, axis=-1, keepdims=True)
        dlt = yv - mu
        rstd = lax.rsqrt(jnp.mean(dlt * dlt, axis=-1, keepdims=True) + EPS)
        xh = dlt * rstd
        lgv = lg_ref[...]
        yn = xh * lgv + lb_ref[...]
        s = _sigmoid(yn)
        dyn = do * (s * (1.0 + yn * (1.0 - s)))
        dxh = dyn * lgv
        dy = rstd * (dxh - jnp.mean(dxh, axis=-1, keepdims=True)
                     - xh * jnp.mean(dxh * xh, axis=-1, keepdims=True))
        dy = jnp.where(valid, dy, 0.0)
        dyc = dy[cen]
        st_ref[0:1, :] += jnp.sum(dyc, axis=0, keepdims=True)
        st_ref[1:2, :] += jnp.sum((dyn * xh)[cen], axis=0, keepdims=True)
        st_ref[2:3, :] += jnp.sum(dyn[cen], axis=0, keepdims=True)
        _store_phases(dy_ph, dy)
        _store_phases(c_ph, c)
        for r0, lanes in _conv_blocks(tb, cw):
            dc = jnp.zeros((CONV_ROWS, LANES), F32)
            for j in range(CONV_KERNEL):
                dc = dc + w_ref[j:j + 1, lanes] * _shifted(dy_ph, HALO_ROWS + CONV_PAD - j, r0, lanes)
            rows = slice(HALO_ROWS + r0, HALO_ROWS + r0 + CONV_ROWS)
            sgc = sg[rows, lanes]
            da_ref[r0:r0 + CONV_ROWS, lanes] = (dc * sgc).astype(BF16)
            dg_ref[r0:r0 + CONV_ROWS, lanes] = (dc * a[rows, lanes] * sgc * (1.0 - sgc)).astype(BF16)
        for l0 in range(0, cw, LANES):
            lanes = slice(l0, l0 + LANES)
            for j in range(CONV_KERNEL):
                part = jnp.zeros((CONV_ROWS, LANES), F32)
                for r0 in range(0, tb, CONV_ROWS):
                    part = part + (_shifted(dy_ph, HALO_ROWS, r0, lanes)
                                   * _shifted(c_ph, j + HALO_ROWS - CONV_PAD, r0, lanes))
                dw_ref[j:j + 1, lanes] += jnp.sum(part, axis=0, keepdims=True)

    out = pl.BlockSpec((tb, cw), lambda i: (i, 0))
    wspec = pl.BlockSpec((CONV_KERNEL + 1, cw), lambda i: (0, 0))
    return pl.pallas_call(
        body,
        grid=(t // tb,),
        in_specs=(_halo_specs(cw, cfg.off_ca // cw, tb, t) + _halo_specs(cw, cfg.off_cg // cw, tb, t)
                  + _halo_specs(cw, 0, tb, t) + _halo_specs(cw, cfg.attn // cw, tb, t) + [wspec, vec, vec]),
        out_specs=[out, out, wspec, pl.BlockSpec((8, cw), lambda i: (0, 0))],
        out_shape=[jax.ShapeDtypeStruct((t, cw), BF16), jax.ShapeDtypeStruct((t, cw), BF16),
                   jax.ShapeDtypeStruct((CONV_KERNEL + 1, cw), F32), jax.ShapeDtypeStruct((8, cw), F32)],
        scratch_shapes=[pltpu.VMEM((F32_SUBLANES, tb + 2 * HALO_ROWS, cw), F32)] * 2,
        compiler_params=_cparams("arbitrary"),
        name=name,
    )(z, z, z, z, z, z, y, y, y, dmix, dmix, dmix, w, lg, lb)


_SQRT_HALF = 1.0 / math.sqrt(2.0)
_INV_SQRT_2PI = 1.0 / math.sqrt(2.0 * math.pi)


def _gelu(v):
    return 0.5 * v * (1.0 + lax.erf(v * _SQRT_HALF))


def _gelu_grad(v):
    return 0.5 * (1.0 + lax.erf(v * _SQRT_HALF)) + v * jnp.exp(-0.5 * v * v) * _INV_SQRT_2PI


def _sgu_norm(zv_ref, lg_ref, lb_ref):
    xv = zv_ref[...].astype(F32)
    v = _gelu(xv)
    mu = jnp.mean(v, axis=-1, keepdims=True)
    dlt = v - mu
    rstd = lax.rsqrt(jnp.mean(dlt * dlt, axis=-1, keepdims=True) + EPS)
    xh = dlt * rstd
    return xv, xh, rstd, xh * lg_ref[...] + lb_ref[...]


def _sgu_specs(cfg):
    sw = cfg.sgu
    zu = pl.BlockSpec((CHUNK, sw), lambda i: (i, cfg.off_u // sw))
    zv = pl.BlockSpec((CHUNK, sw), lambda i: (i, cfg.off_sv // sw))
    vec = pl.BlockSpec((1, sw), lambda i: (0, 0))
    ws = pl.BlockSpec((cfg.sh, CHUNK, CHUNK), lambda i: (0, 0, 0))
    bs = pl.BlockSpec((cfg.sh, CHUNK, 1), lambda i: (0, 0, 0))
    return zu, zv, vec, ws, bs


def _sgu_fwd(z, mix, lg, lb, ws, bs, cfg, *, name):
    t = z.shape[0]
    sw = cfg.sgu
    zu, zv, vec, wspec, bspec = _sgu_specs(cfg)

    def body(zu_ref, zv_ref, lg_ref, lb_ref, ws_ref, bs_ref, mix_ref, o_ref):
        u = _gelu(zu_ref[...].astype(F32))
        _, _, _, vn = _sgu_norm(zv_ref, lg_ref, lb_ref)
        vnb = vn.astype(BF16)
        for h in range(cfg.sh):
            sl = slice(h * HEAD_DIM, (h + 1) * HEAD_DIM)
            sp = _dot(ws_ref[h], vnb[:, sl], "nn") + bs_ref[h]
            o_ref[:, sl] = (u[:, sl] * sp).astype(BF16)

    return pl.pallas_call(
        body,
        grid=(t // CHUNK,),
        in_specs=[zu, zv, vec, vec, wspec, bspec, pl.BlockSpec(memory_space=pl.ANY)],
        out_specs=pl.BlockSpec((CHUNK, sw), lambda i: (i, (cfg.attn + cfg.conv) // sw)),
        out_shape=jax.ShapeDtypeStruct(mix.shape, BF16),
        input_output_aliases={6: 0},
        compiler_params=_cparams("parallel"),
        name=name,
    )(z, z, lg, lb, ws, bs, mix)


def _sgu_bwd(z, dmix, lg, lb, ws, bs, cfg, *, name):
    t = z.shape[0]
    sw = cfg.sgu
    zu, zv, vec, wspec, bspec = _sgu_specs(cfg)
    do_spec = pl.BlockSpec((CHUNK, sw), lambda i: (i, (cfg.attn + cfg.conv) // sw))

    def body(zu_ref, zv_ref, do_ref, lg_ref, lb_ref, ws_ref, bs_ref, duv_ref, dws_ref, dbs_ref, st_ref):
        @pl.when(pl.program_id(0) == 0)
        def _():
            dws_ref[...] = jnp.zeros_like(dws_ref)
            dbs_ref[...] = jnp.zeros_like(dbs_ref)
            st_ref[...] = jnp.zeros_like(st_ref)

        xu = zu_ref[...].astype(F32)
        u = _gelu(xu)
        xv, xh, rstd, vn = _sgu_norm(zv_ref, lg_ref, lb_ref)
        vnb = vn.astype(BF16)
        do = do_ref[...].astype(F32)
        dvn_parts = []
        for h in range(cfg.sh):
            sl = slice(h * HEAD_DIM, (h + 1) * HEAD_DIM)
            wh = ws_ref[h]
            sp = _dot(wh, vnb[:, sl], "nn") + bs_ref[h]
            dsp = do[:, sl] * u[:, sl]
            dspb = dsp.astype(BF16)
            dvn_parts.append(_dot(wh, dspb, "tn"))
            dws_ref[h] += _dot(dspb, vnb[:, sl], "nt")
            dbs_ref[h] += jnp.sum(dsp, axis=1, keepdims=True)
            duv_ref[:, sl] = (do[:, sl] * sp * _gelu_grad(xu[:, sl])).astype(BF16)
        dvn = jnp.concatenate(dvn_parts, axis=1)
        st_ref[0:1, :] += jnp.sum(dvn * xh, axis=0, keepdims=True)
        st_ref[1:2, :] += jnp.sum(dvn, axis=0, keepdims=True)
        dxh = dvn * lg_ref[...]
        dv = rstd * (dxh - jnp.mean(dxh, axis=-1, keepdims=True)
                     - xh * jnp.mean(dxh * xh, axis=-1, keepdims=True))
        duv_ref[:, sw:] = (dv * _gelu_grad(xv)).astype(BF16)

    return pl.pallas_call(
        body,
        grid=(t // CHUNK,),
        in_specs=[zu, zv, do_spec, vec, vec, wspec, bspec],
        out_specs=[pl.BlockSpec((CHUNK, 2 * sw), lambda i: (i, 0)), wspec, bspec,
                   pl.BlockSpec((8, sw), lambda i: (0, 0))],
        out_shape=[jax.ShapeDtypeStruct((t, 2 * sw), BF16), jax.ShapeDtypeStruct((cfg.sh, CHUNK, CHUNK), F32),
                   jax.ShapeDtypeStruct((cfg.sh, CHUNK, 1), F32), jax.ShapeDtypeStruct((8, sw), F32)],
        compiler_params=_cparams("arbitrary"),
        name=name,
    )(z, z, dmix, lg, lb, ws, bs)


def _sum_shards(own, landed, *, name):
    _, r, c = own.shape
    my_id = _flat_id(_my_coords()).astype(jnp.int32).reshape(1)
    tr = _pick(r, (64, 32, 16))

    def body(me_ref, *refs):
        o_ref = refs[N_DEV]
        acc = refs[0][...].astype(F32)
        for p_ref in refs[1:N_DEV]:
            acc = acc + p_ref[...].astype(F32)
        o_ref[...] = acc

    def slab(p):
        return pl.BlockSpec((None, tr, c), lambda i, me: (me[0] ^ p, i, 0))

    return pl.pallas_call(
        body,
        grid_spec=pltpu.PrefetchScalarGridSpec(
            num_scalar_prefetch=1,
            grid=(r // tr,),
            in_specs=[slab(p) for p in range(N_DEV)],
            out_specs=pl.BlockSpec((tr, c), lambda i, me: (i, 0)),
        ),
        out_shape=jax.ShapeDtypeStruct((r, c), F32),
        compiler_params=_cparams("parallel"),
        name=name,
    )(my_id, own, *([landed] * (N_DEV - 1)))


def _adamw(w, g, m, v, *, name):
    r, c = w.shape
    tr = _pick(r, [p for p in (1024, 512, 256, 128, 64, 32, 16, 8) if p * c <= ADAMW_TILE_ELEMS])

    def body(w_ref, g_ref, m_ref, v_ref, d_ref, nm_ref, nv_ref):
        gv = g_ref[...]
        nm = ADAM_B1 * m_ref[...] + (1.0 - ADAM_B1) * gv
        nv = ADAM_B2 * v_ref[...] + (1.0 - ADAM_B2) * (gv * gv)
        m_hat = nm / (1.0 - ADAM_B1 ** ADAM_STEP)
        v_hat = nv / (1.0 - ADAM_B2 ** ADAM_STEP)
        d_ref[...] = -ADAM_LR * (m_hat / (jnp.sqrt(v_hat) + ADAM_EPS) + ADAM_WD * w_ref[...])
        nm_ref[...] = nm
        nv_ref[...] = nv

    blk = pl.BlockSpec((tr, c), lambda i: (i, 0))
    return pl.pallas_call(
        body,
        grid=(r // tr,),
        in_specs=[blk] * 4,
        out_specs=[blk] * 3,
        out_shape=[jax.ShapeDtypeStruct((r, c), F32)] * 3,
        compiler_params=_cparams("parallel"),
        name=name,
    )(w, g, m, v)


def _my_coords():
    return tuple(lax.axis_index(a) for a in MESH_AXES)


def _peer_coords(me, p):
    return tuple(1 - v if (p >> (2 - a)) & 1 else v for a, v in enumerate(me))


def _flat_id(coords):
    return 4 * coords[0] + 2 * coords[1] + coords[2]


def _exchange(arrs, *, scatter, name):
    na = len(arrs)

    def body(*refs):
        ins = refs[:na]
        outs = refs[na:2 * na]
        send_sems, recv_sems, local_sems = refs[2 * na:]
        me = _my_coords()
        my_id = _flat_id(me)

        local = []
        for k in range(na):
            src = ins[k].at[my_id] if scatter else ins[k]
            cp = pltpu.make_async_copy(src, outs[k].at[my_id], local_sems.at[k])
            cp.start()
            local.append(cp)

        def remote(p, k):
            peer = _peer_coords(me, p)
            peer_id = _flat_id(peer)
            sem = (p - 1) * na + k
            src = ins[k].at[peer_id] if scatter else ins[k]
            send = pltpu.make_async_remote_copy(
                src_ref=src, dst_ref=outs[k].at[my_id], send_sem=send_sems.at[sem],
                recv_sem=recv_sems.at[sem], device_id=peer, device_id_type=MESH_ID)
            recv = pltpu.make_async_remote_copy(
                src_ref=src, dst_ref=outs[k].at[peer_id], send_sem=send_sems.at[sem],
                recv_sem=recv_sems.at[sem], device_id=peer, device_id_type=MESH_ID)
            return send, recv

        pairs = [remote(p, k) for p in range(1, N_DEV) for k in range(na)]
        for send, _ in pairs:
            send.start()
        for _, recv in pairs:
            recv.wait_recv()
        for send, _ in pairs:
            send.wait_send()
        for cp in local:
            cp.wait()

    def out_of(a):
        return jax.ShapeDtypeStruct(a.shape if scatter else (N_DEV,) + a.shape, a.dtype)

    hbm = pl.BlockSpec(memory_space=pl.ANY)
    nsem = (N_DEV - 1) * na
    return pl.pallas_call(
        body,
        in_specs=[hbm] * na,
        out_specs=[hbm] * na,
        out_shape=[out_of(a) for a in arrs],
        scratch_shapes=[pltpu.SemaphoreType.DMA((nsem,)), pltpu.SemaphoreType.DMA((nsem,)),
                        pltpu.SemaphoreType.DMA((na,))],
        name=name,
    )(*arrs)


_HBM = pl.BlockSpec(memory_space=pltpu.HBM)
_SEM = pl.BlockSpec(memory_space=pltpu.SEMAPHORE)
_EFFECT = pltpu.SideEffectType.DATAFLOW_SIDE_EFFECTING


def _place_own(land, src, *, scatter, name):
    _, r, c = land.shape
    tr = _pick(r, (256, 128, 64, 32, 16))
    my_id = _flat_id(_my_coords()).astype(jnp.int32).reshape(1)

    def body(me_ref, land_ref, src_ref, out_ref):
        out_ref[...] = src_ref[...]

    if scatter:
        src_spec = pl.BlockSpec((None, tr, c), lambda i, me: (me[0], i, 0))
    else:
        src_spec = pl.BlockSpec((tr, c), lambda i, me: (i, 0))
    return pl.pallas_call(
        body,
        grid_spec=pltpu.PrefetchScalarGridSpec(
            num_scalar_prefetch=1,
            grid=(r // tr,),
            in_specs=[pl.BlockSpec(memory_space=pl.ANY), src_spec],
            out_specs=pl.BlockSpec((None, tr, c), lambda i, me: (me[0], i, 0)),
        ),
        out_shape=jax.ShapeDtypeStruct(land.shape, land.dtype),
        input_output_aliases={1: 0},
        compiler_params=_cparams("parallel"),
        name=name,
    )(my_id, land, src)


def _peer_copy(src_ref, land_ref, send_sems, recv_sems, me, p, scatter, arrival):
    peer = _peer_coords(me, p)
    peer_id = _flat_id(peer)
    return pltpu.make_async_remote_copy(
        src_ref=src_ref.at[peer_id] if scatter else src_ref,
        dst_ref=land_ref.at[peer_id if arrival else _flat_id(me)],
        send_sem=send_sems.at[p - 1], recv_sem=recv_sems.at[p - 1], device_id=peer, device_id_type=MESH_ID)


def _exchange_start(src, land, *, scatter, name):
    def body(src_ref, land_ref, send_sems, recv_sems, src_thru, land_thru, token):
        me = _my_coords()
        for p in range(1, N_DEV):
            _peer_copy(src_ref, land_ref, send_sems, recv_sems, me, p, scatter, False).start()
        token[...] = jnp.zeros_like(token)

    nsem = N_DEV - 1
    return pl.pallas_call(
        body,
        name=name,
        out_shape=(pltpu.SemaphoreType.DMA((nsem,)), pltpu.SemaphoreType.DMA((nsem,)),
                   pltpu.HBM(src.shape, src.dtype), pltpu.HBM(land.shape, land.dtype),
                   jax.ShapeDtypeStruct((8, 128), F32)),
        in_specs=(_HBM, _HBM),
        out_specs=(_SEM, _SEM, _HBM, _HBM, pl.BlockSpec(memory_space=pltpu.VMEM)),
        input_output_aliases={0: 2, 1: 3},
        compiler_params=pltpu.CompilerParams(has_side_effects=_EFFECT),
    )(pltpu.with_memory_space_constraint(src, pltpu.HBM), pltpu.with_memory_space_constraint(land, pltpu.HBM))


def _exchange_wait(handle, after, *, scatter, name):
    send_sems, recv_sems, src_thru, land_thru = handle

    def body(src_ref, land_ref, send_sems, recv_sems, after_ref, src_dead, got_ref):
        me = _my_coords()
        for p in range(1, N_DEV):
            _peer_copy(src_ref, land_ref, send_sems, recv_sems, me, p, scatter, False).wait_send()
            _peer_copy(src_ref, land_ref, send_sems, recv_sems, me, p, scatter, True).wait_recv()

    return pl.pallas_call(
        body,
        name=name,
        out_shape=(pltpu.HBM(src_thru.shape, src_thru.dtype), pltpu.HBM(land_thru.shape, land_thru.dtype)),
        in_specs=(_HBM, _HBM, _SEM, _SEM, pl.BlockSpec(memory_space=pl.ANY)),
        out_specs=(_HBM, _HBM),
        input_output_aliases={0: 0, 1: 1},
        compiler_params=pltpu.CompilerParams(has_side_effects=_EFFECT),
    )(src_thru, land_thru, send_sems, recv_sems, after)


def _exchange_begin(src, *, scatter, name):
    if scatter:
        land = lax.empty(src.shape, src.dtype)
    else:
        land = _place_own(lax.empty((N_DEV,) + src.shape, src.dtype), src, scatter=False, name=name + "_own")
    *handle, token = _exchange_start(src, land, scatter=scatter, name=name + "_start")
    return tuple(handle), token[0, 0]


_SIBLING_MASK = 1
_CHIP_MASKS = (2, 4, 6)
_DIRECT_MASKS = (_SIBLING_MASK,) + _CHIP_MASKS


def _direct_copy(src_ref, land_ref, send_sems, recv_sems, me, a, j, arrival):
    p = _DIRECT_MASKS[j]
    peer = _peer_coords(me, p)
    sem = a * len(_DIRECT_MASKS) + j
    return pltpu.make_async_remote_copy(
        src_ref=src_ref, dst_ref=land_ref.at[_flat_id(peer) if arrival else _flat_id(me)],
        send_sem=send_sems.at[sem], recv_sem=recv_sems.at[sem], device_id=peer, device_id_type=MESH_ID)


def _relay_copy(land_ref, send_sems, recv_sems, me, a, j, arrival):
    sibling = _peer_coords(me, _SIBLING_MASK)
    holder = sibling if arrival else me
    slab = land_ref.at[_flat_id(_peer_coords(holder, _CHIP_MASKS[j]))]
    sem = a * len(_CHIP_MASKS) + j
    return pltpu.make_async_remote_copy(
        src_ref=slab, dst_ref=slab, send_sem=send_sems.at[sem], recv_sem=recv_sems.at[sem],
        device_id=sibling, device_id_type=MESH_ID)


def _hbm_like(arrs):
    return tuple(pltpu.HBM(a.shape, a.dtype) for a in arrs)


def _gather_direct_start(srcs, lands, *, name):
    k = len(srcs)

    def body(*refs):
        send_sems, recv_sems = refs[2 * k:2 * k + 2]
        me = _my_coords()
        for a in range(k):
            for j in range(len(_DIRECT_MASKS)):
                _direct_copy(refs[a], refs[k + a], send_sems, recv_sems, me, a, j, False).start()
        refs[-1][...] = jnp.zeros_like(refs[-1])

    nsem = k * len(_DIRECT_MASKS)
    hbm_in = [pltpu.with_memory_space_constraint(a, pltpu.HBM) for a in (*srcs, *lands)]
    return pl.pallas_call(
        body,
        name=name,
        out_shape=(pltpu.SemaphoreType.DMA((nsem,)), pltpu.SemaphoreType.DMA((nsem,)),
                   *_hbm_like(srcs), *_hbm_like(lands), jax.ShapeDtypeStruct((8, 128), F32)),
        in_specs=(_HBM,) * (2 * k),
        out_specs=(_SEM, _SEM) + (_HBM,) * (2 * k) + (pl.BlockSpec(memory_space=pltpu.VMEM),),
        input_output_aliases={i: 2 + i for i in range(2 * k)},
        compiler_params=pltpu.CompilerParams(has_side_effects=_EFFECT),
    )(*hbm_in)


def _gather_direct_wait(send_sems, recv_sems, srcs, lands, after, *, name):
    k = len(srcs)

    def body(*refs):
        send_sems, recv_sems = refs[2 * k:2 * k + 2]
        me = _my_coords()
        for a in range(k):
            for j in range(len(_DIRECT_MASKS)):
                _direct_copy(refs[a], refs[k + a], send_sems, recv_sems, me, a, j, False).wait_send()
                _direct_copy(refs[a], refs[k + a], send_sems, recv_sems, me, a, j, True).wait_recv()

    return pl.pallas_call(
        body,
        name=name,
        out_shape=(*_hbm_like(srcs), *_hbm_like(lands)),
        in_specs=(_HBM,) * (2 * k) + (_SEM, _SEM, pl.BlockSpec(memory_space=pl.ANY)),
        out_specs=(_HBM,) * (2 * k),
        input_output_aliases={i: i for i in range(2 * k)},
        compiler_params=pltpu.CompilerParams(has_side_effects=_EFFECT),
    )(*srcs, *lands, send_sems, recv_sems, after)[k:]


def _gather_relay_start(lands, *, name):
    k = len(lands)

    def body(*refs):
        send_sems, recv_sems = refs[k:k + 2]
        me = _my_coords()
        for a in range(k):
            for j in range(len(_CHIP_MASKS)):
                _relay_copy(refs[a], send_sems, recv_sems, me, a, j, False).start()
        refs[-1][...] = jnp.zeros_like(refs[-1])

    nsem = k * len(_CHIP_MASKS)
    return pl.pallas_call(
        body,
        name=name,
        out_shape=(pltpu.SemaphoreType.DMA((nsem,)), pltpu.SemaphoreType.DMA((nsem,)),
                   *_hbm_like(lands), jax.ShapeDtypeStruct((8, 128), F32)),
        in_specs=(_HBM,) * k,
        out_specs=(_SEM, _SEM) + (_HBM,) * k + (pl.BlockSpec(memory_space=pltpu.VMEM),),
        input_output_aliases={i: 2 + i for i in range(k)},
        compiler_params=pltpu.CompilerParams(has_side_effects=_EFFECT),
    )(*lands)


def _gather_relay_wait(send_sems, recv_sems, lands, after, *, name):
    k = len(lands)

    def body(*refs):
        send_sems, recv_sems = refs[k:k + 2]
        me = _my_coords()
        for a in range(k):
            for j in range(len(_CHIP_MASKS)):
                _relay_copy(refs[a], send_sems, recv_sems, me, a, j, False).wait_send()
                _relay_copy(refs[a], send_sems, recv_sems, me, a, j, True).wait_recv()

    return pl.pallas_call(
        body,
        name=name,
        out_shape=_hbm_like(lands),
        in_specs=(_HBM,) * k + (_SEM, _SEM, pl.BlockSpec(memory_space=pl.ANY)),
        out_specs=(_HBM,) * k,
        input_output_aliases={i: i for i in range(k)},
        compiler_params=pltpu.CompilerParams(has_side_effects=_EFFECT),
    )(*lands, send_sems, recv_sems, after)


def _gather_group_begin(srcs, *, name):
    lands = [_place_own(lax.empty((N_DEV,) + s.shape, s.dtype), s, scatter=False, name=f"{name}_own{i}")
             for i, s in enumerate(srcs)]
    k = len(srcs)
    out = _gather_direct_start(srcs, lands, name=name + "_start")
    return (out[0], out[1], out[2:2 + k], out[2 + k:2 + 2 * k]), out[-1][0, 0]


def _gather_group_relay(handle, after, *, name):
    lands = _gather_direct_wait(*handle, after, name=name + "_landed")
    out = _gather_relay_start(lands, name=name + "_relay")
    return (out[0], out[1], out[2:-1]), out[-1][0, 0]


def _gather_group_end(handle, after, *, name):
    return _gather_relay_wait(*handle, after, name=name + "_done")


def _allreduce_small(flat, *, name):
    r, c = flat.shape
    rc = r // N_DEV
    assert rc * N_DEV == r and rc % F32_SUBLANES == 0, r

    def body(in_ref, out_ref, buf, send1, recv1, send2, recv2):
        me = _my_coords()
        my_id = _flat_id(me)

        def rows(dev_id):
            return pl.ds(pl.multiple_of(dev_id * rc, F32_SUBLANES), rc)

        def exchange(copies):
            for send, _ in copies:
                send.start()
            for _, recv in copies:
                recv.wait_recv()
            for send, _ in copies:
                send.wait_send()

        def scatter_copy(p):
            peer = _peer_coords(me, p)
            common = dict(src_ref=in_ref.at[rows(_flat_id(peer))], send_sem=send1.at[p - 1],
                          recv_sem=recv1.at[p - 1], device_id=peer, device_id_type=MESH_ID)
            return (pltpu.make_async_remote_copy(dst_ref=buf.at[my_id], **common),
                    pltpu.make_async_remote_copy(dst_ref=buf.at[_flat_id(peer)], **common))

        def gather_copy(p):
            peer = _peer_coords(me, p)
            common = dict(src_ref=out_ref.at[rows(my_id)], send_sem=send2.at[p - 1],
                          recv_sem=recv2.at[p - 1], device_id=peer, device_id_type=MESH_ID)
            return (pltpu.make_async_remote_copy(dst_ref=out_ref.at[rows(my_id)], **common),
                    pltpu.make_async_remote_copy(dst_ref=out_ref.at[rows(_flat_id(peer))], **common))

        buf[my_id] = in_ref[rows(my_id), :]
        exchange([scatter_copy(p) for p in range(1, N_DEV)])
        acc = buf[0]
        for q in range(1, N_DEV):
            acc = acc + buf[q]
        out_ref[rows(my_id), :] = acc
        exchange([gather_copy(p) for p in range(1, N_DEV)])

    vmem = pl.BlockSpec(memory_space=pltpu.VMEM)
    return pl.pallas_call(
        body,
        in_specs=[vmem],
        out_specs=vmem,
        out_shape=jax.ShapeDtypeStruct((r, c), F32),
        scratch_shapes=[pltpu.VMEM((N_DEV, rc, c), F32)] + [pltpu.SemaphoreType.DMA((N_DEV - 1,))] * 4,
        compiler_params=pltpu.CompilerParams(vmem_limit_bytes=VMEM_LIMIT_BYTES),
        name=name,
    )(flat)


WEIGHT_NAMES = ("mix_norm_g", "w_in", "sink", "conv_dw_w", "conv_dw_b", "conv_ln_g", "conv_ln_b",
                "sgu_ln_g", "sgu_ln_b", "sgu_w", "sgu_b", "w_out", "ffn_norm_g", "w_gate", "w_up",
                "w_down", "final_norm_g")
SHARDED = ("w_in", "conv_dw_w", "w_out", "w_gate", "w_up", "w_down")
SMALL = tuple(n for n in WEIGHT_NAMES if n not in ("w_in", "w_out", "w_gate", "w_up", "w_down"))


PACKED = SMALL + ("loss",)


def _pack_small(parts):
    flat = jnp.concatenate([parts[n].reshape(-1) for n in PACKED])
    pad = (-flat.shape[0]) % (N_DEV * F32_SUBLANES * LANES)
    return jnp.pad(flat, (0, pad)).reshape(-1, 128)


def _unpack_small(packed, shapes):
    flat = packed.reshape(-1)
    out, pos = {}, 0
    for n in PACKED:
        size = math.prod(shapes[n])
        out[n] = flat[pos:pos + size].reshape(shapes[n])
        pos += size
    return out


def kernel(x, mix_norm_g, w_in, sink, conv_dw_w, conv_dw_b, conv_ln_g, conv_ln_b, sgu_ln_g, sgu_ln_b, sgu_w, sgu_b, w_out, ffn_norm_g, w_gate, w_up, w_down, final_norm_g, loss_target, m_mix_norm_g, m_w_in, m_sink, m_conv_dw_w, m_conv_dw_b, m_conv_ln_g, m_conv_ln_b, m_sgu_ln_g, m_sgu_ln_b, m_sgu_w, m_sgu_b, m_w_out, m_ffn_norm_g, m_w_gate, m_w_up, m_w_down, m_final_norm_g, v_mix_norm_g, v_w_in, v_sink, v_conv_dw_w, v_conv_dw_b, v_conv_ln_g, v_conv_ln_b, v_sgu_ln_g, v_sgu_ln_b, v_sgu_w, v_sgu_b, v_w_out, v_ffn_norm_g, v_w_gate, v_w_up, v_w_down, v_final_norm_g):
    w = dict(mix_norm_g=mix_norm_g, w_in=w_in, sink=sink, conv_dw_w=conv_dw_w, conv_dw_b=conv_dw_b,
             conv_ln_g=conv_ln_g, conv_ln_b=conv_ln_b, sgu_ln_g=sgu_ln_g, sgu_ln_b=sgu_ln_b, sgu_w=sgu_w,
             sgu_b=sgu_b, w_out=w_out, ffn_norm_g=ffn_norm_g, w_gate=w_gate, w_up=w_up, w_down=w_down,
             final_norm_g=final_norm_g)
    mom_m = dict(zip(WEIGHT_NAMES, (m_mix_norm_g, m_w_in, m_sink, m_conv_dw_w, m_conv_dw_b, m_conv_ln_g,
                                    m_conv_ln_b, m_sgu_ln_g, m_sgu_ln_b, m_sgu_w, m_sgu_b, m_w_out,
                                    m_ffn_norm_g, m_w_gate, m_w_up, m_w_down, m_final_norm_g)))
    mom_v = dict(zip(WEIGHT_NAMES, (v_mix_norm_g, v_w_in, v_sink, v_conv_dw_w, v_conv_dw_b, v_conv_ln_g,
                                    v_conv_ln_b, v_sgu_ln_g, v_sgu_ln_b, v_sgu_w, v_sgu_b, v_w_out,
                                    v_ffn_norm_g, v_w_gate, v_w_up, v_w_down, v_final_norm_g)))

    _, t, d = x.shape
    depth = w_in.shape[0]
    cfg = Cfg(d, t)
    ff = w_gate.shape[2] * N_DEV
    my_id = _flat_id(_my_coords())
    xs = x[0]
    target = loss_target[0]

    tm = _pick(t, (1024, 512))
    tr = _pick(t, (256, 128))
    tb = _pick(t, (256, 128))
    tn_in = _pick(cfg.inw, (896, 512, 448))
    tn_ff = _pick(ff, (512, 1408, 704))
    tk_ff = ff
    tk_in = cfg.inw
    tn_d = _pick(d, (512,))
    tk_t = _pick(t, (2048, 1024, 512))
    tm_in = _pick(cfg.inw, (896, 448))
    tm_ff = _pick(ff, (1408, 704))
    tn_dw = _pick(d, (1024,))

    tabs = _rope_tables(t)

    cflat = conv_dw_w.reshape(-1)
    cshard = jnp.pad(cflat, (0, (-cflat.shape[0]) % (8 * 128))).reshape(-1, 128)
    dw_all = _exchange([cshard], scatter=False, name="gather_conv_w")[0]
    dw_all = dw_all.reshape(N_DEV, -1)[:, :cflat.shape[0]].reshape(N_DEV, depth, CONV_KERNEL, -1)
    dw_all = dw_all.transpose(1, 2, 0, 3).reshape(depth, CONV_KERNEL, cfg.conv)
    dw_pad = jnp.pad(dw_all, ((0, 0), (0, 1), (0, 0)))

    def row(v):
        return v.reshape(1, -1)

    first, rest = ("win",), ("wo", "wg", "wu", "wd")

    def gather_begin(l, names, zero):
        shards = dict(win=w_in[l].T, wo=w_out[l], wg=w_gate[l].T, wu=w_up[l].T, wd=w_down[l])
        handle, started = _gather_group_begin([(shards[k] + zero).astype(BF16) for k in names],
                                              name=f"gather_{names[0]}_{l}")
        return handle, zero + started

    def gather_end(handle, names, after, l):
        full = _gather_group_end(handle, after, name=f"gather_{names[0]}_{l}")
        return {k: f.reshape(-1, d) for k, f in zip(names, full)}

    saved = []
    scatters = [None] * depth
    dw_pad, started = lax.optimization_barrier((dw_pad, jnp.zeros((), F32)))
    landing_first, started = gather_begin(0, first, started)
    landing_wo, started = gather_begin(0, rest[:1], started)
    landing_ffn, started = gather_begin(0, rest[1:], started)
    relayed = None
    for l in range(depth):
        h = _rms_fwd(xs, row(mix_norm_g[l]) + started, tr=tr, name="mix_norm")
        if l == 0:
            relayed, _ = _gather_group_relay(landing_first, h, name="gather_win_0")
            wts = gather_end(relayed, first, h, 0)
        else:
            wts = gather_end(relayed, first + rest, h, l)
        win_t = wts["win"]
        started = jnp.zeros((), F32)
        if l + 1 < depth:
            win_t, started = lax.optimization_barrier((win_t, started))
            landing_next, started = gather_begin(l + 1, first + rest, started)
            h, started = lax.optimization_barrier((h, started))
        z = _in_proj(h, win_t, tabs, cfg, tm=tm, tn=tn_in, name="in_proj")
        qk = z
        mix = _attn_fwd(qk, z, sink[l], cfg, name="attn_fwd")
        mix, conv_y = _conv_fwd(z, mix, dw_pad[l], row(conv_dw_b[l]), row(conv_ln_g[l]) + started,
                                row(conv_ln_b[l]), cfg, tb=tb, name="conv_fwd")
        ws_b = sgu_w[l].astype(BF16)
        bs_c = sgu_b[l][:, :, None]
        mix = _sgu_fwd(z, mix, row(sgu_ln_g[l]), row(sgu_ln_b[l]), ws_b, bs_c, cfg, name="sgu_fwd")
        if l == 0:
            relayed, _ = _gather_group_relay(landing_wo, mix, name="gather_wo_0")
            wts.update(gather_end(relayed, rest[:1], mix, 0))
        x1 = _matmul(mix, wts["wo"], mode="nn", tm=tm, tn=tn_d, tk=d, epilogue=_ep_add, extras=(xs,),
                     out_dtypes=[F32], name="out_proj")[0]
        h2 = _rms_fwd(x1, row(ffn_norm_g[l]), tr=tr, name="ffn_norm")
        if l == 0:
            relayed, _ = _gather_group_relay(landing_ffn, h2, name="gather_wg_0")
            wts.update(gather_end(relayed, rest[1:], h2, 0))
        wo, wg_t, wu_t, wd = wts["wo"], wts["wg"], wts["wu"], wts["wd"]
        gate, up, act = _ffn_up(h2, wg_t, wu_t, tm=tm, tn=tn_ff, name="ffn_up")
        if 0 < l < depth - 1:
            relayed, zero = _gather_group_relay(landing_next, act, name=f"gather_win_{l + 1}")
            act, zero = lax.optimization_barrier((act, zero))
            started = started + zero
        x2 = _matmul(act, wd, mode="nn", tm=tm, tn=tn_d, tk=tk_ff, epilogue=_ep_add, extras=(x1,),
                     out_dtypes=[F32], name="ffn_down")[0]
        if l == 0 and depth > 1:
            relayed, zero = _gather_group_relay(landing_next, x2, name="gather_win_1")
            started = started + zero
        saved.append(dict(x0=xs, h=h, z=z, qk=qk, mix=mix, conv_y=conv_y, x1=x1, h2=h2, gate=gate, up=up,
                          act=act, win_t=win_t, wg_t=wg_t, wu_t=wu_t, wo=wo, wd=wd, ws_b=ws_b, bs_c=bs_c))
        xs = x2

    dx, dxb, head = _loss_head(xs, row(final_norm_g), target, tr=tr, name="loss_head")

    def scatter_begin(grad, n, l):
        handle, zero = _exchange_begin(grad.reshape(N_DEV, -1, d), scatter=True, name=f"scatter_{n}_{l}")
        scatters[l][n] = handle
        return zero

    small = {n: [None] * depth for n in SMALL if n != "final_norm_g"}
    big = {n: [None] * depth for n in ("w_in", "w_out", "w_gate", "w_up", "w_down")}
    for l in reversed(range(depth)):
        s = saved[l]
        dgate, dup = _matmul(dxb, s["wd"], mode="nt", tm=tm, tn=tn_ff, tk=d, epilogue=_ep_swiglu_bwd,
                             extras=(s["gate"], s["up"]), out_dtypes=[BF16, BF16], name="ffn_down_bwd")
        dwd = _matmul(s["act"], dxb, mode="tn", tm=tm_ff, tn=tn_dw, tk=tk_t, epilogue=_ep_plain,
                      out_dtypes=[BF16], name="ffn_down_wgrad")[0]
        scatters[l] = {}
        started = scatter_begin(dwd, "w_down", l)
        dh2 = _matmul(dgate, s["wg_t"], mode="nn", tm=tm, tn=tn_d, tk=tk_ff, epilogue=_ep_plain,
                      out_dtypes=[F32], name="ffn_gate_bwd")[0]
        dh2 = _matmul(dup, s["wu_t"], mode="nn", tm=tm, tn=tn_d, tk=tk_ff, epilogue=_ep_add, extras=(dh2,),
                      out_dtypes=[BF16], name="ffn_up_bwd")[0]
        dwg_t = _matmul(dgate, s["h2"], mode="tn", tm=tm_ff, tn=tn_dw, tk=tk_t, epilogue=_ep_plain,
                        out_dtypes=[BF16], name="ffn_gate_wgrad")[0]
        dwu_t = _matmul(dup, s["h2"], mode="tn", tm=tm_ff, tn=tn_dw, tk=tk_t, epilogue=_ep_plain,
                        out_dtypes=[BF16], name="ffn_up_wgrad")[0]
        started = started + scatter_begin(dwg_t, "w_gate", l) + scatter_begin(dwu_t, "w_up", l)
        dx1, dx1b, dg2 = _rms_bwd(dh2, s["x1"], row(ffn_norm_g[l]) + started, dx, tr=tr, name="ffn_norm_bwd")

        dmix = _matmul(dx1b, s["wo"], mode="nt", tm=tm, tn=tn_d, tk=d, epilogue=_ep_plain,
                       out_dtypes=[BF16], name="out_proj_bwd")[0]
        dwo = _matmul(s["mix"], dx1b, mode="tn", tm=_pick(d, (1024,)), tn=tn_dw, tk=tk_t, epilogue=_ep_plain,
                      out_dtypes=[BF16], name="out_proj_wgrad")[0]
        started = scatter_begin(dwo, "w_out", l)
        dq, dk_acc, dv_acc, dsink = _attn_bwd(s["qk"], s["z"], dmix, sink[l], tabs, cfg, name="attn_bwd")
        dk, dv = _kv_finish(dk_acc, dv_acc, tabs, cfg, name="attn_bwd_kv")
        da, dcg, dcw, cst = _conv_bwd(s["z"], s["conv_y"], dmix, dw_pad[l], row(conv_ln_g[l]) + started,
                                      row(conv_ln_b[l]), cfg, tb=tb, name="conv_bwd")
        duv, dws, dbs, sst = _sgu_bwd(s["z"], dmix, row(sgu_ln_g[l]), row(sgu_ln_b[l]), s["ws_b"], s["bs_c"],
                                      cfg, name="sgu_bwd")
        dz = jnp.concatenate([dq, dk, dv, da, dcg, duv], axis=1)
        dh = _matmul(dz, s["win_t"], mode="nn", tm=tm, tn=tn_d, tk=tk_in, epilogue=_ep_plain,
                     out_dtypes=[BF16], name="in_proj_bwd")[0]
        dwin_t = _matmul(dz, s["h"], mode="tn", tm=tm_in, tn=tn_dw, tk=tk_t, epilogue=_ep_plain,
                         out_dtypes=[BF16], name="in_proj_wgrad")[0]
        started = scatter_begin(dwin_t, "w_in", l)
        dx, dxb, dg1 = _rms_bwd(dh, s["x0"], row(mix_norm_g[l]) + started, dx1, tr=tr, name="mix_norm_bwd")

        small["mix_norm_g"][l] = dg1[0]
        small["ffn_norm_g"][l] = dg2[0]
        small["sink"][l] = dsink[:, :Q_PER_KV, 0].reshape(-1)
        small["conv_dw_w"][l] = dcw[:CONV_KERNEL]
        small["conv_dw_b"][l] = cst[0]
        small["conv_ln_g"][l] = cst[1]
        small["conv_ln_b"][l] = cst[2]
        small["sgu_ln_g"][l] = sst[0]
        small["sgu_ln_b"][l] = sst[1]
        small["sgu_w"][l] = dws
        small["sgu_b"][l] = dbs[:, :, 0]

    grads, deltas, new_m, new_v = {}, {}, {}, {}

    transposed = ("w_in", "w_gate", "w_up")

    def adamw(n, grad):
        flip = (lambda a: jnp.swapaxes(a, 1, 2)) if n in transposed else (lambda a: a)
        shape = flip(w[n]).shape
        view = lambda a: flip(a).reshape(-1, shape[-1])
        out = _adamw(view(w[n]), grad.reshape(-1, shape[-1]), view(mom_m[n]), view(mom_v[n]), name="adamw")
        grads[n] = flip(grad.reshape(shape))
        deltas[n], new_m[n], new_v[n] = [flip(o.reshape(shape)) for o in out]

    after = dx
    for n in ("w_down", "w_gate", "w_up", "w_out", "w_in"):
        for l in reversed(range(depth)):
            own, landed = _exchange_wait(scatters[l][n], after, scatter=True, name=f"scatter_{n}_{l}_wait")
            big[n][l] = _sum_shards(own, landed, name="sum_grads")
        adamw(n, jnp.stack(big[n]))
        after = new_v[n]

    parts = {n: jnp.stack(v) for n, v in small.items()}
    parts["final_norm_g"] = head[0]
    parts["loss"] = head[1, :1]
    shapes = {n: parts[n].shape for n in PACKED}
    summed = _unpack_small(_allreduce_small(_pack_small(parts), name="allreduce_small"), shapes)
    loss = summed["loss"][0]
    cshard_w = conv_dw_w.shape[2]
    summed["conv_dw_w"] = lax.dynamic_slice_in_dim(summed["conv_dw_w"], my_id * cshard_w, cshard_w, axis=2)
    for n in SMALL:
        adamw(n, summed[n])

    return (loss, dx[None], *[grads[n] for n in WEIGHT_NAMES], *[deltas[n] for n in WEIGHT_NAMES],
            *[new_m[n] for n in WEIGHT_NAMES], *[new_v[n] for n in WEIGHT_NAMES])
```
